```python
import jax, jax.numpy as jnp
from jax import lax
import numpy as np

D_MODEL = 1024
BATCH = 4
SEQ = 4096
DEPTH = 2
DEC_BATCH = 32
DEC_SEQ = 64
PAST_LEN = 2048

CHUNK = 64
RWKV_HEADS = 8
RWKV_HD = 64
RWKV_W = RWKV_HEADS * RWKV_HD
DECAY_LORA = 64
AAA_LORA = 64
GATE_LORA = 128
RWKV_SIZES = (RWKV_W, RWKV_W, RWKV_W, DECAY_LORA, AAA_LORA, GATE_LORA)
RWKV_COLS = 3 * RWKV_W + DECAY_LORA + AAA_LORA + GATE_LORA
GN_EPS = 64e-5
SWA_HEADS = 8
SWA_KV_HEADS = 2
SWA_GROUP = SWA_HEADS // SWA_KV_HEADS
SWA_HD = 64
WINDOW = 128
BAND_CHUNKS = WINDOW // CHUNK
MEM_TOKENS = 256
MEM_HEADS = 4
MEM_HD = 128
MEM_W = MEM_HEADS * MEM_HD
N_BRANCH = 3
BR_W = 512
IN_SIZES = (RWKV_COLS, SWA_HEADS * SWA_HD, SWA_KV_HEADS * SWA_HD, SWA_KV_HEADS * SWA_HD, MEM_W, N_BRANCH * D_MODEL)
IN_COLS = RWKV_COLS + SWA_HEADS * SWA_HD + 2 * SWA_KV_HEADS * SWA_HD + MEM_W + N_BRANCH * D_MODEL
D_FF = 2816
CONV_W = 3
RMS_EPS = 1e-6

kernel_name = 'hybrid_rwkv7_swa_sink_mem_convffn_step'


def _split_points(sizes):
    pts, acc = [], 0
    for s in sizes[:-1]:
        acc += s
        pts.append(acc)
    return pts


def _rms(x, g):
    xf = x.astype(jnp.float32)
    y = xf * lax.rsqrt(jnp.mean(xf * xf, axis=-1, keepdims=True) + RMS_EPS)
    return (y * g.astype(jnp.float32)).astype(x.dtype)


def _rwkv(p, shift_in, s0, mu, w0, w2, a0, a2, g2, k_k, k_a, r_k, ln_g, ln_b):
    B, T = p.shape[0], p.shape[1]
    p_pad = jnp.concatenate([shift_in.astype(p.dtype), p], axis=1)
    pm = p + (p_pad[:, :-1] - p) * mu
    r, k, v, xw, xa, xg = jnp.split(pm, _split_points(RWKV_SIZES), axis=-1)
    w = -jax.nn.softplus(-(w0 + jnp.tanh(xw) @ w2)) - 0.5
    decay = jnp.exp(-jnp.exp(w.astype(jnp.float32)))
    a = jax.nn.sigmoid(a0 + xa @ a2)
    g = jax.nn.sigmoid(xg) @ g2
    hs = lambda z: z.reshape(B, T, RWKV_HEADS, RWKV_HD).astype(jnp.float32)
    kk = hs(k * k_k)
    kk = kk / jnp.maximum(jnp.sqrt(jnp.sum(kk * kk, axis=-1, keepdims=True)), 1e-12)
    k = k * (1.0 + (a - 1.0) * k_a)
    rh, kh, vh, ah, wh = hs(r), hs(k), hs(v), hs(a), hs(decay)
    tm = lambda z: jnp.moveaxis(z, 1, 0)

    def step(S, inp):
        r_t, w_t, k_t, v_t, kk_t, a_t = inp
        sa = jnp.einsum('bhvk,bhk->bhv', S, -kk_t)
        S = S * w_t[:, :, None, :] + sa[..., None] * (kk_t * a_t)[:, :, None, :] + v_t[..., None] * k_t[:, :, None, :]
        return S, jnp.einsum('bhvk,bhk->bhv', S, r_t)

    s_fin, o = lax.scan(step, s0.astype(jnp.float32), (tm(rh), tm(wh), tm(kh), tm(vh), tm(kk), tm(ah)))
    o = jnp.moveaxis(o, 0, 1)
    mean = jnp.mean(o, axis=-1, keepdims=True)
    var = jnp.mean(jnp.square(o - mean), axis=-1, keepdims=True)
    o = ((o - mean) * lax.rsqrt(var + GN_EPS)).reshape(B, T, RWKV_W) * ln_g.astype(jnp.float32) + ln_b.astype(jnp.float32)
    bonus = jnp.sum(rh * kh * r_k.astype(jnp.float32), axis=-1, keepdims=True) * vh
    o = (o + bonus.reshape(B, T, RWKV_W)) * g.astype(jnp.float32)
    return o.astype(p.dtype), s_fin.astype(s0.dtype), p_pad[:, -1:]


def _attend_sink(q, k, v, dist, valid, sink):
    s = jnp.einsum('bcqhgd,bckhd->bchgqk', q, k).astype(jnp.float32) * (SWA_HD ** -0.5)
    slopes = jnp.exp2(-8.0 * jnp.arange(1, SWA_HEADS + 1, dtype=jnp.float32) / SWA_HEADS)
    s = s - slopes.reshape(SWA_KV_HEADS, SWA_GROUP, 1, 1) * dist
    if valid is not None:
        s = jnp.where(valid[None, :, None, None, None, :], s, -jnp.inf)
    sink_l = jnp.broadcast_to(sink.astype(jnp.float32).reshape(1, 1, SWA_KV_HEADS, SWA_GROUP, 1, 1), s.shape[:-1] + (1,))
    pr = jax.nn.softmax(jnp.concatenate([s, sink_l], axis=-1), axis=-1)[..., :-1]
    return jnp.einsum('bchgqk,bckhd->bcqhgd', pr.astype(v.dtype), v)


def _swa_prompt(q, k, v, sink):
    B, T = q.shape[0], q.shape[1]
    NC = T // CHUNK
    qc = q.reshape(B, NC, CHUNK, SWA_KV_HEADS, SWA_GROUP, SWA_HD)

    def band(z):
        zc = z.reshape(B, NC, CHUNK, SWA_KV_HEADS, SWA_HD)
        zp = jnp.pad(zc, ((0, 0), (BAND_CHUNKS, 0), (0, 0), (0, 0), (0, 0)))
        return jnp.concatenate([zp[:, i:i + NC] for i in range(BAND_CHUNKS + 1)], axis=2)

    KB = (BAND_CHUNKS + 1) * CHUNK
    qi = jnp.arange(CHUNK)
    kj = jnp.arange(KB)
    dist = jnp.abs(BAND_CHUNKS * CHUNK + qi[:, None] - kj[None, :]).astype(jnp.float32)
    valid = (jnp.arange(NC)[:, None] - BAND_CHUNKS + kj[None, :] // CHUNK) >= 0
    o = _attend_sink(qc, band(k), band(v), dist, valid, sink)
    return o.reshape(B, T, SWA_HEADS * SWA_HD)


def _swa_sample(q, k, v, k_cache, v_cache, sink):
    B, T = q.shape[0], q.shape[1]
    L = k_cache.shape[1]
    kf = jnp.concatenate([k_cache.astype(k.dtype), k], axis=1)
    vf = jnp.concatenate([v_cache.astype(v.dtype), v], axis=1)
    dist = jnp.abs(L + jnp.arange(T)[:, None] - jnp.arange(L + T)[None, :]).astype(jnp.float32)
    o = _attend_sink(q.reshape(B, 1, T, SWA_KV_HEADS, SWA_GROUP, SWA_HD), kf[:, None], vf[:, None], dist, None, sink)
    return o.reshape(B, T, SWA_HEADS * SWA_HD), kf[:, -L:], vf[:, -L:]


def _mem_kv(mem, g, w_kv, kn_g):
    B, M = mem.shape[0], mem.shape[1]
    mk, mv = jnp.split(_rms(mem, g) @ w_kv, 2, axis=-1)
    mk = _rms(mk.reshape(B, M, MEM_HEADS, MEM_HD), kn_g)
    return mk, mv.reshape(B, M, MEM_HEADS, MEM_HD)


def _mem_attend(q, mk, mv):
    s = jnp.einsum('bthd,bmhd->bhtm', q, mk.astype(q.dtype)).astype(jnp.float32) * (MEM_HD ** -0.5)
    pr = jax.nn.softmax(s, axis=-1)
    return jnp.einsum('bhtm,bmhd->bthd', pr.astype(q.dtype), mv.astype(q.dtype))


def _layer(x, lw, mem_k, mem_v, shift_in, s0, conv_in, swa_k_cache, swa_v_cache):
    B, T = x.shape[0], x.shape[1]
    h = _rms(x, lw['norm1_g'])
    proj = h @ lw['w_in']
    p_rwkv, q, k, v, qm, gates = jnp.split(proj, _split_points(IN_SIZES), axis=-1)
    o_a, s_new, shift_new = _rwkv(p_rwkv, shift_in, s0, lw['rwkv_mu'], lw['rwkv_w0'], lw['rwkv_w2'], lw['rwkv_a0'],
                                  lw['rwkv_a2'], lw['rwkv_g2'], lw['rwkv_kk'], lw['rwkv_ka'], lw['rwkv_rk'],
                                  lw['rwkv_ln_g'], lw['rwkv_ln_b'])
    q = _rms(q.reshape(B, T, SWA_HEADS, SWA_HD), lw['swa_qn_g'])
    k = _rms(k.reshape(B, T, SWA_KV_HEADS, SWA_HD), lw['swa_kn_g'])
    v = v.reshape(B, T, SWA_KV_HEADS, SWA_HD)
    if swa_k_cache is None:
        o_b = _swa_prompt(q, k, v, lw['swa_sink'])
        kw, vw = k[:, -WINDOW:], v[:, -WINDOW:]
    else:
        o_b, kw, vw = _swa_sample(q, k, v, swa_k_cache, swa_v_cache, lw['swa_sink'])
    qm = _rms(qm.reshape(B, T, MEM_HEADS, MEM_HD), lw['mem_qn_g'])
    o_m = _mem_attend(qm, mem_k, mem_v).reshape(B, T, MEM_W)
    br = jnp.einsum('btnc,ncd->btnd', jnp.stack([o_a, o_b, o_m], axis=2), lw['w_branch'])
    gt = jax.nn.sigmoid(gates.reshape(B, T, N_BRANCH, D_MODEL))
    x = x + jnp.sum(gt * br, axis=2) @ lw['w_out']
    h = _rms(x, lw['norm2_g'])
    a_in, u = jnp.split(h @ lw['w_up'], 2, axis=-1)
    a_pad = jnp.concatenate([conv_in.astype(a_in.dtype), a_in], axis=1)
    c = lw['conv_b'] + sum([a_pad[:, j:j + T] * lw['conv_w'][j] for j in range(CONV_W)])
    x = x + (jax.nn.gelu(c) * u) @ lw['w_down']
    return x, (kw, vw, s_new, shift_new, a_pad[:, -(CONV_W - 1):])


def setup_inputs(seed: int = 0) -> dict:
    key = jax.random.key(seed)
    ks = iter(jax.random.split(key, 48))
    nrm = lambda shape, scale: jax.random.normal(next(ks), shape, jnp.float32) * scale
    L = DEPTH
    swa_len = min(WINDOW, PAST_LEN)
    return {
        'x_prompt': nrm((BATCH, SEQ, D_MODEL), 1.0),
        'x_sample': nrm((DEC_BATCH, DEC_SEQ, D_MODEL), 1.0),
        'cache_swa_k': nrm((L, DEC_BATCH, swa_len, SWA_KV_HEADS, SWA_HD), 1.0),
        'cache_swa_v': nrm((L, DEC_BATCH, swa_len, SWA_KV_HEADS, SWA_HD), 1.0),
        'cache_mem_k': nrm((L, DEC_BATCH, MEM_TOKENS, MEM_HEADS, MEM_HD), 1.0),
        'cache_mem_v': nrm((L, DEC_BATCH, MEM_TOKENS, MEM_HEADS, MEM_HD), 1.0),
        'state_rwkv': nrm((L, DEC_BATCH, RWKV_HEADS, RWKV_HD, RWKV_HD), 0.1),
        'state_shift': nrm((L, DEC_BATCH, 1, RWKV_COLS), 1.0),
        'state_conv': nrm((L, DEC_BATCH, CONV_W - 1, D_FF), 1.0),
        'mem_prompt': nrm((BATCH, MEM_TOKENS, D_MODEL), 1.0),
        'norm1_g': 1.0 + nrm((L, D_MODEL), 0.02),
        'w_in': nrm((L, D_MODEL, IN_COLS), D_MODEL ** -0.5),
        'rwkv_mu': jax.random.uniform(next(ks), (L, RWKV_COLS), jnp.float32),
        'rwkv_w0': jax.random.uniform(next(ks), (L, RWKV_W), jnp.float32, -6.0, -1.0),
        'rwkv_w2': nrm((L, DECAY_LORA, RWKV_W), 0.1 * DECAY_LORA ** -0.5),
        'rwkv_a0': nrm((L, RWKV_W), 0.5),
        'rwkv_a2': nrm((L, AAA_LORA, RWKV_W), 0.5 * AAA_LORA ** -0.5),
        'rwkv_g2': nrm((L, GATE_LORA, RWKV_W), GATE_LORA ** -0.5),
        'rwkv_kk': 0.85 + nrm((L, RWKV_W), 0.02),
        'rwkv_ka': 1.0 + nrm((L, RWKV_W), 0.02),
        'rwkv_rk': nrm((L, RWKV_HEADS, RWKV_HD), 0.1),
        'rwkv_ln_g': 1.0 + nrm((L, RWKV_W), 0.02),
        'rwkv_ln_b': nrm((L, RWKV_W), 0.02),
        'swa_qn_g': 1.0 + nrm((L, SWA_HD), 0.02),
        'swa_kn_g': 1.0 + nrm((L, SWA_HD), 0.02),
        'swa_sink': nrm((L, SWA_HEADS), 0.5),
        'mem_norm_g': 1.0 + nrm((L, D_MODEL), 0.02),
        'w_mem_kv': nrm((L, D_MODEL, 2 * MEM_W), D_MODEL ** -0.5),
        'mem_qn_g': 1.0 + nrm((L, MEM_HD), 0.02),
        'mem_kn_g': 1.0 + nrm((L, MEM_HD), 0.02),
        'w_branch': nrm((L, N_BRANCH, BR_W, D_MODEL), BR_W ** -0.5),
        'w_out': nrm((L, D_MODEL, D_MODEL), D_MODEL ** -0.5),
        'norm2_g': 1.0 + nrm((L, D_MODEL), 0.02),
        'w_up': nrm((L, D_MODEL, 2 * D_FF), D_MODEL ** -0.5),
        'conv_w': nrm((L, CONV_W, D_FF), CONV_W ** -0.5),
        'conv_b': nrm((L, D_FF), 0.02),
        'w_down': nrm((L, D_FF, D_MODEL), D_FF ** -0.5),
    }


def reference(x_prompt, x_sample, cache_swa_k, cache_swa_v, cache_mem_k, cache_mem_v, state_rwkv, state_shift,
              state_conv, mem_prompt, norm1_g, w_in, rwkv_mu, rwkv_w0, rwkv_w2, rwkv_a0, rwkv_a2, rwkv_g2, rwkv_kk,
              rwkv_ka, rwkv_rk, rwkv_ln_g, rwkv_ln_b, swa_qn_g, swa_kn_g, swa_sink, mem_norm_g, w_mem_kv, mem_qn_g,
              mem_kn_g, w_branch, w_out, norm2_g, w_up, conv_w, conv_b, w_down):
    Bp = x_prompt.shape[0]
    dt = x_prompt.dtype
    yp, ys = x_prompt, x_sample
    swk_p, swv_p, mk_p, mv_p, rw_p, sh_p, cv_p = [], [], [], [], [], [], []
    swk_s, swv_s, rw_s, sh_s, cv_s = [], [], [], [], []
    for l in range(DEPTH):
        lw = {'norm1_g': norm1_g[l], 'w_in': w_in[l], 'rwkv_mu': rwkv_mu[l], 'rwkv_w0': rwkv_w0[l],
              'rwkv_w2': rwkv_w2[l], 'rwkv_a0': rwkv_a0[l], 'rwkv_a2': rwkv_a2[l], 'rwkv_g2': rwkv_g2[l],
              'rwkv_kk': rwkv_kk[l], 'rwkv_ka': rwkv_ka[l], 'rwkv_rk': rwkv_rk[l], 'rwkv_ln_g': rwkv_ln_g[l],
              'rwkv_ln_b': rwkv_ln_b[l], 'swa_qn_g': swa_qn_g[l], 'swa_kn_g': swa_kn_g[l], 'swa_sink': swa_sink[l],
              'mem_qn_g': mem_qn_g[l], 'w_branch': w_branch[l], 'w_out': w_out[l], 'norm2_g': norm2_g[l],
              'w_up': w_up[l], 'conv_w': conv_w[l], 'conv_b': conv_b[l], 'w_down': w_down[l]}
        mk, mv = _mem_kv(mem_prompt, mem_norm_g[l], w_mem_kv[l], mem_kn_g[l])
        yp, (kw, vw, s_new, sh_new, cv_new) = _layer(
            yp, lw, mk, mv,
            jnp.zeros((Bp, 1, RWKV_COLS), dt),
            jnp.zeros((Bp, RWKV_HEADS, RWKV_HD, RWKV_HD), dt),
            jnp.zeros((Bp, CONV_W - 1, D_FF), dt), None, None)
        swk_p.append(kw); swv_p.append(vw); mk_p.append(mk); mv_p.append(mv)
        rw_p.append(s_new); sh_p.append(sh_new); cv_p.append(cv_new)
        ys, (kw, vw, s_new, sh_new, cv_new) = _layer(
            ys, lw, cache_mem_k[l], cache_mem_v[l], state_shift[l], state_rwkv[l], state_conv[l],
            cache_swa_k[l], cache_swa_v[l])
        swk_s.append(kw); swv_s.append(vw); rw_s.append(s_new); sh_s.append(sh_new); cv_s.append(cv_new)
    return (yp, ys,
            jnp.stack(swk_p), jnp.stack(swv_p), jnp.stack(mk_p), jnp.stack(mv_p),
            jnp.stack(rw_p), jnp.stack(sh_p), jnp.stack(cv_p),
            jnp.stack(swk_s), jnp.stack(swv_s), jnp.stack(rw_s), jnp.stack(sh_s), jnp.stack(cv_s))
```

```python
import functools

import jax
import jax.numpy as jnp
from jax import lax
from jax.experimental import pallas as pl
from jax.experimental.pallas import tpu as pltpu

F32 = jnp.float32
BF16 = jnp.bfloat16
HIGHEST = lax.Precision.HIGHEST

D_MODEL = 1024
DEPTH = 2
CHUNK = 64
RWKV_HEADS = 8
RWKV_HD = 64
RWKV_W = 512
RWKV_COLS = 1792
GN_EPS = 64e-5
SWA_HEADS = 8
SWA_HD = 64
WINDOW = 128
MEM_TOKENS = 256
MEM_HEADS = 4
MEM_HD = 128
MEM_W = 512
N_BRANCH = 3
D_FF = 2816
CONV_W = 3
RMS_EPS = 1e-6

LANES = 128
PAIRS = RWKV_W // LANES
VMEM_LIMIT = 56 * 1024 * 1024
FF_BLOCK = D_FF // 2

C_Q = RWKV_COLS
C_K = C_Q + SWA_HEADS * SWA_HD
C_V = C_K + LANES
C_QM = C_V + LANES
C_GT = C_QM + MEM_W
IN_COLS = C_GT + N_BRANCH * D_MODEL


def _dot(a, b, dims=((1,), (0,)), exact=False):
    if exact:
        return lax.dot_general(a, b, (dims, ((), ())), precision=HIGHEST, preferred_element_type=F32)
    return lax.dot_general(a.astype(BF16), b.astype(BF16), (dims, ((), ())), preferred_element_type=F32)


_NT = ((1,), (1,))
_TN = ((0,), (0,))


def _seg_sum(x, e):
    hi = x.astype(BF16)
    lo = (x - hi.astype(F32)).astype(BF16)
    return jnp.dot(hi, e, preferred_element_type=F32) + jnp.dot(lo, e, preferred_element_type=F32)


def _sigmoid(x):
    return 1.0 / (1.0 + jnp.exp(-x))


def _const_spec(shape):
    n = len(shape)
    return pl.BlockSpec(shape, lambda *_: (0,) * n, pipeline_mode=pl.Buffered(1))


def _params():
    return pltpu.CompilerParams(dimension_semantics=("arbitrary", "arbitrary"), vmem_limit_bytes=VMEM_LIMIT)


def _in_kernel(x_ref, g1_ref, w_ref, qng_ref, kng_ref, mqg_ref, e_ref,
               p_ref, q_ref, k_ref, v_ref, qm_ref, gt_ref):
    G, L, D = x_ref.shape
    R = G * L
    x = x_ref[...].reshape(R, D)
    h = x * lax.rsqrt(jnp.mean(x * x, axis=-1, keepdims=True) + RMS_EPS) * g1_ref[...]
    hb = h.astype(BF16)
    e = e_ref[...]

    def proj(c0, c1):
        return jnp.dot(hb, w_ref[:, c0:c1], preferred_element_type=F32)

    def head_rms(z, gain):
        return z * lax.rsqrt(_seg_sum(z * z, e) * (1.0 / SWA_HD) + RMS_EPS) * gain

    p_ref[...] = proj(0, RWKV_COLS).reshape(G, L, RWKV_COLS)
    for s in range(SWA_HEADS * SWA_HD // LANES):
        z = proj(C_Q + s * LANES, C_Q + (s + 1) * LANES)
        q_ref[:, :, s * LANES:(s + 1) * LANES] = head_rms(z, qng_ref[...]).reshape(G, L, LANES)
    k_ref[...] = head_rms(proj(C_K, C_V), kng_ref[...]).reshape(G, L, LANES)
    v_ref[...] = proj(C_V, C_QM).reshape(G, L, LANES)
    for s in range(MEM_HEADS):
        z = proj(C_QM + s * MEM_HD, C_QM + (s + 1) * MEM_HD)
        z = z * lax.rsqrt(jnp.mean(z * z, axis=-1, keepdims=True) + RMS_EPS) * mqg_ref[...]
        qm_ref[:, :, s * MEM_HD:(s + 1) * MEM_HD] = z.reshape(G, L, MEM_HD)
    for s in range(N_BRANCH):
        z = proj(C_GT + s * D_MODEL, C_GT + (s + 1) * D_MODEL)
        gt_ref[:, :, s * D_MODEL:(s + 1) * D_MODEL] = _sigmoid(z).reshape(G, L, D_MODEL)


def _in_proj(x, g1, w_in_b, qng, kng, mqg, e, G, L):
    B, T, D = x.shape
    tile = lambda c: pl.BlockSpec((G, L, c), lambda i, j: (i, j, 0))
    out_cols = (RWKV_COLS, SWA_HEADS * SWA_HD, LANES, LANES, MEM_W, N_BRANCH * D_MODEL)
    return pl.pallas_call(
        _in_kernel,
        grid=(B // G, T // L),
        in_specs=[tile(D), _const_spec((1, D)), _const_spec((D, IN_COLS)), _const_spec((1, LANES)),
                  _const_spec((1, LANES)), _const_spec((1, MEM_HD)), _const_spec((LANES, LANES))],
        out_specs=[tile(c) for c in out_cols],
        out_shape=[jax.ShapeDtypeStruct((B, T, c), F32) for c in out_cols],
        compiler_params=_params(),
        name="in_proj",
    )(x, g1, w_in_b, qng, kng, mqg, e)


def _rwkv_kernel(p_ref, sh_ref, s0_ref, mu_ref, w0_ref, w2_ref, a0_ref, a2_ref, g2_ref, kk_ref, ka_ref,
                 rk_ref, lng_ref, lnb_ref, e_ref,
                 o_ref, sout_ref,
                 prev_scr, st_scr, r_scr, k_scr, v_scr, am_scr, b_scr, lw_scr, g_scr, o_scr):
    G, L, _ = p_ref.shape
    C = CHUNK
    nc = L // C
    j = pl.program_id(1)
    e = e_ref[...]

    @pl.when(j == 0)
    def _():
        st_scr[...] = s0_ref[...]
        prev_scr[...] = sh_ref[...]

    first_row = lax.broadcasted_iota(jnp.int32, (L, 1), 0) == 0
    for g in range(G):
        p = p_ref[g]
        shifted = jnp.where(first_row, prev_scr[g], pltpu.roll(p, 1, 0))
        pm = p + (shifted - p) * mu_ref[...]
        prev_scr[g] = p[L - 1:L, :]
        rows = slice(g * L, (g + 1) * L)
        r = pm[:, 0:RWKV_W]
        k = pm[:, RWKV_W:2 * RWKV_W]
        v = pm[:, 2 * RWKV_W:3 * RWKV_W]
        xwa = pm[:, 3 * RWKV_W:3 * RWKV_W + LANES]
        xg = pm[:, 3 * RWKV_W + LANES:RWKV_COLS]
        z = w0_ref[...] + _dot(jnp.tanh(xwa), w2_ref[...])
        w = -(jnp.maximum(-z, 0.0) + jnp.log(1.0 + jnp.exp(-jnp.abs(z)))) - 0.5
        a = _sigmoid(a0_ref[...] + _dot(xwa, a2_ref[...]))
        kkv = k * kk_ref[...]
        for s in range(PAIRS):
            sl = slice(s * LANES, (s + 1) * LANES)
            kks = kkv[:, sl]
            kkn = kks / jnp.maximum(jnp.sqrt(_seg_sum(kks * kks, e)), 1e-12)
            am_scr[rows, sl] = -kkn
            b_scr[rows, sl] = kkn * a[:, sl]
        r_scr[rows, :] = r
        k_scr[rows, :] = k * (1.0 + (a - 1.0) * ka_ref[...])
        v_scr[rows, :] = v
        lw_scr[rows, :] = -jnp.exp(w)
        g_scr[rows, :] = _dot(_sigmoid(xg), g2_ref[...])

    ri = lax.broadcasted_iota(jnp.int32, (C, C), 0)
    ci = lax.broadcasted_iota(jnp.int32, (C, C), 1)
    cumsum_mat = (ri >= ci).astype(F32)
    r2 = lax.broadcasted_iota(jnp.int32, (2 * C, 2 * C), 0)
    c2 = lax.broadcasted_iota(jnp.int32, (2 * C, 2 * C), 1)
    same_head = (r2 // C) == (c2 // C)
    strict_bd = jnp.logical_and(same_head, (r2 % C) > (c2 % C)).astype(F32)
    rq = lax.broadcasted_iota(jnp.int32, (C, 2 * C), 0)
    cq = lax.broadcasted_iota(jnp.int32, (C, 2 * C), 1)
    incl_lo = jnp.logical_and(cq < C, rq >= cq).astype(F32)
    incl_hi = jnp.logical_and(cq >= C, rq >= cq - C).astype(F32)
    bd_mask = same_head.astype(F32)
    lo = lax.broadcasted_iota(jnp.int32, (1, LANES), 1) < RWKV_HD

    def chunk(i, carry):
        g = i // nc
        rows = pl.ds(pl.multiple_of(i * C, C), C)
        lw = lw_scr[rows, :]
        cum = _dot(cumsum_mat, lw, exact=True)
        cum_end = cum[C - 1:C, :]
        dec = jnp.exp(cum)
        inv = jnp.exp(-cum)
        dec_prev = jnp.exp(cum - lw)
        dec_rest = jnp.exp(cum_end - cum)
        dec_end = jnp.exp(cum_end)
        kc = k_scr[rows, :]
        bc = b_scr[rows, :]
        r_t = r_scr[rows, :] * dec
        a_t = am_scr[rows, :] * dec_prev
        b_t = bc * inv
        k_t = kc * inv
        b_e = bc * dec_rest
        k_e = kc * dec_rest
        vc = v_scr[rows, :]
        for s in range(PAIRS):
            sl = slice(s * LANES, (s + 1) * LANES)
            S = st_scr[g, s]
            at, rt, bt, kt, vv = a_t[:, sl], r_t[:, sl], b_t[:, sl], k_t[:, sl], vc[:, sl]
            at_lo = jnp.where(lo, at, 0.0)
            at_hi = jnp.where(lo, 0.0, at)
            rt_lo = jnp.where(lo, rt, 0.0)
            rt_hi = jnp.where(lo, 0.0, rt)
            lhs = jnp.concatenate([at_lo, at_hi, rt_lo, rt_hi], axis=0)
            sb = _dot(lhs, jnp.concatenate([bt, bt], axis=0), _NT, exact=True)
            sk = _dot(lhs, jnp.concatenate([kt, kt], axis=0), _NT, exact=True)
            m = sb[0:2 * C] * strict_bd
            mk = sk[0:2 * C] * strict_bd
            arb = sb[2 * C:3 * C] * incl_lo + sb[3 * C:4 * C] * incl_hi
            ark = sk[2 * C:3 * C] * incl_lo + sk[3 * C:4 * C] * incl_hi
            ars = _dot(jnp.concatenate([at_lo, at_hi, rt], axis=0), S, _NT, exact=True)
            vs = jnp.concatenate([jnp.where(lo, vv, 0.0), jnp.where(lo, 0.0, vv)], axis=0)
            x = ars[0:2 * C] + _dot(mk, vs, exact=True)
            for it in range(6):
                if it < 5:
                    mx = _dot(m, jnp.concatenate([m, x], axis=1), exact=True)
                    x = x + mx[:, LANES:]
                    m = mx[:, :LANES]
                else:
                    x = x + _dot(m, x, exact=True)
            o = ars[2 * C:3 * C] + _dot(jnp.concatenate([arb, ark], axis=1),
                                        jnp.concatenate([x, vs], axis=0), exact=True)
            u = x[0:C] + x[C:2 * C]
            upd = _dot(jnp.concatenate([u, vv], axis=0),
                       jnp.concatenate([b_e[:, sl], k_e[:, sl]], axis=0), _TN, exact=True)
            st_scr[g, s] = S * dec_end[:, sl] + upd * bd_mask
            o_scr[rows, sl] = o
        return carry

    lax.fori_loop(0, G * nc, chunk, 0)

    for s in range(PAIRS):
        sl = slice(s * LANES, (s + 1) * LANES)
        o = o_scr[:, sl]
        d = o - _seg_sum(o, e) * (1.0 / RWKV_HD)
        var = _seg_sum(d * d, e) * (1.0 / RWKV_HD)
        y = d * lax.rsqrt(var + GN_EPS) * lng_ref[:, sl] + lnb_ref[:, sl]
        bonus = _seg_sum(r_scr[:, sl] * k_scr[:, sl] * rk_ref[:, sl], e) * v_scr[:, sl]
        o_ref[:, :, sl] = ((y + bonus) * g_scr[:, sl]).reshape(G, L, LANES)

    @pl.when(j == pl.num_programs(1) - 1)
    def _():
        sout_ref[...] = st_scr[...]


def _rwkv(p, shift_in, s0_pairs, lw, e, G, L):
    B, T, _ = p.shape
    R = G * L
    vec = lambda c: _const_spec((1, c))
    buf = lambda: pltpu.VMEM((R, RWKV_W), F32)
    return pl.pallas_call(
        _rwkv_kernel,
        grid=(B // G, T // L),
        in_specs=[pl.BlockSpec((G, L, RWKV_COLS), lambda i, j: (i, j, 0)),
                  pl.BlockSpec((G, 1, RWKV_COLS), lambda i, j: (i, 0, 0)),
                  pl.BlockSpec((G, PAIRS, LANES, LANES), lambda i, j: (i, 0, 0, 0)),
                  vec(RWKV_COLS), vec(RWKV_W), _const_spec((LANES, RWKV_W)), vec(RWKV_W),
                  _const_spec((LANES, RWKV_W)), _const_spec((LANES, RWKV_W)), vec(RWKV_W), vec(RWKV_W),
                  vec(RWKV_W), vec(RWKV_W), vec(RWKV_W), _const_spec((LANES, LANES))],
        out_specs=[pl.BlockSpec((G, L, RWKV_W), lambda i, j: (i, j, 0)),
                   pl.BlockSpec((G, PAIRS, LANES, LANES), lambda i, j: (i, 0, 0, 0))],
        out_shape=[jax.ShapeDtypeStruct((B, T, RWKV_W), F32),
                   jax.ShapeDtypeStruct((B, PAIRS, LANES, LANES), F32)],
        scratch_shapes=[pltpu.VMEM((G, 1, RWKV_COLS), F32), pltpu.VMEM((G, PAIRS, LANES, LANES), F32)]
                       + [buf() for _ in range(8)],
        compiler_params=_params(),
        name="rwkv7",
    )(p, shift_in, s0_pairs, lw["mu"], lw["w0"], lw["w2p"], lw["a0"], lw["a2p"], lw["g2"], lw["kk"],
      lw["ka"], lw["rk"], lw["ln_g"], lw["ln_b"], e)


def _swa_kernel(sink_ref, q_ref, k_ref, v_ref, hk_ref, hv_ref, o_ref, kc, ks, vc, vs, *, has_cache):
    G, L, _ = q_ref.shape
    C = CHUNK
    nc = L // C
    KB = WINDOW + C
    j = pl.program_id(1)
    lo = lax.broadcasted_iota(jnp.int32, (1, LANES), 1) < SWA_HD
    qi = lax.broadcasted_iota(jnp.int32, (C, KB), 0)
    kj = lax.broadcasted_iota(jnp.int32, (C, KB), 1)
    dist = jnp.abs(WINDOW + qi - kj).astype(F32)
    for g in range(G):
        kc[0:WINDOW, :] = hk_ref[g]
        kc[WINDOW:, :] = k_ref[g]
        vc[0:WINDOW, :] = hv_ref[g]
        vc[WINDOW:, :] = v_ref[g]
        ks[...] = pltpu.roll(kc[...], SWA_HD, 1)
        vs[...] = pltpu.roll(vc[...], SWA_HD, 1)

        def chunk(c, carry):
            off = pl.multiple_of(c * C, C)
            keys = (kc[pl.ds(off, KB), :], ks[pl.ds(off, KB), :])
            vals = (vc[pl.ds(off, KB), :], vs[pl.ds(off, KB), :])
            if has_cache:
                masked = None
            else:
                masked = jnp.logical_and(j == 0, kj + c * C < WINDOW)
            for s in range(SWA_HEADS * SWA_HD // LANES):
                kv = s // 2
                q = q_ref[g, pl.ds(off, C), s * LANES:(s + 1) * LANES]
                outs = []
                for half in range(2):
                    h = 2 * s + half
                    qh = jnp.where(lo, q, 0.0) if half == 0 else jnp.where(lo, 0.0, q)
                    sel = 0 if kv == half else 1
                    sc = _dot(qh, keys[sel], _NT) * (SWA_HD ** -0.5) - (2.0 ** -(h + 1)) * dist
                    if masked is not None:
                        sc = jnp.where(masked, -jnp.inf, sc)
                    sink = sink_ref[h]
                    mx = jnp.maximum(jnp.max(sc, axis=-1, keepdims=True), sink)
                    ex = jnp.exp(sc - mx)
                    den = jnp.sum(ex, axis=-1, keepdims=True) + jnp.exp(sink - mx)
                    outs.append(_dot(ex / den, vals[sel]))
                o_ref[g, pl.ds(off, C), s * LANES:(s + 1) * LANES] = jnp.where(lo, outs[0], outs[1])
            return carry

        lax.fori_loop(0, nc, chunk, 0)


def _swa(q, k, v, halo_k, halo_v, sink, G, L, has_cache):
    B, T, _ = q.shape
    if has_cache:
        halo = pl.BlockSpec((G, WINDOW, LANES), lambda i, j: (i, 0, 0))
    else:
        per = L // WINDOW
        halo = pl.BlockSpec((G, WINDOW, LANES), lambda i, j: (i, jnp.maximum(j * per - 1, 0), 0))
    tile = lambda c: pl.BlockSpec((G, L, c), lambda i, j: (i, j, 0))
    cat = lambda: pltpu.VMEM((WINDOW + L, LANES), F32)
    return pl.pallas_call(
        functools.partial(_swa_kernel, has_cache=has_cache),
        grid=(B // G, T // L),
        in_specs=[pl.BlockSpec(memory_space=pltpu.SMEM), tile(SWA_HEADS * SWA_HD), tile(LANES), tile(LANES),
                  halo, halo],
        out_specs=tile(SWA_HEADS * SWA_HD),
        out_shape=jax.ShapeDtypeStruct((B, T, SWA_HEADS * SWA_HD), F32),
        scratch_shapes=[cat(), cat(), cat(), cat()],
        compiler_params=_params(),
        name="swa",
    )(sink, q, k, v, halo_k, halo_v)


def _mem_kv_kernel(m_ref, g_ref, w_ref, kng_ref, mk_ref, mv_ref):
    G, M, D = m_ref.shape
    x = m_ref[...].reshape(G * M, D)
    h = (x * lax.rsqrt(jnp.mean(x * x, axis=-1, keepdims=True) + RMS_EPS) * g_ref[...]).astype(BF16)
    for s in range(MEM_HEADS):
        z = jnp.dot(h, w_ref[:, s * MEM_HD:(s + 1) * MEM_HD], preferred_element_type=F32)
        z = z * lax.rsqrt(jnp.mean(z * z, axis=-1, keepdims=True) + RMS_EPS) * kng_ref[...]
        mk_ref[:, :, s * MEM_HD:(s + 1) * MEM_HD] = z.reshape(G, M, MEM_HD)
    mv_ref[...] = jnp.dot(h, w_ref[:, MEM_W:], preferred_element_type=F32).reshape(G, M, MEM_W)


def _mem_kv(mem, g, w_b, kng):
    B, M, D = mem.shape
    blk = lambda c: pl.BlockSpec((1, M, c), lambda i, j: (i, 0, 0))
    return pl.pallas_call(
        _mem_kv_kernel,
        grid=(B, 1),
        in_specs=[blk(D), _const_spec((1, D)), _const_spec((D, 2 * MEM_W)), _const_spec((1, MEM_HD))],
        out_specs=[blk(MEM_W), blk(MEM_W)],
        out_shape=[jax.ShapeDtypeStruct((B, M, MEM_W), F32)] * 2,
        compiler_params=_params(),
        name="mem_kv",
    )(mem, g, w_b, kng)


def _mem_att_kernel(q_ref, mk_ref, mv_ref, o_ref):
    G, L, _ = q_ref.shape
    for g in range(G):
        for s in range(MEM_HEADS):
            sl = slice(s * MEM_HD, (s + 1) * MEM_HD)
            sc = _dot(q_ref[g, :, sl], mk_ref[g, :, sl], _NT) * (MEM_HD ** -0.5)
            ex = jnp.exp(sc - jnp.max(sc, axis=-1, keepdims=True))
            pr = ex / jnp.sum(ex, axis=-1, keepdims=True)
            o_ref[g, :, sl] = _dot(pr, mv_ref[g, :, sl])


def _mem_att(qm, mk, mv, G, L):
    B, T, _ = qm.shape
    tile = pl.BlockSpec((G, L, MEM_W), lambda i, j: (i, j, 0))
    mem = pl.BlockSpec((G, MEM_TOKENS, MEM_W), lambda i, j: (i, 0, 0))
    return pl.pallas_call(
        _mem_att_kernel,
        grid=(B // G, T // L),
        in_specs=[tile, mem, mem],
        out_specs=tile,
        out_shape=jax.ShapeDtypeStruct((B, T, MEM_W), F32),
        compiler_params=_params(),
        name="mem_att",
    )(qm, mk, mv)


def _merge_kernel(x_ref, oa_ref, ob_ref, om_ref, gt_ref, wb_ref, wo_ref, y_ref):
    G, L, D = x_ref.shape
    R = G * L
    mix = None
    for n, o_ref in enumerate((oa_ref, ob_ref, om_ref)):
        br = jnp.dot(o_ref[...].reshape(R, RWKV_W).astype(BF16), wb_ref[n], preferred_element_type=F32)
        t = gt_ref[:, :, n * D:(n + 1) * D].reshape(R, D) * br
        mix = t if mix is None else mix + t
    y = x_ref[...].reshape(R, D) + jnp.dot(mix.astype(BF16), wo_ref[...], preferred_element_type=F32)
    y_ref[...] = y.reshape(G, L, D)


def _merge(x, oa, ob, om, gt, wb_b, wo_b, G, L):
    B, T, D = x.shape
    tile = lambda c: pl.BlockSpec((G, L, c), lambda i, j: (i, j, 0))
    return pl.pallas_call(
        _merge_kernel,
        grid=(B // G, T // L),
        in_specs=[tile(D), tile(RWKV_W), tile(RWKV_W), tile(MEM_W), tile(N_BRANCH * D),
                  _const_spec((N_BRANCH, RWKV_W, D)), _const_spec((D, D))],
        out_specs=tile(D),
        out_shape=jax.ShapeDtypeStruct((B, T, D), F32),
        compiler_params=_params(),
        name="merge",
    )(x, oa, ob, om, gt, wb_b, wo_b)


def _ffn_kernel(x_ref, cin_ref, g2_ref, wu_ref, cw_ref, cb_ref, wd_ref, y_ref, cout_ref, carry):
    G, L, D = x_ref.shape
    R = G * L
    j = pl.program_id(1)

    @pl.when(j == 0)
    def _():
        carry[...] = cin_ref[...]

    x = x_ref[...].reshape(R, D)
    hb = (x * lax.rsqrt(jnp.mean(x * x, axis=-1, keepdims=True) + RMS_EPS) * g2_ref[...]).astype(BF16)
    row = lax.broadcasted_iota(jnp.int32, (L, 1), 0)
    acc = x
    for blk in range(D_FF // FF_BLOCK):
        cs = slice(blk * FF_BLOCK, (blk + 1) * FF_BLOCK)
        a_in = jnp.dot(hb, wu_ref[:, cs], preferred_element_type=F32)
        u = jnp.dot(hb, wu_ref[:, D_FF + blk * FF_BLOCK:D_FF + (blk + 1) * FF_BLOCK],
                    preferred_element_type=F32)
        convs = []
        for g in range(G):
            a = a_in[g * L:(g + 1) * L]
            prev = carry[g, :, cs]
            a1 = jnp.where(row == 0, prev[1:2], pltpu.roll(a, 1, 0))
            a2 = jnp.where(row == 0, prev[0:1], jnp.where(row == 1, prev[1:2], pltpu.roll(a, 2, 0)))
            carry[g, :, cs] = a[L - 2:L]
            convs.append(cb_ref[:, cs] + a2 * cw_ref[0:1, cs] + a1 * cw_ref[1:2, cs] + a * cw_ref[2:3, cs])
        c = convs[0] if G == 1 else jnp.concatenate(convs, axis=0)
        gelu = 0.5 * c * (1.0 + jnp.tanh(0.7978845608028654 * (c + 0.044715 * (c * c * c))))
        acc = acc + jnp.dot((gelu * u).astype(BF16), wd_ref[cs, :], preferred_element_type=F32)
    y_ref[...] = acc.reshape(G, L, D)

    @pl.when(j == pl.num_programs(1) - 1)
    def _():
        cout_ref[...] = carry[...]


def _ffn(x, conv_in, g2, wu_b, cw, cb, wd_b, G, L):
    B, T, D = x.shape
    tile = pl.BlockSpec((G, L, D), lambda i, j: (i, j, 0))
    st = pl.BlockSpec((G, CONV_W - 1, D_FF), lambda i, j: (i, 0, 0))
    return pl.pallas_call(
        _ffn_kernel,
        grid=(B // G, T // L),
        in_specs=[tile, st, _const_spec((1, D)), _const_spec((D, 2 * D_FF)), _const_spec((CONV_W, D_FF)),
                  _const_spec((1, D_FF)), _const_spec((D_FF, D))],
        out_specs=[tile, st],
        out_shape=[jax.ShapeDtypeStruct((B, T, D), F32), jax.ShapeDtypeStruct((B, CONV_W - 1, D_FF), F32)],
        scratch_shapes=[pltpu.VMEM((G, CONV_W - 1, D_FF), F32)],
        compiler_params=_params(),
        name="conv_ffn",
    )(x, conv_in, g2, wu_b, cw, cb, wd_b)


def _state_to_pairs(s):
    B = s.shape[0]
    sr = s.reshape(B, PAIRS, 2, RWKV_HD, RWKV_HD)
    eye = jnp.eye(2, dtype=s.dtype)
    bd = sr[:, :, :, :, None, :] * eye[None, None, :, None, :, None]
    return bd.reshape(B, PAIRS, LANES, LANES)


def _pairs_to_state(sp):
    B = sp.shape[0]
    sr = sp.reshape(B, PAIRS, 2, RWKV_HD, 2, RWKV_HD)
    return jnp.stack([sr[:, :, 0, :, 0, :], sr[:, :, 1, :, 1, :]], axis=2).reshape(B, RWKV_HEADS, RWKV_HD, RWKV_HD)


def _layer(x, lw, e, mk, mv, shift_in, s0, conv_in, halo_k, halo_v, tiles):
    has_cache = halo_k is not None
    G, L_in, L_mix, L_out = tiles
    p, q, k, v, qm, gt = _in_proj(x, lw["norm1_g"], lw["w_in"], lw["qn_g"], lw["kn_g"], lw["mqn_g"], e, G, L_in)
    oa, s_pairs = _rwkv(p, shift_in, _state_to_pairs(s0), lw, e, G, L_mix)
    if has_cache:
        ob = _swa(q, k, v, halo_k, halo_v, lw["sink"], G, L_mix, True)
    else:
        ob = _swa(q, k, v, k, v, lw["sink"], G, L_mix, False)
    om = _mem_att(qm, mk, mv, G, L_mix)
    x = _merge(x, oa, ob, om, gt, lw["w_branch"], lw["w_out"], G, L_out)
    x, conv_new = _ffn(x, conv_in, lw["norm2_g"], lw["w_up"], lw["conv_w"], lw["conv_b"], lw["w_down"], G, L_out)
    return x, (k, v, _pairs_to_state(s_pairs), p[:, -1:, :], conv_new)


def kernel(x_prompt, x_sample, cache_swa_k, cache_swa_v, cache_mem_k, cache_mem_v, state_rwkv, state_shift, state_conv, mem_prompt, norm1_g, w_in, rwkv_mu, rwkv_w0, rwkv_w2, rwkv_a0, rwkv_a2, rwkv_g2, rwkv_kk, rwkv_ka, rwkv_rk, rwkv_ln_g, rwkv_ln_b, swa_qn_g, swa_kn_g, swa_sink, mem_norm_g, w_mem_kv, mem_qn_g, mem_kn_g, w_branch, w_out, norm2_g, w_up, conv_w, conv_b, w_down):
    Bp, Tp, _ = x_prompt.shape
    Bs, Ts, _ = x_sample.shape
    dt = x_prompt.dtype
    half = jnp.arange(LANES) // RWKV_HD
    e = (half[:, None] == half[None, :]).astype(BF16)
    row = lambda a: a.reshape(1, -1)
    zpad = jnp.zeros((LANES - 64, RWKV_W), dt)

    yp, ys = x_prompt, x_sample
    outs_p = [[] for _ in range(7)]
    outs_s = [[] for _ in range(5)]
    prompt_tiles = (1, 256, 256, 256)
    sample_tiles = (8, Ts, Ts, Ts)
    for l in range(DEPTH):
        lw = {
            "norm1_g": row(norm1_g[l]), "w_in": w_in[l].astype(BF16),
            "qn_g": row(jnp.tile(swa_qn_g[l], 2)), "kn_g": row(jnp.tile(swa_kn_g[l], 2)),
            "mqn_g": row(mem_qn_g[l]),
            "mu": row(rwkv_mu[l]), "w0": row(rwkv_w0[l]),
            "w2p": jnp.concatenate([rwkv_w2[l], zpad], axis=0),
            "a0": row(rwkv_a0[l]),
            "a2p": jnp.concatenate([zpad, rwkv_a2[l]], axis=0),
            "g2": rwkv_g2[l], "kk": row(rwkv_kk[l]), "ka": row(rwkv_ka[l]), "rk": row(rwkv_rk[l]),
            "ln_g": row(rwkv_ln_g[l]), "ln_b": row(rwkv_ln_b[l]),
            "sink": swa_sink[l],
            "w_branch": w_branch[l].astype(BF16), "w_out": w_out[l].astype(BF16),
            "norm2_g": row(norm2_g[l]), "w_up": w_up[l].astype(BF16), "conv_w": conv_w[l],
            "conv_b": row(conv_b[l]), "w_down": w_down[l].astype(BF16),
        }
        mk, mv = _mem_kv(mem_prompt, row(mem_norm_g[l]), w_mem_kv[l].astype(BF16), row(mem_kn_g[l]))
        yp, (k, v, s_new, sh_new, cv_new) = _layer(
            yp, lw, e, mk, mv,
            jnp.zeros((Bp, 1, RWKV_COLS), dt),
            jnp.zeros((Bp, RWKV_HEADS, RWKV_HD, RWKV_HD), dt),
            jnp.zeros((Bp, CONV_W - 1, D_FF), dt), None, None, prompt_tiles)
        kv_shape = (Bp, WINDOW, 2, SWA_HD)
        for lst, val in zip(outs_p, (k[:, -WINDOW:].reshape(kv_shape), v[:, -WINDOW:].reshape(kv_shape),
                                     mk.reshape(Bp, MEM_TOKENS, MEM_HEADS, MEM_HD),
                                     mv.reshape(Bp, MEM_TOKENS, MEM_HEADS, MEM_HD), s_new, sh_new, cv_new)):
            lst.append(val)
        ck = cache_swa_k[l].reshape(Bs, WINDOW, LANES)
        cv = cache_swa_v[l].reshape(Bs, WINDOW, LANES)
        ys, (k, v, s_new, sh_new, cv_new) = _layer(
            ys, lw, e, cache_mem_k[l].reshape(Bs, MEM_TOKENS, MEM_W), cache_mem_v[l].reshape(Bs, MEM_TOKENS, MEM_W),
            state_shift[l], state_rwkv[l], state_conv[l], ck, cv, sample_tiles)
        kv_shape = (Bs, WINDOW, 2, SWA_HD)
        kf = jnp.concatenate([ck, k], axis=1)[:, -WINDOW:].reshape(kv_shape)
        vf = jnp.concatenate([cv, v], axis=1)[:, -WINDOW:].reshape(kv_shape)
        for lst, val in zip(outs_s, (kf, vf, s_new, sh_new, cv_new)):
            lst.append(val)
    return (yp, ys) + tuple(jnp.stack(o) for o in outs_p) + tuple(jnp.stack(o) for o in outs_s)
```

```python
import functools

import jax
import jax.numpy as jnp
from jax import lax
from jax.experimental import pallas as pl
from jax.experimental.pallas import tpu as pltpu

F32 = jnp.float32
BF16 = jnp.bfloat16
HIGHEST = lax.Precision.HIGHEST

D_MODEL = 1024
DEPTH = 2
CHUNK = 64
RWKV_HEADS = 8
RWKV_HD = 64
RWKV_W = 512
RWKV_COLS = 1792
GN_EPS = 64e-5
SWA_HEADS = 8
SWA_HD = 64
WINDOW = 128
MEM_TOKENS = 256
MEM_HEADS = 4
MEM_HD = 128
MEM_W = 512
N_BRANCH = 3
D_FF = 2816
CONV_W = 3
RMS_EPS = 1e-6

LANES = 128
PAIRS = RWKV_W // LANES
VMEM_LIMIT = 56 * 1024 * 1024
FF_BLOCK = D_FF // 2

C_Q = RWKV_COLS
C_K = C_Q + SWA_HEADS * SWA_HD
C_V = C_K + LANES
C_QM = C_V + LANES
C_GT = C_QM + MEM_W
IN_COLS = C_GT + N_BRANCH * D_MODEL


def _dot(a, b, dims=((1,), (0,)), exact=False):
    if exact:
        return lax.dot_general(a, b, (dims, ((), ())), precision=HIGHEST, preferred_element_type=F32)
    return lax.dot_general(a.astype(BF16), b.astype(BF16), (dims, ((), ())), preferred_element_type=F32)


_NT = ((1,), (1,))
_TN = ((0,), (0,))


def _seg_sum(x, e):
    hi = x.astype(BF16)
    lo = (x - hi.astype(F32)).astype(BF16)
    return jnp.dot(hi, e, preferred_element_type=F32) + jnp.dot(lo, e, preferred_element_type=F32)


def _sigmoid(x):
    return 1.0 / (1.0 + jnp.exp(-x))


def _const_spec(shape):
    n = len(shape)
    return pl.BlockSpec(shape, lambda *_: (0,) * n, pipeline_mode=pl.Buffered(1))


def _params():
    return pltpu.CompilerParams(dimension_semantics=("arbitrary", "arbitrary"), vmem_limit_bytes=VMEM_LIMIT)


def _in_kernel(x_ref, g1_ref, w_ref, qng_ref, kng_ref, mqg_ref, e_ref,
               p_ref, q_ref, k_ref, v_ref, qm_ref, gt_ref):
    G, L, D = x_ref.shape
    R = G * L
    x = x_ref[...].reshape(R, D)
    h = x * lax.rsqrt(jnp.mean(x * x, axis=-1, keepdims=True) + RMS_EPS) * g1_ref[...]
    hb = h.astype(BF16)
    e = e_ref[...]

    def proj(c0, c1):
        return jnp.dot(hb, w_ref[:, c0:c1], preferred_element_type=F32)

    def head_rms(z, gain):
        return z * lax.rsqrt(_seg_sum(z * z, e) * (1.0 / SWA_HD) + RMS_EPS) * gain

    p_ref[...] = proj(0, RWKV_COLS).reshape(G, L, RWKV_COLS)
    for s in range(SWA_HEADS * SWA_HD // LANES):
        z = proj(C_Q + s * LANES, C_Q + (s + 1) * LANES)
        q_ref[:, :, s * LANES:(s + 1) * LANES] = head_rms(z, qng_ref[...]).reshape(G, L, LANES)
    k_ref[...] = head_rms(proj(C_K, C_V), kng_ref[...]).reshape(G, L, LANES)
    v_ref[...] = proj(C_V, C_QM).reshape(G, L, LANES)
    for s in range(MEM_HEADS):
        z = proj(C_QM + s * MEM_HD, C_QM + (s + 1) * MEM_HD)
        z = z * lax.rsqrt(jnp.mean(z * z, axis=-1, keepdims=True) + RMS_EPS) * mqg_ref[...]
        qm_ref[:, :, s * MEM_HD:(s + 1) * MEM_HD] = z.reshape(G, L, MEM_HD)
    for s in range(N_BRANCH):
        z = proj(C_GT + s * D_MODEL, C_GT + (s + 1) * D_MODEL)
        gt_ref[:, :, s * D_MODEL:(s + 1) * D_MODEL] = _sigmoid(z).reshape(G, L, D_MODEL)


def _in_proj(x, g1, w_in_b, qng, kng, mqg, e, G, L):
    B, T, D = x.shape
    tile = lambda c: pl.BlockSpec((G, L, c), lambda i, j: (i, j, 0))
    out_cols = (RWKV_COLS, SWA_HEADS * SWA_HD, LANES, LANES, MEM_W, N_BRANCH * D_MODEL)
    return pl.pallas_call(
        _in_kernel,
        grid=(B // G, T // L),
        in_specs=[tile(D), _const_spec((1, D)), _const_spec((D, IN_COLS)), _const_spec((1, LANES)),
                  _const_spec((1, LANES)), _const_spec((1, MEM_HD)), _const_spec((LANES, LANES))],
        out_specs=[tile(c) for c in out_cols],
        out_shape=[jax.ShapeDtypeStruct((B, T, c), F32) for c in out_cols],
        compiler_params=_params(),
        name="in_proj",
    )(x, g1, w_in_b, qng, kng, mqg, e)


def _rwkv_kernel(p_ref, sh_ref, s0_ref, mu_ref, w0_ref, w2_ref, a0_ref, a2_ref, g2_ref, kk_ref, ka_ref,
                 rk_ref, lng_ref, lnb_ref, e_ref,
                 o_ref, sout_ref,
                 prev_scr, st_scr, r_scr, k_scr, v_scr, am_scr, b_scr, lw_scr, g_scr, o_scr,
                 wr_scr, uo_scr, bkt_scr, vb_scr, dcol_scr):
    G, L, _ = p_ref.shape
    C = CHUNK
    nc = L // C
    j = pl.program_id(1)
    e = e_ref[...]

    @pl.when(j == 0)
    def _():
        st_scr[...] = s0_ref[...]
        prev_scr[...] = sh_ref[...]

    first_row = lax.broadcasted_iota(jnp.int32, (L, 1), 0) == 0
    for g in range(G):
        p = p_ref[g]
        shifted = jnp.where(first_row, prev_scr[g], pltpu.roll(p, 1, 0))
        pm = p + (shifted - p) * mu_ref[...]
        prev_scr[g] = p[L - 1:L, :]
        rows = slice(g * L, (g + 1) * L)
        r = pm[:, 0:RWKV_W]
        k = pm[:, RWKV_W:2 * RWKV_W]
        v = pm[:, 2 * RWKV_W:3 * RWKV_W]
        xwa = pm[:, 3 * RWKV_W:3 * RWKV_W + LANES]
        xg = pm[:, 3 * RWKV_W + LANES:RWKV_COLS]
        z = w0_ref[...] + _dot(jnp.tanh(xwa), w2_ref[...])
        w = -(jnp.maximum(-z, 0.0) + jnp.log(1.0 + jnp.exp(-jnp.abs(z)))) - 0.5
        a = _sigmoid(a0_ref[...] + _dot(xwa, a2_ref[...]))
        kkv = k * kk_ref[...]
        for s in range(PAIRS):
            sl = slice(s * LANES, (s + 1) * LANES)
            kks = kkv[:, sl]
            kkn = kks / jnp.maximum(jnp.sqrt(_seg_sum(kks * kks, e)), 1e-12)
            am_scr[rows, sl] = -kkn
            b_scr[rows, sl] = kkn * a[:, sl]
        r_scr[rows, :] = r
        k_scr[rows, :] = k * (1.0 + (a - 1.0) * ka_ref[...])
        v_scr[rows, :] = v
        lw_scr[rows, :] = -jnp.exp(w)
        g_scr[rows, :] = _dot(_sigmoid(xg), g2_ref[...])

    ri = lax.broadcasted_iota(jnp.int32, (C, C), 0)
    ci = lax.broadcasted_iota(jnp.int32, (C, C), 1)
    cumsum_mat = (ri >= ci).astype(BF16)
    rq = lax.broadcasted_iota(jnp.int32, (C, 2 * C), 0)
    cq = lax.broadcasted_iota(jnp.int32, (C, 2 * C), 1)
    strict_l = jnp.logical_and(cq < C, rq > cq).astype(F32)
    strict_r = jnp.logical_and(cq >= C, rq > cq - C).astype(F32)
    incl = (rq >= cq % C).astype(F32)
    r2 = lax.broadcasted_iota(jnp.int32, (2 * C, 2 * C), 0)
    c2 = lax.broadcasted_iota(jnp.int32, (2 * C, 2 * C), 1)
    bd_mask = ((r2 // C) == (c2 // C)).astype(F32)
    lo = lax.broadcasted_iota(jnp.int32, (1, LANES), 1) < RWKV_HD

    def stack(z):
        return jnp.concatenate([jnp.where(lo, z, 0.0), jnp.where(lo, 0.0, z)], axis=0)

    def fold(z):
        return z[0:C] + z[C:2 * C]

    def prepare(i, carry):
        rows = pl.ds(pl.multiple_of(i * C, C), C)
        lw = lw_scr[rows, :]
        lw_hi = lw.astype(BF16)
        lw_lo = (lw - lw_hi.astype(F32)).astype(BF16)
        cum = (jnp.dot(cumsum_mat, lw_hi, preferred_element_type=F32)
               + jnp.dot(cumsum_mat, lw_lo, preferred_element_type=F32))
        cum_end = cum[C - 1:C, :]
        dec = jnp.exp(cum)
        inv = jnp.exp(-cum)
        dec_prev = jnp.exp(cum - lw)
        dec_rest = jnp.exp(cum_end - cum)
        dec_end = jnp.exp(cum_end)
        kc = k_scr[rows, :]
        bc = b_scr[rows, :]
        r_t = r_scr[rows, :] * dec
        a_t = am_scr[rows, :] * dec_prev
        b_t = bc * inv
        k_t = kc * inv
        b_e = bc * dec_rest
        k_e = kc * dec_rest
        vc = v_scr[rows, :]
        P = range(PAIRS)
        sl = [slice(s * LANES, (s + 1) * LANES) for s in P]
        for s in P:
            bkt_scr[i, s] = jnp.concatenate([b_e[:, sl[s]], k_e[:, sl[s]]], axis=0).T.astype(BF16)
            vb_scr[i, s] = vc[:, sl[s]].astype(BF16)
            dcol_scr[i, s] = jnp.broadcast_to(dec_end[:, sl[s]], (LANES, LANES)).T
        lhs = [jnp.concatenate([a_t[:, sl[s]], r_t[:, sl[s]]], axis=0).astype(BF16) for s in P]
        nb = [_dot(lhs[s], stack(b_t[:, sl[s]]), _NT) for s in P]
        nk = [_dot(lhs[s], stack(k_t[:, sl[s]]), _NT) for s in P]
        m = [jnp.concatenate([nb[s][0:C] * strict_l, nb[s][0:C] * strict_r], axis=0) for s in P]
        mk = [jnp.concatenate([nk[s][0:C] * strict_l, nk[s][0:C] * strict_r], axis=0) for s in P]
        vs = [stack(vc[:, sl[s]]).astype(BF16) for s in P]
        y = [jnp.concatenate([stack(a_t[:, sl[s]]), _dot(mk[s], vs[s])], axis=1) for s in P]
        for it in range(6):
            if it < 5:
                my = [_dot(m[s], jnp.concatenate([m[s], y[s]], axis=1)) for s in P]
                y = [y[s] + my[s][:, LANES:] for s in P]
                m = [my[s][:, :LANES] for s in P]
            else:
                my = [_dot(m[s], y[s]) for s in P]
                y = [y[s] + my[s] for s in P]
        aw = [_dot(nb[s][C:2 * C] * incl, y[s]) for s in P]
        akv = [_dot(nk[s][C:2 * C] * incl, vs[s]) for s in P]
        for s in P:
            wf = fold(y[s][:, :LANES])
            uv = fold(y[s][:, LANES:])
            rw = r_t[:, sl[s]] + aw[s][:, :LANES]
            ov = aw[s][:, LANES:] + akv[s]
            wr_scr[i, s] = jnp.concatenate([wf, rw], axis=0).astype(BF16)
            uo_scr[i, s] = jnp.concatenate([uv, ov], axis=0)
        return carry

    lax.fori_loop(0, G * nc, prepare, 0)

    def advance(i, carry):
        g = i // nc
        rows = pl.ds(pl.multiple_of(i * C, C), C)
        hs = [st_scr[g, s] for s in range(PAIRS)]
        new_h, outs = [], []
        for s in range(PAIRS):
            y = jnp.dot(wr_scr[i, s], hs[s].astype(BF16), preferred_element_type=F32) + uo_scr[i, s]
            uvb = jnp.concatenate([y[0:C].astype(BF16), vb_scr[i, s]], axis=0)
            upd = jnp.dot(bkt_scr[i, s], uvb, preferred_element_type=F32)
            new_h.append(hs[s] * dcol_scr[i, s] + upd * bd_mask)
            outs.append(y[C:2 * C])
        for s in range(PAIRS):
            st_scr[g, s] = new_h[s]
            o_scr[rows, s * LANES:(s + 1) * LANES] = outs[s]
        return carry

    lax.fori_loop(0, G * nc, advance, 0)

    for s in range(PAIRS):
        sl = slice(s * LANES, (s + 1) * LANES)
        o = o_scr[:, sl]
        d = o - _seg_sum(o, e) * (1.0 / RWKV_HD)
        var = _seg_sum(d * d, e) * (1.0 / RWKV_HD)
        y = d * lax.rsqrt(var + GN_EPS) * lng_ref[:, sl] + lnb_ref[:, sl]
        bonus = _seg_sum(r_scr[:, sl] * k_scr[:, sl] * rk_ref[:, sl], e) * v_scr[:, sl]
        o_ref[:, :, sl] = ((y + bonus) * g_scr[:, sl]).reshape(G, L, LANES)

    @pl.when(j == pl.num_programs(1) - 1)
    def _():
        sout_ref[...] = st_scr[...]


def _rwkv(p, shift_in, s0_pairs, lw, e, G, L):
    B, T, _ = p.shape
    R = G * L
    vec = lambda c: _const_spec((1, c))
    buf = lambda: pltpu.VMEM((R, RWKV_W), F32)
    per_chunk = lambda rows, dt: pltpu.VMEM((R // CHUNK, PAIRS, rows, LANES), dt)
    return pl.pallas_call(
        _rwkv_kernel,
        grid=(B // G, T // L),
        in_specs=[pl.BlockSpec((G, L, RWKV_COLS), lambda i, j: (i, j, 0)),
                  pl.BlockSpec((G, 1, RWKV_COLS), lambda i, j: (i, 0, 0)),
                  pl.BlockSpec((G, PAIRS, LANES, LANES), lambda i, j: (i, 0, 0, 0)),
                  vec(RWKV_COLS), vec(RWKV_W), _const_spec((LANES, RWKV_W)), vec(RWKV_W),
                  _const_spec((LANES, RWKV_W)), _const_spec((LANES, RWKV_W)), vec(RWKV_W), vec(RWKV_W),
                  vec(RWKV_W), vec(RWKV_W), vec(RWKV_W), _const_spec((LANES, LANES))],
        out_specs=[pl.BlockSpec((G, L, RWKV_W), lambda i, j: (i, j, 0)),
                   pl.BlockSpec((G, PAIRS, LANES, LANES), lambda i, j: (i, 0, 0, 0))],
        out_shape=[jax.ShapeDtypeStruct((B, T, RWKV_W), F32),
                   jax.ShapeDtypeStruct((B, PAIRS, LANES, LANES), F32)],
        scratch_shapes=[pltpu.VMEM((G, 1, RWKV_COLS), F32), pltpu.VMEM((G, PAIRS, LANES, LANES), F32)]
                       + [buf() for _ in range(8)]
                       + [per_chunk(LANES, BF16), per_chunk(LANES, F32), per_chunk(LANES, BF16),
                          per_chunk(CHUNK, BF16), per_chunk(LANES, F32)],
        compiler_params=_params(),
        name="rwkv7",
    )(p, shift_in, s0_pairs, lw["mu"], lw["w0"], lw["w2p"], lw["a0"], lw["a2p"], lw["g2"], lw["kk"],
      lw["ka"], lw["rk"], lw["ln_g"], lw["ln_b"], e)


def _swa_kernel(sink_ref, q_ref, k_ref, v_ref, hk_ref, hv_ref, o_ref, kc, ks, vc, vs, *, has_cache):
    G, L, _ = q_ref.shape
    C = CHUNK
    nc = L // C
    KB = WINDOW + C
    j = pl.program_id(1)
    lo = lax.broadcasted_iota(jnp.int32, (1, LANES), 1) < SWA_HD
    qi = lax.broadcasted_iota(jnp.int32, (C, KB), 0)
    kj = lax.broadcasted_iota(jnp.int32, (C, KB), 1)
    dist = jnp.abs(WINDOW + qi - kj).astype(F32)
    for g in range(G):
        kc[0:WINDOW, :] = hk_ref[g]
        kc[WINDOW:, :] = k_ref[g]
        vc[0:WINDOW, :] = hv_ref[g]
        vc[WINDOW:, :] = v_ref[g]
        ks[...] = pltpu.roll(kc[...], SWA_HD, 1)
        vs[...] = pltpu.roll(vc[...], SWA_HD, 1)

        def chunk(c, carry):
            off = pl.multiple_of(c * C, C)
            keys = (kc[pl.ds(off, KB), :], ks[pl.ds(off, KB), :])
            vals = (vc[pl.ds(off, KB), :], vs[pl.ds(off, KB), :])
            if has_cache:
                masked = None
            else:
                masked = jnp.logical_and(j == 0, kj + c * C < WINDOW)
            for s in range(SWA_HEADS * SWA_HD // LANES):
                kv = s // 2
                q = q_ref[g, pl.ds(off, C), s * LANES:(s + 1) * LANES]
                outs = []
                for half in range(2):
                    h = 2 * s + half
                    qh = jnp.where(lo, q, 0.0) if half == 0 else jnp.where(lo, 0.0, q)
                    sel = 0 if kv == half else 1
                    sc = _dot(qh, keys[sel], _NT) * (SWA_HD ** -0.5) - (2.0 ** -(h + 1)) * dist
                    if masked is not None:
                        sc = jnp.where(masked, -jnp.inf, sc)
                    sink = sink_ref[h]
                    mx = jnp.maximum(jnp.max(sc, axis=-1, keepdims=True), sink)
                    ex = jnp.exp(sc - mx)
                    den = jnp.sum(ex, axis=-1, keepdims=True) + jnp.exp(sink - mx)
                    outs.append(_dot(ex / den, vals[sel]))
                o_ref[g, pl.ds(off, C), s * LANES:(s + 1) * LANES] = jnp.where(lo, outs[0], outs[1])
            return carry

        lax.fori_loop(0, nc, chunk, 0)


def _swa(q, k, v, halo_k, halo_v, sink, G, L, has_cache):
    B, T, _ = q.shape
    if has_cache:
        halo = pl.BlockSpec((G, WINDOW, LANES), lambda i, j: (i, 0, 0))
    else:
        per = L // WINDOW
        halo = pl.BlockSpec((G, WINDOW, LANES), lambda i, j: (i, jnp.maximum(j * per - 1, 0), 0))
    tile = lambda c: pl.BlockSpec((G, L, c), lambda i, j: (i, j, 0))
    cat = lambda: pltpu.VMEM((WINDOW + L, LANES), F32)
    return pl.pallas_call(
        functools.partial(_swa_kernel, has_cache=has_cache),
        grid=(B // G, T // L),
        in_specs=[pl.BlockSpec(memory_space=pltpu.SMEM), tile(SWA_HEADS * SWA_HD), tile(LANES), tile(LANES),
                  halo, halo],
        out_specs=tile(SWA_HEADS * SWA_HD),
        out_shape=jax.ShapeDtypeStruct((B, T, SWA_HEADS * SWA_HD), F32),
        scratch_shapes=[cat(), cat(), cat(), cat()],
        compiler_params=_params(),
        name="swa",
    )(sink, q, k, v, halo_k, halo_v)


def _mem_kv_kernel(m_ref, g_ref, w_ref, kng_ref, mk_ref, mv_ref):
    G, M, D = m_ref.shape
    x = m_ref[...].reshape(G * M, D)
    h = (x * lax.rsqrt(jnp.mean(x * x, axis=-1, keepdims=True) + RMS_EPS) * g_ref[...]).astype(BF16)
    for s in range(MEM_HEADS):
        z = jnp.dot(h, w_ref[:, s * MEM_HD:(s + 1) * MEM_HD], preferred_element_type=F32)
        z = z * lax.rsqrt(jnp.mean(z * z, axis=-1, keepdims=True) + RMS_EPS) * kng_ref[...]
        mk_ref[:, :, s * MEM_HD:(s + 1) * MEM_HD] = z.reshape(G, M, MEM_HD)
    mv_ref[...] = jnp.dot(h, w_ref[:, MEM_W:], preferred_element_type=F32).reshape(G, M, MEM_W)


def _mem_kv(mem, g, w_b, kng):
    B, M, D = mem.shape
    blk = lambda c: pl.BlockSpec((1, M, c), lambda i, j: (i, 0, 0))
    return pl.pallas_call(
        _mem_kv_kernel,
        grid=(B, 1),
        in_specs=[blk(D), _const_spec((1, D)), _const_spec((D, 2 * MEM_W)), _const_spec((1, MEM_HD))],
        out_specs=[blk(MEM_W), blk(MEM_W)],
        out_shape=[jax.ShapeDtypeStruct((B, M, MEM_W), F32)] * 2,
        compiler_params=_params(),
        name="mem_kv",
    )(mem, g, w_b, kng)


def _mem_att_kernel(q_ref, mk_ref, mv_ref, o_ref):
    G, L, _ = q_ref.shape
    for g in range(G):
        for s in range(MEM_HEADS):
            sl = slice(s * MEM_HD, (s + 1) * MEM_HD)
            sc = _dot(q_ref[g, :, sl], mk_ref[g, :, sl], _NT) * (MEM_HD ** -0.5)
            ex = jnp.exp(sc - jnp.max(sc, axis=-1, keepdims=True))
            pr = ex / jnp.sum(ex, axis=-1, keepdims=True)
            o_ref[g, :, sl] = _dot(pr, mv_ref[g, :, sl])


def _mem_att(qm, mk, mv, G, L):
    B, T, _ = qm.shape
    tile = pl.BlockSpec((G, L, MEM_W), lambda i, j: (i, j, 0))
    mem = pl.BlockSpec((G, MEM_TOKENS, MEM_W), lambda i, j: (i, 0, 0))
    return pl.pallas_call(
        _mem_att_kernel,
        grid=(B // G, T // L),
        in_specs=[tile, mem, mem],
        out_specs=tile,
        out_shape=jax.ShapeDtypeStruct((B, T, MEM_W), F32),
        compiler_params=_params(),
        name="mem_att",
    )(qm, mk, mv)


def _merge_kernel(x_ref, oa_ref, ob_ref, om_ref, gt_ref, wb_ref, wo_ref, y_ref):
    G, L, D = x_ref.shape
    R = G * L
    mix = None
    for n, o_ref in enumerate((oa_ref, ob_ref, om_ref)):
        br = jnp.dot(o_ref[...].reshape(R, RWKV_W).astype(BF16), wb_ref[n], preferred_element_type=F32)
        t = gt_ref[:, :, n * D:(n + 1) * D].reshape(R, D) * br
        mix = t if mix is None else mix + t
    y = x_ref[...].reshape(R, D) + jnp.dot(mix.astype(BF16), wo_ref[...], preferred_element_type=F32)
    y_ref[...] = y.reshape(G, L, D)


def _merge(x, oa, ob, om, gt, wb_b, wo_b, G, L):
    B, T, D = x.shape
    tile = lambda c: pl.BlockSpec((G, L, c), lambda i, j: (i, j, 0))
    return pl.pallas_call(
        _merge_kernel,
        grid=(B // G, T // L),
        in_specs=[tile(D), tile(RWKV_W), tile(RWKV_W), tile(MEM_W), tile(N_BRANCH * D),
                  _const_spec((N_BRANCH, RWKV_W, D)), _const_spec((D, D))],
        out_specs=tile(D),
        out_shape=jax.ShapeDtypeStruct((B, T, D), F32),
        compiler_params=_params(),
        name="merge",
    )(x, oa, ob, om, gt, wb_b, wo_b)


def _ffn_kernel(x_ref, cin_ref, g2_ref, wu_ref, cw_ref, cb_ref, wd_ref, y_ref, cout_ref, carry):
    G, L, D = x_ref.shape
    R = G * L
    j = pl.program_id(1)

    @pl.when(j == 0)
    def _():
        carry[...] = cin_ref[...]

    x = x_ref[...].reshape(R, D)
    hb = (x * lax.rsqrt(jnp.mean(x * x, axis=-1, keepdims=True) + RMS_EPS) * g2_ref[...]).astype(BF16)
    row = lax.broadcasted_iota(jnp.int32, (L, 1), 0)
    acc = x
    for blk in range(D_FF // FF_BLOCK):
        cs = slice(blk * FF_BLOCK, (blk + 1) * FF_BLOCK)
        a_in = jnp.dot(hb, wu_ref[:, cs], preferred_element_type=F32)
        u = jnp.dot(hb, wu_ref[:, D_FF + blk * FF_BLOCK:D_FF + (blk + 1) * FF_BLOCK],
                    preferred_element_type=F32)
        convs = []
        for g in range(G):
            a = a_in[g * L:(g + 1) * L]
            prev = carry[g, :, cs]
            a1 = jnp.where(row == 0, prev[1:2], pltpu.roll(a, 1, 0))
            a2 = jnp.where(row == 0, prev[0:1], jnp.where(row == 1, prev[1:2], pltpu.roll(a, 2, 0)))
            carry[g, :, cs] = a[L - 2:L]
            convs.append(cb_ref[:, cs] + a2 * cw_ref[0:1, cs] + a1 * cw_ref[1:2, cs] + a * cw_ref[2:3, cs])
        c = convs[0] if G == 1 else jnp.concatenate(convs, axis=0)
        gelu = 0.5 * c * (1.0 + jnp.tanh(0.7978845608028654 * (c + 0.044715 * (c * c * c))))
        acc = acc + jnp.dot((gelu * u).astype(BF16), wd_ref[cs, :], preferred_element_type=F32)
    y_ref[...] = acc.reshape(G, L, D)

    @pl.when(j == pl.num_programs(1) - 1)
    def _():
        cout_ref[...] = carry[...]


def _ffn(x, conv_in, g2, wu_b, cw, cb, wd_b, G, L):
    B, T, D = x.shape
    tile = pl.BlockSpec((G, L, D), lambda i, j: (i, j, 0))
    st = pl.BlockSpec((G, CONV_W - 1, D_FF), lambda i, j: (i, 0, 0))
    return pl.pallas_call(
        _ffn_kernel,
        grid=(B // G, T // L),
        in_specs=[tile, st, _const_spec((1, D)), _const_spec((D, 2 * D_FF)), _const_spec((CONV_W, D_FF)),
                  _const_spec((1, D_FF)), _const_spec((D_FF, D))],
        out_specs=[tile, st],
        out_shape=[jax.ShapeDtypeStruct((B, T, D), F32), jax.ShapeDtypeStruct((B, CONV_W - 1, D_FF), F32)],
        scratch_shapes=[pltpu.VMEM((G, CONV_W - 1, D_FF), F32)],
        compiler_params=_params(),
        name="conv_ffn",
    )(x, conv_in, g2, wu_b, cw, cb, wd_b)


def _state_to_pairs(s):
    B = s.shape[0]
    sr = s.reshape(B, PAIRS, 2, RWKV_HD, RWKV_HD)
    eye = jnp.eye(2, dtype=s.dtype)
    bd = sr[:, :, :, :, None, :] * eye[None, None, :, None, :, None]
    return bd.reshape(B, PAIRS, LANES, LANES)


def _pairs_to_state(sp):
    B = sp.shape[0]
    sr = sp.reshape(B, PAIRS, 2, RWKV_HD, 2, RWKV_HD)
    return jnp.stack([sr[:, :, 0, :, 0, :], sr[:, :, 1, :, 1, :]], axis=2).reshape(B, RWKV_HEADS, RWKV_HD, RWKV_HD)


def _layer(x, lw, e, mk, mv, shift_in, s0, conv_in, halo_k, halo_v, tiles):
    has_cache = halo_k is not None
    G, L_in, L_mix, L_out = tiles
    p, q, k, v, qm, gt = _in_proj(x, lw["norm1_g"], lw["w_in"], lw["qn_g"], lw["kn_g"], lw["mqn_g"], e, G, L_in)
    oa, h_pairs = _rwkv(p, shift_in, jnp.swapaxes(_state_to_pairs(s0), -1, -2), lw, e, G, L_mix)
    if has_cache:
        ob = _swa(q, k, v, halo_k, halo_v, lw["sink"], G, L_mix, True)
    else:
        ob = _swa(q, k, v, k, v, lw["sink"], G, L_mix, False)
    om = _mem_att(qm, mk, mv, G, L_mix)
    x = _merge(x, oa, ob, om, gt, lw["w_branch"], lw["w_out"], G, L_out)
    x, conv_new = _ffn(x, conv_in, lw["norm2_g"], lw["w_up"], lw["conv_w"], lw["conv_b"], lw["w_down"], G, L_out)
    return x, (k, v, _pairs_to_state(jnp.swapaxes(h_pairs, -1, -2)), p[:, -1:, :], conv_new)


def kernel(x_prompt, x_sample, cache_swa_k, cache_swa_v, cache_mem_k, cache_mem_v, state_rwkv, state_shift, state_conv, mem_prompt, norm1_g, w_in, rwkv_mu, rwkv_w0, rwkv_w2, rwkv_a0, rwkv_a2, rwkv_g2, rwkv_kk, rwkv_ka, rwkv_rk, rwkv_ln_g, rwkv_ln_b, swa_qn_g, swa_kn_g, swa_sink, mem_norm_g, w_mem_kv, mem_qn_g, mem_kn_g, w_branch, w_out, norm2_g, w_up, conv_w, conv_b, w_down):
    Bp, Tp, _ = x_prompt.shape
    Bs, Ts, _ = x_sample.shape
    dt = x_prompt.dtype
    half = jnp.arange(LANES) // RWKV_HD
    e = (half[:, None] == half[None, :]).astype(BF16)
    row = lambda a: a.reshape(1, -1)
    zpad = jnp.zeros((LANES - 64, RWKV_W), dt)

    yp, ys = x_prompt, x_sample
    outs_p = [[] for _ in range(7)]
    outs_s = [[] for _ in range(5)]
    prompt_tiles = (1, 256, 256, 256)
    sample_tiles = (8, Ts, Ts, Ts)
    for l in range(DEPTH):
        lw = {
            "norm1_g": row(norm1_g[l]), "w_in": w_in[l].astype(BF16),
            "qn_g": row(jnp.tile(swa_qn_g[l], 2)), "kn_g": row(jnp.tile(swa_kn_g[l], 2)),
            "mqn_g": row(mem_qn_g[l]),
            "mu": row(rwkv_mu[l]), "w0": row(rwkv_w0[l]),
            "w2p": jnp.concatenate([rwkv_w2[l], zpad], axis=0),
            "a0": row(rwkv_a0[l]),
            "a2p": jnp.concatenate([zpad, rwkv_a2[l]], axis=0),
            "g2": rwkv_g2[l], "kk": row(rwkv_kk[l]), "ka": row(rwkv_ka[l]), "rk": row(rwkv_rk[l]),
            "ln_g": row(rwkv_ln_g[l]), "ln_b": row(rwkv_ln_b[l]),
            "sink": swa_sink[l],
            "w_branch": w_branch[l].astype(BF16), "w_out": w_out[l].astype(BF16),
            "norm2_g": row(norm2_g[l]), "w_up": w_up[l].astype(BF16), "conv_w": conv_w[l],
            "conv_b": row(conv_b[l]), "w_down": w_down[l].astype(BF16),
        }
        mk, mv = _mem_kv(mem_prompt, row(mem_norm_g[l]), w_mem_kv[l].astype(BF16), row(mem_kn_g[l]))
        yp, (k, v, s_new, sh_new, cv_new) = _layer(
            yp, lw, e, mk, mv,
            jnp.zeros((Bp, 1, RWKV_COLS), dt),
            jnp.zeros((Bp, RWKV_HEADS, RWKV_HD, RWKV_HD), dt),
            jnp.zeros((Bp, CONV_W - 1, D_FF), dt), None, None, prompt_tiles)
        kv_shape = (Bp, WINDOW, 2, SWA_HD)
        for lst, val in zip(outs_p, (k[:, -WINDOW:].reshape(kv_shape), v[:, -WINDOW:].reshape(kv_shape),
                                     mk.reshape(Bp, MEM_TOKENS, MEM_HEADS, MEM_HD),
                                     mv.reshape(Bp, MEM_TOKENS, MEM_HEADS, MEM_HD), s_new, sh_new, cv_new)):
            lst.append(val)
        ck = cache_swa_k[l].reshape(Bs, WINDOW, LANES)
        cv = cache_swa_v[l].reshape(Bs, WINDOW, LANES)
        ys, (k, v, s_new, sh_new, cv_new) = _layer(
            ys, lw, e, cache_mem_k[l].reshape(Bs, MEM_TOKENS, MEM_W), cache_mem_v[l].reshape(Bs, MEM_TOKENS, MEM_W),
            state_shift[l], state_rwkv[l], state_conv[l], ck, cv, sample_tiles)
        kv_shape = (Bs, WINDOW, 2, SWA_HD)
        kf = jnp.concatenate([ck, k], axis=1)[:, -WINDOW:].reshape(kv_shape)
        vf = jnp.concatenate([cv, v], axis=1)[:, -WINDOW:].reshape(kv_shape)
        for lst, val in zip(outs_s, (kf, vf, s_new, sh_new, cv_new)):
            lst.append(val)
    return (yp, ys) + tuple(jnp.stack(o) for o in outs_p) + tuple(jnp.stack(o) for o in outs_s)
```

```python
import functools

import jax
import jax.numpy as jnp
from jax import lax
from jax.experimental import pallas as pl
from jax.experimental.pallas import tpu as pltpu

F32 = jnp.float32
BF16 = jnp.bfloat16
HIGHEST = lax.Precision.HIGHEST

D_MODEL = 1024
DEPTH = 2
CHUNK = 64
RWKV_HEADS = 8
RWKV_HD = 64
RWKV_W = 512
RWKV_COLS = 1792
GN_EPS = 64e-5
SWA_HEADS = 8
SWA_HD = 64
WINDOW = 128
MEM_TOKENS = 256
MEM_HEADS = 4
MEM_HD = 128
MEM_W = 512
N_BRANCH = 3
D_FF = 2816
CONV_W = 3
RMS_EPS = 1e-6

LANES = 128
PAIRS = RWKV_W // LANES
VMEM_LIMIT = 56 * 1024 * 1024
FF_BLOCK = D_FF // 2
UNROLL = 2

C_Q = RWKV_COLS
C_K = C_Q + SWA_HEADS * SWA_HD
C_V = C_K + LANES
C_QM = C_V + LANES
C_GT = C_QM + MEM_W
IN_COLS = C_GT + N_BRANCH * D_MODEL


def _dot(a, b, dims=((1,), (0,)), exact=False):
    if exact:
        return lax.dot_general(a, b, (dims, ((), ())), precision=HIGHEST, preferred_element_type=F32)
    return lax.dot_general(a.astype(BF16), b.astype(BF16), (dims, ((), ())), preferred_element_type=F32)


_NT = ((1,), (1,))
_TN = ((0,), (0,))


def _seg_sum(x, e):
    hi = x.astype(BF16)
    lo = (x - hi.astype(F32)).astype(BF16)
    return jnp.dot(hi, e, preferred_element_type=F32) + jnp.dot(lo, e, preferred_element_type=F32)


def _sigmoid(x):
    return 1.0 / (1.0 + jnp.exp(-x))


def _const_spec(shape):
    n = len(shape)
    return pl.BlockSpec(shape, lambda *_: (0,) * n, pipeline_mode=pl.Buffered(1))


def _params():
    return pltpu.CompilerParams(dimension_semantics=("arbitrary", "arbitrary"), vmem_limit_bytes=VMEM_LIMIT)


def _in_kernel(x_ref, g1_ref, w_ref, qng_ref, kng_ref, mqg_ref, e_ref,
               p_ref, q_ref, k_ref, v_ref, qm_ref, gt_ref):
    G, L, D = x_ref.shape
    R = G * L
    x = x_ref[...].reshape(R, D)
    h = x * lax.rsqrt(jnp.mean(x * x, axis=-1, keepdims=True) + RMS_EPS) * g1_ref[...]
    hb = h.astype(BF16)
    e = e_ref[...]

    def proj(c0, c1):
        return jnp.dot(hb, w_ref[:, c0:c1], preferred_element_type=F32)

    def head_rms(z, gain):
        return z * lax.rsqrt(_seg_sum(z * z, e) * (1.0 / SWA_HD) + RMS_EPS) * gain

    p_ref[...] = proj(0, RWKV_COLS).reshape(G, L, RWKV_COLS)
    for s in range(SWA_HEADS * SWA_HD // LANES):
        z = proj(C_Q + s * LANES, C_Q + (s + 1) * LANES)
        q_ref[:, :, s * LANES:(s + 1) * LANES] = head_rms(z, qng_ref[...]).reshape(G, L, LANES)
    k_ref[...] = head_rms(proj(C_K, C_V), kng_ref[...]).reshape(G, L, LANES)
    v_ref[...] = proj(C_V, C_QM).reshape(G, L, LANES)
    for s in range(MEM_HEADS):
        z = proj(C_QM + s * MEM_HD, C_QM + (s + 1) * MEM_HD)
        z = z * lax.rsqrt(jnp.mean(z * z, axis=-1, keepdims=True) + RMS_EPS) * mqg_ref[...]
        qm_ref[:, :, s * MEM_HD:(s + 1) * MEM_HD] = z.reshape(G, L, MEM_HD)
    for s in range(N_BRANCH):
        z = proj(C_GT + s * D_MODEL, C_GT + (s + 1) * D_MODEL)
        gt_ref[:, :, s * D_MODEL:(s + 1) * D_MODEL] = _sigmoid(z).reshape(G, L, D_MODEL)


def _in_proj(x, g1, w_in_b, qng, kng, mqg, e, G, L):
    B, T, D = x.shape
    tile = lambda c: pl.BlockSpec((G, L, c), lambda i, j: (i, j, 0))
    out_cols = (RWKV_COLS, SWA_HEADS * SWA_HD, LANES, LANES, MEM_W, N_BRANCH * D_MODEL)
    return pl.pallas_call(
        _in_kernel,
        grid=(B // G, T // L),
        in_specs=[tile(D), _const_spec((1, D)), _const_spec((D, IN_COLS)), _const_spec((1, LANES)),
                  _const_spec((1, LANES)), _const_spec((1, MEM_HD)), _const_spec((LANES, LANES))],
        out_specs=[tile(c) for c in out_cols],
        out_shape=[jax.ShapeDtypeStruct((B, T, c), F32) for c in out_cols],
        compiler_params=_params(),
        name="in_proj",
    )(x, g1, w_in_b, qng, kng, mqg, e)


def _rwkv_kernel(p_ref, sh_ref, s0_ref, mu_ref, w0_ref, w2_ref, a0_ref, a2_ref, g2_ref, kk_ref, ka_ref,
                 rk_ref, lng_ref, lnb_ref, e_ref,
                 o_ref, sout_ref,
                 prev_scr, st_scr, r_scr, k_scr, v_scr, am_scr, b_scr, lw_scr, g_scr, o_scr,
                 wr_scr, uo_scr, bkt_scr, vb_scr, dcol_scr):
    G, L, _ = p_ref.shape
    C = CHUNK
    nc = L // C
    j = pl.program_id(1)
    e = e_ref[...]

    @pl.when(j == 0)
    def _():
        st_scr[...] = s0_ref[...]
        prev_scr[...] = sh_ref[...]

    first_row = lax.broadcasted_iota(jnp.int32, (L, 1), 0) == 0
    for g in range(G):
        p = p_ref[g]
        shifted = jnp.where(first_row, prev_scr[g], pltpu.roll(p, 1, 0))
        pm = p + (shifted - p) * mu_ref[...]
        prev_scr[g] = p[L - 1:L, :]
        rows = slice(g * L, (g + 1) * L)
        r = pm[:, 0:RWKV_W]
        k = pm[:, RWKV_W:2 * RWKV_W]
        v = pm[:, 2 * RWKV_W:3 * RWKV_W]
        xwa = pm[:, 3 * RWKV_W:3 * RWKV_W + LANES]
        xg = pm[:, 3 * RWKV_W + LANES:RWKV_COLS]
        z = w0_ref[...] + _dot(jnp.tanh(xwa), w2_ref[...])
        w = -(jnp.maximum(-z, 0.0) + jnp.log(1.0 + jnp.exp(-jnp.abs(z)))) - 0.5
        a = _sigmoid(a0_ref[...] + _dot(xwa, a2_ref[...]))
        kkv = k * kk_ref[...]
        for s in range(PAIRS):
            sl = slice(s * LANES, (s + 1) * LANES)
            kks = kkv[:, sl]
            kkn = kks / jnp.maximum(jnp.sqrt(_seg_sum(kks * kks, e)), 1e-12)
            am_scr[rows, sl] = -kkn
            b_scr[rows, sl] = kkn * a[:, sl]
        r_scr[rows, :] = r
        k_scr[rows, :] = k * (1.0 + (a - 1.0) * ka_ref[...])
        v_scr[rows, :] = v
        lw_scr[rows, :] = -jnp.exp(w)
        g_scr[rows, :] = _dot(_sigmoid(xg), g2_ref[...])

    ri = lax.broadcasted_iota(jnp.int32, (C, C), 0)
    ci = lax.broadcasted_iota(jnp.int32, (C, C), 1)
    cumsum_mat = (ri >= ci).astype(BF16)
    rq = lax.broadcasted_iota(jnp.int32, (C, 2 * C), 0)
    cq = lax.broadcasted_iota(jnp.int32, (C, 2 * C), 1)
    strict_l = jnp.logical_and(cq < C, rq > cq).astype(F32)
    strict_r = jnp.logical_and(cq >= C, rq > cq - C).astype(F32)
    incl = (rq >= cq % C).astype(F32)
    incl_l = jnp.logical_and(cq < C, rq >= cq).astype(F32)
    incl_r = jnp.logical_and(cq >= C, rq >= cq - C).astype(F32)
    r2 = lax.broadcasted_iota(jnp.int32, (2 * C, 2 * C), 0)
    c2 = lax.broadcasted_iota(jnp.int32, (2 * C, 2 * C), 1)
    bd_mask = ((r2 // C) == (c2 // C)).astype(F32)
    lo = lax.broadcasted_iota(jnp.int32, (1, LANES), 1) < RWKV_HD

    def stack(z):
        return jnp.concatenate([jnp.where(lo, z, 0.0), jnp.where(lo, 0.0, z)], axis=0)

    def stack_other(z):
        return jnp.concatenate([jnp.where(lo, 0.0, z), jnp.where(lo, z, 0.0)], axis=0)

    def prepare(t, carry):
        units = []
        for n in range(UNROLL):
            i = t * UNROLL + n
            rows = pl.ds(pl.multiple_of(i * C, C), C)
            lw = lw_scr[rows, :]
            lw_hi = lw.astype(BF16)
            lw_lo = (lw - lw_hi.astype(F32)).astype(BF16)
            cum = (jnp.dot(cumsum_mat, lw_hi, preferred_element_type=F32)
                   + jnp.dot(cumsum_mat, lw_lo, preferred_element_type=F32))
            cum_end = cum[C - 1:C, :]
            inv = jnp.exp(-cum)
            dec_rest = jnp.exp(cum_end - cum)
            dec_end = jnp.exp(cum_end)
            kc = k_scr[rows, :]
            bc = b_scr[rows, :]
            r_t = r_scr[rows, :] * jnp.exp(cum)
            a_t = am_scr[rows, :] * jnp.exp(cum - lw)
            b_t = bc * inv
            k_t = kc * inv
            b_e = bc * dec_rest
            k_e = kc * dec_rest
            vc = v_scr[rows, :]
            for s in range(PAIRS):
                sl = slice(s * LANES, (s + 1) * LANES)
                bkt_scr[i, s] = jnp.concatenate([b_e[:, sl], k_e[:, sl]], axis=0).T.astype(BF16)
                vb_scr[i, s] = vc[:, sl].astype(BF16)
                dcol_scr[i, s] = jnp.broadcast_to(dec_end[:, sl], (LANES, LANES)).T
                units.append((i, s, a_t[:, sl], r_t[:, sl], b_t[:, sl], k_t[:, sl], vc[:, sl]))
        N = range(len(units))
        lhs = [jnp.concatenate([u[2], u[3]], axis=0).astype(BF16) for u in units]
        nb = [_dot(lhs[n], stack(units[n][4]), _NT) for n in N]
        nk = [_dot(lhs[n], stack(units[n][5]), _NT) for n in N]
        m = [jnp.concatenate([nb[n][0:C] * strict_l, nb[n][0:C] * strict_r], axis=0) for n in N]
        mk = [jnp.concatenate([nk[n][0:C] * strict_l, nk[n][0:C] * strict_r], axis=0) for n in N]
        vs = [stack(units[n][6]).astype(BF16) for n in N]
        vsw = [stack_other(pltpu.roll(units[n][6], RWKV_HD, 1)).astype(BF16) for n in N]
        y = [stack(units[n][2]) + _dot(mk[n], vsw[n]) for n in N]
        for it in range(6):
            if it < 5:
                my = [_dot(m[n], jnp.concatenate([m[n], y[n]], axis=1)) for n in N]
                y = [y[n] + my[n][:, LANES:] for n in N]
                m = [my[n][:, :LANES] for n in N]
            else:
                my = [_dot(m[n], y[n]) for n in N]
                y = [y[n] + my[n] for n in N]
        arb = [jnp.concatenate([nb[n][C:2 * C] * incl_l, nb[n][C:2 * C] * incl_r], axis=0) for n in N]
        ay = [_dot(arb[n], y[n]) for n in N]
        akv = [_dot(nk[n][C:2 * C] * incl, vs[n]) for n in N]
        for n in N:
            i, s = units[n][0], units[n][1]
            wf = jnp.where(lo, y[n][0:C], y[n][C:2 * C])
            rw = units[n][3] + jnp.where(lo, ay[n][0:C], ay[n][C:2 * C])
            swapped = jnp.concatenate([jnp.where(lo, y[n][C:2 * C], y[n][0:C]),
                                       jnp.where(lo, ay[n][C:2 * C], ay[n][0:C])], axis=0)
            uo = pltpu.roll(swapped, RWKV_HD, 1)
            wr_scr[i, s] = jnp.concatenate([wf, rw], axis=0).astype(BF16)
            uo_scr[i, s] = jnp.concatenate([uo[0:C], uo[C:2 * C] + akv[n]], axis=0)
        return carry

    lax.fori_loop(0, G * nc // UNROLL, prepare, 0)

    def advance(i, carry):
        g = i // nc
        rows = pl.ds(pl.multiple_of(i * C, C), C)
        hs = [st_scr[g, s] for s in range(PAIRS)]
        new_h, outs = [], []
        for s in range(PAIRS):
            y = jnp.dot(wr_scr[i, s], hs[s].astype(BF16), preferred_element_type=F32) + uo_scr[i, s]
            uvb = jnp.concatenate([y[0:C].astype(BF16), vb_scr[i, s]], axis=0)
            upd = jnp.dot(bkt_scr[i, s], uvb, preferred_element_type=F32)
            new_h.append(hs[s] * dcol_scr[i, s] + upd * bd_mask)
            outs.append(y[C:2 * C])
        for s in range(PAIRS):
            st_scr[g, s] = new_h[s]
            o_scr[rows, s * LANES:(s + 1) * LANES] = outs[s]
        return carry

    lax.fori_loop(0, G * nc, advance, 0)

    for s in range(PAIRS):
        sl = slice(s * LANES, (s + 1) * LANES)
        o = o_scr[:, sl]
        d = o - _seg_sum(o, e) * (1.0 / RWKV_HD)
        var = _seg_sum(d * d, e) * (1.0 / RWKV_HD)
        y = d * lax.rsqrt(var + GN_EPS) * lng_ref[:, sl] + lnb_ref[:, sl]
        bonus = _seg_sum(r_scr[:, sl] * k_scr[:, sl] * rk_ref[:, sl], e) * v_scr[:, sl]
        o_ref[:, :, sl] = ((y + bonus) * g_scr[:, sl]).reshape(G, L, LANES)

    @pl.when(j == pl.num_programs(1) - 1)
    def _():
        sout_ref[...] = st_scr[...]


def _rwkv(p, shift_in, s0_pairs, lw, e, G, L):
    B, T, _ = p.shape
    R = G * L
    vec = lambda c: _const_spec((1, c))
    buf = lambda: pltpu.VMEM((R, RWKV_W), F32)
    per_chunk = lambda rows, dt: pltpu.VMEM((R // CHUNK, PAIRS, rows, LANES), dt)
    return pl.pallas_call(
        _rwkv_kernel,
        grid=(B // G, T // L),
        in_specs=[pl.BlockSpec((G, L, RWKV_COLS), lambda i, j: (i, j, 0)),
                  pl.BlockSpec((G, 1, RWKV_COLS), lambda i, j: (i, 0, 0)),
                  pl.BlockSpec((G, PAIRS, LANES, LANES), lambda i, j: (i, 0, 0, 0)),
                  vec(RWKV_COLS), vec(RWKV_W), _const_spec((LANES, RWKV_W)), vec(RWKV_W),
                  _const_spec((LANES, RWKV_W)), _const_spec((LANES, RWKV_W)), vec(RWKV_W), vec(RWKV_W),
                  vec(RWKV_W), vec(RWKV_W), vec(RWKV_W), _const_spec((LANES, LANES))],
        out_specs=[pl.BlockSpec((G, L, RWKV_W), lambda i, j: (i, j, 0)),
                   pl.BlockSpec((G, PAIRS, LANES, LANES), lambda i, j: (i, 0, 0, 0))],
        out_shape=[jax.ShapeDtypeStruct((B, T, RWKV_W), F32),
                   jax.ShapeDtypeStruct((B, PAIRS, LANES, LANES), F32)],
        scratch_shapes=[pltpu.VMEM((G, 1, RWKV_COLS), F32), pltpu.VMEM((G, PAIRS, LANES, LANES), F32)]
                       + [buf() for _ in range(8)]
                       + [per_chunk(LANES, BF16), per_chunk(LANES, F32), per_chunk(LANES, BF16),
                          per_chunk(CHUNK, BF16), per_chunk(LANES, F32)],
        compiler_params=_params(),
        name="rwkv7",
    )(p, shift_in, s0_pairs, lw["mu"], lw["w0"], lw["w2p"], lw["a0"], lw["a2p"], lw["g2"], lw["kk"],
      lw["ka"], lw["rk"], lw["ln_g"], lw["ln_b"], e)


def _swa_kernel(sink_ref, q_ref, k_ref, v_ref, hk_ref, hv_ref, o_ref, kd, vd, *, has_cache):
    G, L, _ = q_ref.shape
    C = CHUNK
    nc = L // C
    KB = WINDOW + C
    GROUP = 4
    j = pl.program_id(1)
    lo = lax.broadcasted_iota(jnp.int32, (1, LANES), 1) < SWA_HD
    qi = lax.broadcasted_iota(jnp.int32, (C, KB), 0)
    kj = lax.broadcasted_iota(jnp.int32, (C, KB), 1)
    dist = jnp.abs(WINDOW + qi - kj).astype(F32)
    kj4 = lax.broadcasted_iota(jnp.int32, (GROUP * C, KB), 1)
    ones = jnp.ones((WINDOW + L, LANES), F32)
    neg_pad = jnp.full((GROUP * C, 2 * LANES - KB), -jnp.inf, F32)
    bias = [jnp.concatenate([(2.0 ** -(GROUP * kv + h + 1)) * dist for h in range(GROUP)], axis=0)
            for kv in range(2)]
    sink = [jnp.concatenate([jnp.full((C, LANES), sink_ref[GROUP * kv + h], F32) for h in range(GROUP)], axis=0)
            for kv in range(2)]
    for g in range(G):
        kc = jnp.concatenate([hk_ref[g], k_ref[g]], axis=0)
        vc = jnp.concatenate([hv_ref[g], v_ref[g]], axis=0)
        ks = pltpu.roll(kc, SWA_HD, 1)
        vs = pltpu.roll(vc, SWA_HD, 1)
        kd[g, 0] = jnp.where(lo, kc, ks).astype(BF16)
        kd[g, 1] = jnp.where(lo, ks, kc).astype(BF16)
        vd[g, 0] = jnp.concatenate([jnp.where(lo, vc, vs), ones], axis=1).astype(BF16)
        vd[g, 1] = jnp.concatenate([jnp.where(lo, vs, vc), ones], axis=1).astype(BF16)

    def chunks(t, carry, mask_halo):
        units = []
        for n in range(UNROLL):
            i = t * UNROLL + n
            for kv in range(2):
                units.append((i // nc, pl.multiple_of((i % nc) * C, C), kv))
        scs = []
        for g, off, kv in units:
            q0 = q_ref[g, pl.ds(off, C), (2 * kv) * LANES:(2 * kv + 1) * LANES]
            q1 = q_ref[g, pl.ds(off, C), (2 * kv + 1) * LANES:(2 * kv + 2) * LANES]
            lhs = jnp.concatenate([jnp.where(lo, q0, 0.0), jnp.where(lo, 0.0, q0),
                                   jnp.where(lo, q1, 0.0), jnp.where(lo, 0.0, q1)], axis=0).astype(BF16)
            sc = lax.dot_general(lhs, kd[g, kv, pl.ds(off, KB), :], ((_NT), ((), ())),
                                 preferred_element_type=F32) - bias[kv]
            if mask_halo:
                sc = jnp.where(kj4 + off < WINDOW, -jnp.inf, sc)
            scs.append(sc)
        exs, sink_terms = [], []
        for (g, off, kv), sc in zip(units, scs):
            folded = jnp.maximum(sc[:, :LANES], jnp.concatenate([sc[:, LANES:], neg_pad], axis=1))
            mx = jnp.maximum(jnp.broadcast_to(jnp.max(folded, axis=-1, keepdims=True), (GROUP * C, LANES)),
                             sink[kv])
            ex = jnp.concatenate([jnp.exp(sc[:, :LANES] - mx), jnp.exp(sc[:, LANES:] - mx[:, :KB - LANES])],
                                 axis=1)
            exs.append(ex.astype(BF16))
            sink_terms.append(jnp.exp(sink[kv] - mx))
        for (g, off, kv), ex, st in zip(units, exs, sink_terms):
            pvd = jnp.dot(ex, vd[g, kv, pl.ds(off, KB), :], preferred_element_type=F32)
            pv = pvd[:, :LANES] / (pvd[:, LANES:] + st)
            o_ref[g, pl.ds(off, C), (2 * kv) * LANES:(2 * kv + 1) * LANES] = jnp.where(lo, pv[0:C], pv[C:2 * C])
            o_ref[g, pl.ds(off, C), (2 * kv + 1) * LANES:(2 * kv + 2) * LANES] = jnp.where(
                lo, pv[2 * C:3 * C], pv[3 * C:4 * C])
        return carry

    steps = G * nc // UNROLL
    if has_cache:
        lax.fori_loop(0, steps, functools.partial(chunks, mask_halo=False), 0)
    else:
        @pl.when(j == 0)
        def _():
            lax.fori_loop(0, steps, functools.partial(chunks, mask_halo=True), 0)

        @pl.when(j != 0)
        def _():
            lax.fori_loop(0, steps, functools.partial(chunks, mask_halo=False), 0)


def _swa(q, k, v, halo_k, halo_v, sink, G, L, has_cache):
    B, T, _ = q.shape
    if has_cache:
        halo = pl.BlockSpec((G, WINDOW, LANES), lambda i, j: (i, 0, 0))
    else:
        per = L // WINDOW
        halo = pl.BlockSpec((G, WINDOW, LANES), lambda i, j: (i, jnp.maximum(j * per - 1, 0), 0))
    tile = lambda c: pl.BlockSpec((G, L, c), lambda i, j: (i, j, 0))
    cat = lambda cols: pltpu.VMEM((G, 2, WINDOW + L, cols), BF16)
    return pl.pallas_call(
        functools.partial(_swa_kernel, has_cache=has_cache),
        grid=(B // G, T // L),
        in_specs=[pl.BlockSpec(memory_space=pltpu.SMEM), tile(SWA_HEADS * SWA_HD), tile(LANES), tile(LANES),
                  halo, halo],
        out_specs=tile(SWA_HEADS * SWA_HD),
        out_shape=jax.ShapeDtypeStruct((B, T, SWA_HEADS * SWA_HD), F32),
        scratch_shapes=[cat(LANES), cat(2 * LANES)],
        compiler_params=_params(),
        name="swa",
    )(sink, q, k, v, halo_k, halo_v)


def _mem_kv_kernel(m_ref, g_ref, w_ref, kng_ref, mk_ref, mv_ref):
    G, M, D = m_ref.shape
    x = m_ref[...].reshape(G * M, D)
    h = (x * lax.rsqrt(jnp.mean(x * x, axis=-1, keepdims=True) + RMS_EPS) * g_ref[...]).astype(BF16)
    for s in range(MEM_HEADS):
        z = jnp.dot(h, w_ref[:, s * MEM_HD:(s + 1) * MEM_HD], preferred_element_type=F32)
        z = z * lax.rsqrt(jnp.mean(z * z, axis=-1, keepdims=True) + RMS_EPS) * kng_ref[...]
        mk_ref[:, :, s * MEM_HD:(s + 1) * MEM_HD] = z.reshape(G, M, MEM_HD)
    mv_ref[...] = jnp.dot(h, w_ref[:, MEM_W:], preferred_element_type=F32).reshape(G, M, MEM_W)


def _mem_kv(mem, g, w_b, kng):
    B, M, D = mem.shape
    blk = lambda c: pl.BlockSpec((1, M, c), lambda i, j: (i, 0, 0))
    return pl.pallas_call(
        _mem_kv_kernel,
        grid=(B, 1),
        in_specs=[blk(D), _const_spec((1, D)), _const_spec((D, 2 * MEM_W)), _const_spec((1, MEM_HD))],
        out_specs=[blk(MEM_W), blk(MEM_W)],
        out_shape=[jax.ShapeDtypeStruct((B, M, MEM_W), F32)] * 2,
        compiler_params=_params(),
        name="mem_kv",
    )(mem, g, w_b, kng)


def _mem_att_kernel(q_ref, mk_ref, mv_ref, o_ref):
    G, L, _ = q_ref.shape
    for g in range(G):
        for s in range(MEM_HEADS):
            sl = slice(s * MEM_HD, (s + 1) * MEM_HD)
            sc = _dot(q_ref[g, :, sl], mk_ref[g, :, sl], _NT) * (MEM_HD ** -0.5)
            ex = jnp.exp(sc - jnp.max(sc, axis=-1, keepdims=True))
            pr = ex / jnp.sum(ex, axis=-1, keepdims=True)
            o_ref[g, :, sl] = _dot(pr, mv_ref[g, :, sl])


def _mem_att(qm, mk, mv, G, L):
    B, T, _ = qm.shape
    tile = pl.BlockSpec((G, L, MEM_W), lambda i, j: (i, j, 0))
    mem = pl.BlockSpec((G, MEM_TOKENS, MEM_W), lambda i, j: (i, 0, 0))
    return pl.pallas_call(
        _mem_att_kernel,
        grid=(B // G, T // L),
        in_specs=[tile, mem, mem],
        out_specs=tile,
        out_shape=jax.ShapeDtypeStruct((B, T, MEM_W), F32),
        compiler_params=_params(),
        name="mem_att",
    )(qm, mk, mv)


def _merge_kernel(x_ref, oa_ref, ob_ref, om_ref, gt_ref, wb_ref, wo_ref, y_ref):
    G, L, D = x_ref.shape
    R = G * L
    mix = None
    for n, o_ref in enumerate((oa_ref, ob_ref, om_ref)):
        br = jnp.dot(o_ref[...].reshape(R, RWKV_W).astype(BF16), wb_ref[n], preferred_element_type=F32)
        t = gt_ref[:, :, n * D:(n + 1) * D].reshape(R, D) * br
        mix = t if mix is None else mix + t
    y = x_ref[...].reshape(R, D) + jnp.dot(mix.astype(BF16), wo_ref[...], preferred_element_type=F32)
    y_ref[...] = y.reshape(G, L, D)


def _merge(x, oa, ob, om, gt, wb_b, wo_b, G, L):
    B, T, D = x.shape
    tile = lambda c: pl.BlockSpec((G, L, c), lambda i, j: (i, j, 0))
    return pl.pallas_call(
        _merge_kernel,
        grid=(B // G, T // L),
        in_specs=[tile(D), tile(RWKV_W), tile(RWKV_W), tile(MEM_W), tile(N_BRANCH * D),
                  _const_spec((N_BRANCH, RWKV_W, D)), _const_spec((D, D))],
        out_specs=tile(D),
        out_shape=jax.ShapeDtypeStruct((B, T, D), F32),
        compiler_params=_params(),
        name="merge",
    )(x, oa, ob, om, gt, wb_b, wo_b)


def _ffn_kernel(x_ref, cin_ref, g2_ref, wu_ref, cw_ref, cb_ref, wd_ref, y_ref, cout_ref, carry):
    G, L, D = x_ref.shape
    R = G * L
    j = pl.program_id(1)

    @pl.when(j == 0)
    def _():
        carry[...] = cin_ref[...]

    x = x_ref[...].reshape(R, D)
    hb = (x * lax.rsqrt(jnp.mean(x * x, axis=-1, keepdims=True) + RMS_EPS) * g2_ref[...]).astype(BF16)
    row = lax.broadcasted_iota(jnp.int32, (L, 1), 0)
    acc = x
    for blk in range(D_FF // FF_BLOCK):
        cs = slice(blk * FF_BLOCK, (blk + 1) * FF_BLOCK)
        a_in = jnp.dot(hb, wu_ref[:, cs], preferred_element_type=F32)
        u = jnp.dot(hb, wu_ref[:, D_FF + blk * FF_BLOCK:D_FF + (blk + 1) * FF_BLOCK],
                    preferred_element_type=F32)
        convs = []
        for g in range(G):
            a = a_in[g * L:(g + 1) * L]
            prev = carry[g, :, cs]
            a1 = jnp.where(row == 0, prev[1:2], pltpu.roll(a, 1, 0))
            a2 = jnp.where(row == 0, prev[0:1], jnp.where(row == 1, prev[1:2], pltpu.roll(a, 2, 0)))
            carry[g, :, cs] = a[L - 2:L]
            convs.append(cb_ref[:, cs] + a2 * cw_ref[0:1, cs] + a1 * cw_ref[1:2, cs] + a * cw_ref[2:3, cs])
        c = convs[0] if G == 1 else jnp.concatenate(convs, axis=0)
        gelu = 0.5 * c * (1.0 + jnp.tanh(0.7978845608028654 * (c + 0.044715 * (c * c * c))))
        acc = acc + jnp.dot((gelu * u).astype(BF16), wd_ref[cs, :], preferred_element_type=F32)
    y_ref[...] = acc.reshape(G, L, D)

    @pl.when(j == pl.num_programs(1) - 1)
    def _():
        cout_ref[...] = carry[...]


def _ffn(x, conv_in, g2, wu_b, cw, cb, wd_b, G, L):
    B, T, D = x.shape
    tile = pl.BlockSpec((G, L, D), lambda i, j: (i, j, 0))
    st = pl.BlockSpec((G, CONV_W - 1, D_FF), lambda i, j: (i, 0, 0))
    return pl.pallas_call(
        _ffn_kernel,
        grid=(B // G, T // L),
        in_specs=[tile, st, _const_spec((1, D)), _const_spec((D, 2 * D_FF)), _const_spec((CONV_W, D_FF)),
                  _const_spec((1, D_FF)), _const_spec((D_FF, D))],
        out_specs=[tile, st],
        out_shape=[jax.ShapeDtypeStruct((B, T, D), F32), jax.ShapeDtypeStruct((B, CONV_W - 1, D_FF), F32)],
        scratch_shapes=[pltpu.VMEM((G, CONV_W - 1, D_FF), F32)],
        compiler_params=_params(),
        name="conv_ffn",
    )(x, conv_in, g2, wu_b, cw, cb, wd_b)


def _state_to_pairs(s):
    B = s.shape[0]
    sr = s.reshape(B, PAIRS, 2, RWKV_HD, RWKV_HD)
    eye = jnp.eye(2, dtype=s.dtype)
    bd = sr[:, :, :, :, None, :] * eye[None, None, :, None, :, None]
    return bd.reshape(B, PAIRS, LANES, LANES)


def _pairs_to_state(sp):
    B = sp.shape[0]
    sr = sp.reshape(B, PAIRS, 2, RWKV_HD, 2, RWKV_HD)
    return jnp.stack([sr[:, :, 0, :, 0, :], sr[:, :, 1, :, 1, :]], axis=2).reshape(B, RWKV_HEADS, RWKV_HD, RWKV_HD)


def _layer(x, lw, e, mk, mv, shift_in, s0, conv_in, halo_k, halo_v, tiles):
    has_cache = halo_k is not None
    G, L_in, L_mix, L_out = tiles
    p, q, k, v, qm, gt = _in_proj(x, lw["norm1_g"], lw["w_in"], lw["qn_g"], lw["kn_g"], lw["mqn_g"], e, G, L_in)
    oa, h_pairs = _rwkv(p, shift_in, jnp.swapaxes(_state_to_pairs(s0), -1, -2), lw, e, G, L_mix)
    if has_cache:
        ob = _swa(q, k, v, halo_k, halo_v, lw["sink"], G, L_mix, True)
    else:
        ob = _swa(q, k, v, k, v, lw["sink"], G, L_mix, False)
    om = _mem_att(qm, mk, mv, G, L_mix)
    x = _merge(x, oa, ob, om, gt, lw["w_branch"], lw["w_out"], G, L_out)
    x, conv_new = _ffn(x, conv_in, lw["norm2_g"], lw["w_up"], lw["conv_w"], lw["conv_b"], lw["w_down"], G, L_out)
    return x, (k, v, _pairs_to_state(jnp.swapaxes(h_pairs, -1, -2)), p[:, -1:, :], conv_new)


def kernel(x_prompt, x_sample, cache_swa_k, cache_swa_v, cache_mem_k, cache_mem_v, state_rwkv, state_shift, state_conv, mem_prompt, norm1_g, w_in, rwkv_mu, rwkv_w0, rwkv_w2, rwkv_a0, rwkv_a2, rwkv_g2, rwkv_kk, rwkv_ka, rwkv_rk, rwkv_ln_g, rwkv_ln_b, swa_qn_g, swa_kn_g, swa_sink, mem_norm_g, w_mem_kv, mem_qn_g, mem_kn_g, w_branch, w_out, norm2_g, w_up, conv_w, conv_b, w_down):
    Bp, Tp, _ = x_prompt.shape
    Bs, Ts, _ = x_sample.shape
    dt = x_prompt.dtype
    half = jnp.arange(LANES) // RWKV_HD
    e = (half[:, None] == half[None, :]).astype(BF16)
    row = lambda a: a.reshape(1, -1)
    zpad = jnp.zeros((LANES - 64, RWKV_W), dt)

    yp, ys = x_prompt, x_sample
    outs_p = [[] for _ in range(7)]
    outs_s = [[] for _ in range(5)]
    prompt_tiles = (1, 256, 256, 256)
    sample_tiles = (8, Ts, Ts, Ts)
    for l in range(DEPTH):
        lw = {
            "norm1_g": row(norm1_g[l]), "w_in": w_in[l].astype(BF16),
            "qn_g": row(jnp.tile(swa_qn_g[l], 2)) * (SWA_HD ** -0.5), "kn_g": row(jnp.tile(swa_kn_g[l], 2)),
            "mqn_g": row(mem_qn_g[l]),
            "mu": row(rwkv_mu[l]), "w0": row(rwkv_w0[l]),
            "w2p": jnp.concatenate([rwkv_w2[l], zpad], axis=0),
            "a0": row(rwkv_a0[l]),
            "a2p": jnp.concatenate([zpad, rwkv_a2[l]], axis=0),
            "g2": rwkv_g2[l], "kk": row(rwkv_kk[l]), "ka": row(rwkv_ka[l]), "rk": row(rwkv_rk[l]),
            "ln_g": row(rwkv_ln_g[l]), "ln_b": row(rwkv_ln_b[l]),
            "sink": swa_sink[l],
            "w_branch": w_branch[l].astype(BF16), "w_out": w_out[l].astype(BF16),
            "norm2_g": row(norm2_g[l]), "w_up": w_up[l].astype(BF16), "conv_w": conv_w[l],
            "conv_b": row(conv_b[l]), "w_down": w_down[l].astype(BF16),
        }
        mk, mv = _mem_kv(mem_prompt, row(mem_norm_g[l]), w_mem_kv[l].astype(BF16), row(mem_kn_g[l]))
        yp, (k, v, s_new, sh_new, cv_new) = _layer(
            yp, lw, e, mk, mv,
            jnp.zeros((Bp, 1, RWKV_COLS), dt),
            jnp.zeros((Bp, RWKV_HEADS, RWKV_HD, RWKV_HD), dt),
            jnp.zeros((Bp, CONV_W - 1, D_FF), dt), None, None, prompt_tiles)
        kv_shape = (Bp, WINDOW, 2, SWA_HD)
        for lst, val in zip(outs_p, (k[:, -WINDOW:].reshape(kv_shape), v[:, -WINDOW:].reshape(kv_shape),
                                     mk.reshape(Bp, MEM_TOKENS, MEM_HEADS, MEM_HD),
                                     mv.reshape(Bp, MEM_TOKENS, MEM_HEADS, MEM_HD), s_new, sh_new, cv_new)):
            lst.append(val)
        ck = cache_swa_k[l].reshape(Bs, WINDOW, LANES)
        cv = cache_swa_v[l].reshape(Bs, WINDOW, LANES)
        ys, (k, v, s_new, sh_new, cv_new) = _layer(
            ys, lw, e, cache_mem_k[l].reshape(Bs, MEM_TOKENS, MEM_W), cache_mem_v[l].reshape(Bs, MEM_TOKENS, MEM_W),
            state_shift[l], state_rwkv[l], state_conv[l], ck, cv, sample_tiles)
        kv_shape = (Bs, WINDOW, 2, SWA_HD)
        kf = jnp.concatenate([ck, k], axis=1)[:, -WINDOW:].reshape(kv_shape)
        vf = jnp.concatenate([cv, v], axis=1)[:, -WINDOW:].reshape(kv_shape)
        for lst, val in zip(outs_s, (kf, vf, s_new, sh_new, cv_new)):
            lst.append(val)
    return (yp, ys) + tuple(jnp.stack(o) for o in outs_p) + tuple(jnp.stack(o) for o in outs_s)
```

```python
import functools
import math

import jax
import jax.numpy as jnp
from jax import lax
from jax.experimental import pallas as pl
from jax.experimental.pallas import tpu as pltpu

F32 = jnp.float32
BF16 = jnp.bfloat16
HIGHEST = lax.Precision.HIGHEST

D_MODEL = 1024
DEPTH = 2
CHUNK = 64
RWKV_HEADS = 8
RWKV_HD = 64
RWKV_W = 512
RWKV_COLS = 1792
GN_EPS = 64e-5
SWA_HEADS = 8
SWA_HD = 64
WINDOW = 128
MEM_TOKENS = 256
MEM_HEADS = 4
MEM_HD = 128
MEM_W = 512
N_BRANCH = 3
D_FF = 2816
CONV_W = 3
RMS_EPS = 1e-6

LANES = 128
PAIRS = RWKV_W // LANES
VMEM_LIMIT = 56 * 1024 * 1024
MXU_DIM = 256
FF_BLOCKS = ((0, 6 * MXU_DIM), (6 * MXU_DIM, D_FF))
UNROLL = 2

C_Q = RWKV_COLS
C_K = C_Q + SWA_HEADS * SWA_HD
C_V = C_K + LANES
C_QM = C_V + LANES
C_GT = C_QM + MEM_W
IN_COLS = C_GT + N_BRANCH * D_MODEL


def _dot(a, b, dims=((1,), (0,)), exact=False):
    if exact:
        return lax.dot_general(a, b, (dims, ((), ())), precision=HIGHEST, preferred_element_type=F32)
    return lax.dot_general(a.astype(BF16), b.astype(BF16), (dims, ((), ())), preferred_element_type=F32)


_NT = ((1,), (1,))
_TN = ((0,), (0,))


def _seg_sum(x, e):
    return jnp.dot(x.astype(BF16), e, preferred_element_type=F32)


def _sigmoid(x):
    return 1.0 / (1.0 + jnp.exp(-x))


def _const_spec(shape):
    n = len(shape)
    return pl.BlockSpec(shape, lambda *_: (0,) * n, pipeline_mode=pl.Buffered(1))


def _layer_spec(shape, l):
    n = len(shape)
    return pl.BlockSpec((None,) + tuple(shape), lambda *_: (l,) + (0,) * n, pipeline_mode=pl.Buffered(1))


def _params():
    return pltpu.CompilerParams(dimension_semantics=("arbitrary", "arbitrary"), vmem_limit_bytes=VMEM_LIMIT)


def _in_kernel(x_ref, g1_ref, w_ref, qng_ref, kng_ref, mqg_ref, e_ref,
               p_ref, q_ref, k_ref, v_ref, qm_ref):
    G, L, D = x_ref.shape
    R = G * L
    x = x_ref[...].reshape(R, D)
    h = x * lax.rsqrt(jnp.mean(x * x, axis=-1, keepdims=True) + RMS_EPS) * g1_ref[...]
    hb = h.astype(BF16)
    e = e_ref[...]

    def proj(c0, c1):
        return jnp.dot(hb, w_ref[:, c0:c1], preferred_element_type=F32)

    def head_rms(z, gain):
        return z * lax.rsqrt(_seg_sum(z * z, e) * (1.0 / SWA_HD) + RMS_EPS) * gain

    p_ref[...] = proj(0, RWKV_COLS).reshape(G, L, RWKV_COLS)
    zq = proj(C_Q, C_K)
    for s in range(SWA_HEADS * SWA_HD // LANES):
        sl = slice(s * LANES, (s + 1) * LANES)
        q_ref[:, :, sl] = head_rms(zq[:, sl], qng_ref[...]).astype(BF16).reshape(G, L, LANES)
    zkv = proj(C_K, C_QM)
    k_ref[...] = head_rms(zkv[:, :LANES], kng_ref[...]).reshape(G, L, LANES)
    v_ref[...] = zkv[:, LANES:].reshape(G, L, LANES)
    zqm = proj(C_QM, C_GT)
    for s in range(MEM_HEADS):
        sl = slice(s * MEM_HD, (s + 1) * MEM_HD)
        z = zqm[:, sl]
        z = z * lax.rsqrt(jnp.mean(z * z, axis=-1, keepdims=True) + RMS_EPS) * mqg_ref[...]
        qm_ref[:, :, sl] = z.astype(BF16).reshape(G, L, MEM_HD)


def _in_proj(x, g1, w_in_b, l, qng, kng, mqg, e, G, L):
    B, T, D = x.shape
    tile = lambda c: pl.BlockSpec((G, L, c), lambda i, j: (i, j, 0))
    outs = ((RWKV_COLS, F32), (SWA_HEADS * SWA_HD, BF16), (LANES, F32), (LANES, F32), (MEM_W, BF16))
    return pl.pallas_call(
        _in_kernel,
        grid=(B // G, T // L),
        in_specs=[tile(D), _const_spec((1, D)), _layer_spec((D, C_GT), l), _const_spec((1, LANES)),
                  _const_spec((1, LANES)), _const_spec((1, MEM_HD)), _const_spec((LANES, LANES))],
        out_specs=[tile(c) for c, _ in outs],
        out_shape=[jax.ShapeDtypeStruct((B, T, c), dt) for c, dt in outs],
        compiler_params=_params(),
        name="in_proj",
    )(x, g1, w_in_b, qng, kng, mqg, e)


def _rwkv_kernel(p_ref, sh_ref, s0_ref, mu_ref, w0_ref, w2_ref, a0_ref, a2_ref, g2_ref, kk_ref, ka_ref,
                 rk_ref, lng_ref, lnb_ref, e_ref,
                 o_ref, sout_ref,
                 prev_scr, st_scr, r_scr, k_scr, v_scr, am_scr, b_scr, lw_scr, g_scr, o_scr,
                 wr_scr, uo_scr, bkt_scr, vb_scr, dcol_scr):
    G, L, _ = p_ref.shape
    C = CHUNK
    nc = L // C
    j = pl.program_id(1)
    e = e_ref[...]

    @pl.when(j == 0)
    def _():
        st_scr[...] = s0_ref[...]
        prev_scr[...] = sh_ref[...]

    first_row = lax.broadcasted_iota(jnp.int32, (L, 1), 0) == 0
    for g in range(G):
        p = p_ref[g]
        shifted = jnp.where(first_row, prev_scr[g], pltpu.roll(p, 1, 0))
        pm = p + (shifted - p) * mu_ref[...]
        prev_scr[g] = p[L - 1:L, :]
        rows = slice(g * L, (g + 1) * L)
        r = pm[:, 0:RWKV_W]
        k = pm[:, RWKV_W:2 * RWKV_W]
        v = pm[:, 2 * RWKV_W:3 * RWKV_W]
        xwa = pm[:, 3 * RWKV_W:3 * RWKV_W + LANES]
        xg = pm[:, 3 * RWKV_W + LANES:RWKV_COLS]
        z = w0_ref[...] + _dot(jnp.tanh(xwa), w2_ref[...])
        a = _sigmoid(a0_ref[...] + _dot(xwa, a2_ref[...]))
        kkv = k * kk_ref[...]
        for s in range(PAIRS):
            sl = slice(s * LANES, (s + 1) * LANES)
            kks = kkv[:, sl]
            kkn = kks * lax.rsqrt(jnp.maximum(_seg_sum(kks * kks, e), 1e-24))
            am_scr[rows, sl] = -kkn
            b_scr[rows, sl] = kkn * a[:, sl]
        r_scr[rows, :] = r
        k_scr[rows, :] = k * (1.0 + (a - 1.0) * ka_ref[...])
        v_scr[rows, :] = v
        lw_scr[rows, :] = (-math.exp(-0.5)) * _sigmoid(z)
        g_scr[rows, :] = _dot(_sigmoid(xg), g2_ref[...])

    ri = lax.broadcasted_iota(jnp.int32, (C, C), 0)
    ci = lax.broadcasted_iota(jnp.int32, (C, C), 1)
    cumsum_mat = (ri >= ci).astype(BF16)
    rq = lax.broadcasted_iota(jnp.int32, (C, 2 * C), 0)
    cq = lax.broadcasted_iota(jnp.int32, (C, 2 * C), 1)
    strict_l = jnp.logical_and(cq < C, rq > cq).astype(F32)
    strict_r = jnp.logical_and(cq >= C, rq > cq - C).astype(F32)
    incl = (rq >= cq % C).astype(F32)
    incl_l = jnp.logical_and(cq < C, rq >= cq).astype(F32)
    incl_r = jnp.logical_and(cq >= C, rq >= cq - C).astype(F32)
    r2 = lax.broadcasted_iota(jnp.int32, (2 * C, 2 * C), 0)
    c2 = lax.broadcasted_iota(jnp.int32, (2 * C, 2 * C), 1)
    bd_mask = ((r2 // C) == (c2 // C)).astype(F32)
    lo = lax.broadcasted_iota(jnp.int32, (1, LANES), 1) < RWKV_HD

    def stack(z):
        return jnp.concatenate([jnp.where(lo, z, 0.0), jnp.where(lo, 0.0, z)], axis=0)

    def stack_other(z):
        return jnp.concatenate([jnp.where(lo, 0.0, z), jnp.where(lo, z, 0.0)], axis=0)

    def prepare(t, carry):
        units = []
        for n in range(UNROLL):
            i = t * UNROLL + n
            rows = pl.ds(pl.multiple_of(i * C, C), C)
            lw = lw_scr[rows, :]
            lw_hi = lw.astype(BF16)
            lw_lo = (lw - lw_hi.astype(F32)).astype(BF16)
            cum = (jnp.dot(cumsum_mat, lw_hi, preferred_element_type=F32)
                   + jnp.dot(cumsum_mat, lw_lo, preferred_element_type=F32))
            cum_end = cum[C - 1:C, :]
            inv = jnp.exp(-cum)
            dec_rest = jnp.exp(cum_end - cum)
            dec_end = jnp.exp(cum_end)
            kc = k_scr[rows, :]
            bc = b_scr[rows, :]
            r_t = r_scr[rows, :] * jnp.exp(cum)
            a_t = am_scr[rows, :] * jnp.exp(cum - lw)
            b_t = bc * inv
            k_t = kc * inv
            b_e = bc * dec_rest
            k_e = kc * dec_rest
            vc = v_scr[rows, :]
            for s in range(PAIRS):
                sl = slice(s * LANES, (s + 1) * LANES)
                bkt_scr[i, s] = jnp.concatenate([b_e[:, sl], k_e[:, sl]], axis=0).T.astype(BF16)
                vb_scr[i, s] = vc[:, sl].astype(BF16)
                dcol_scr[i, s] = jnp.broadcast_to(dec_end[:, sl], (LANES, LANES)).T
                units.append((i, s, a_t[:, sl], r_t[:, sl], b_t[:, sl], k_t[:, sl], vc[:, sl]))
        N = range(len(units))
        lhs = [jnp.concatenate([u[2], u[3]], axis=0).astype(BF16) for u in units]
        nb = [_dot(lhs[n], stack(units[n][4]), _NT) for n in N]
        nk = [_dot(lhs[n], stack(units[n][5]), _NT) for n in N]
        m = [jnp.concatenate([nb[n][0:C] * strict_l, nb[n][0:C] * strict_r], axis=0) for n in N]
        mk = [jnp.concatenate([nk[n][0:C] * strict_l, nk[n][0:C] * strict_r], axis=0) for n in N]
        vs = [stack(units[n][6]).astype(BF16) for n in N]
        vsw = [stack_other(pltpu.roll(units[n][6], RWKV_HD, 1)).astype(BF16) for n in N]
        y = [stack(units[n][2]) + _dot(mk[n], vsw[n]) for n in N]
        for it in range(6):
            if it < 5:
                my = [_dot(m[n], jnp.concatenate([m[n], y[n]], axis=1)) for n in N]
                y = [y[n] + my[n][:, LANES:] for n in N]
                m = [my[n][:, :LANES] for n in N]
            else:
                my = [_dot(m[n], y[n]) for n in N]
                y = [y[n] + my[n] for n in N]
        arb = [jnp.concatenate([nb[n][C:2 * C] * incl_l, nb[n][C:2 * C] * incl_r], axis=0) for n in N]
        ay = [_dot(arb[n], y[n]) for n in N]
        akv = [_dot(nk[n][C:2 * C] * incl, vs[n]) for n in N]
        for n in N:
            i, s = units[n][0], units[n][1]
            wf = jnp.where(lo, y[n][0:C], y[n][C:2 * C])
            rw = units[n][3] + jnp.where(lo, ay[n][0:C], ay[n][C:2 * C])
            swapped = jnp.concatenate([jnp.where(lo, y[n][C:2 * C], y[n][0:C]),
                                       jnp.where(lo, ay[n][C:2 * C], ay[n][0:C])], axis=0)
            uo = pltpu.roll(swapped, RWKV_HD, 1)
            wr_scr[i, s] = jnp.concatenate([wf, rw], axis=0).astype(BF16)
            uo_scr[i, s] = jnp.concatenate([uo[0:C], uo[C:2 * C] + akv[n]], axis=0)
        return carry

    lax.fori_loop(0, G * nc // UNROLL, prepare, 0)

    def advance(i, carry):
        g = i // nc
        rows = pl.ds(pl.multiple_of(i * C, C), C)
        hs = [st_scr[g, s] for s in range(PAIRS)]
        new_h, outs = [], []
        for s in range(PAIRS):
            y = jnp.dot(wr_scr[i, s], hs[s].astype(BF16), preferred_element_type=F32) + uo_scr[i, s]
            uvb = jnp.concatenate([y[0:C].astype(BF16), vb_scr[i, s]], axis=0)
            upd = jnp.dot(bkt_scr[i, s], uvb, preferred_element_type=F32)
            new_h.append(hs[s] * dcol_scr[i, s] + upd * bd_mask)
            outs.append(y[C:2 * C])
        for s in range(PAIRS):
            st_scr[g, s] = new_h[s]
            o_scr[rows, s * LANES:(s + 1) * LANES] = outs[s]
        return carry

    lax.fori_loop(0, G * nc, advance, 0)

    for s in range(PAIRS):
        sl = slice(s * LANES, (s + 1) * LANES)
        o = o_scr[:, sl]
        d = o - _seg_sum(o, e) * (1.0 / RWKV_HD)
        var = _seg_sum(d * d, e) * (1.0 / RWKV_HD)
        y = d * lax.rsqrt(var + GN_EPS) * lng_ref[:, sl] + lnb_ref[:, sl]
        bonus = _seg_sum(r_scr[:, sl] * k_scr[:, sl] * rk_ref[:, sl], e) * v_scr[:, sl]
        o_ref[:, :, sl] = ((y + bonus) * g_scr[:, sl]).astype(BF16).reshape(G, L, LANES)

    @pl.when(j == pl.num_programs(1) - 1)
    def _():
        sout_ref[...] = st_scr[...]


def _rwkv(p, shift_in, s0_pairs, lw, e, G, L):
    B, T, _ = p.shape
    R = G * L
    vec = lambda c: _const_spec((1, c))
    buf = lambda: pltpu.VMEM((R, RWKV_W), F32)
    per_chunk = lambda rows, dt: pltpu.VMEM((R // CHUNK, PAIRS, rows, LANES), dt)
    return pl.pallas_call(
        _rwkv_kernel,
        grid=(B // G, T // L),
        in_specs=[pl.BlockSpec((G, L, RWKV_COLS), lambda i, j: (i, j, 0)),
                  pl.BlockSpec((G, 1, RWKV_COLS), lambda i, j: (i, 0, 0)),
                  pl.BlockSpec((G, PAIRS, LANES, LANES), lambda i, j: (i, 0, 0, 0)),
                  vec(RWKV_COLS), vec(RWKV_W), _const_spec((LANES, RWKV_W)), vec(RWKV_W),
                  _const_spec((LANES, RWKV_W)), _const_spec((LANES, RWKV_W)), vec(RWKV_W), vec(RWKV_W),
                  vec(RWKV_W), vec(RWKV_W), vec(RWKV_W), _const_spec((LANES, LANES))],
        out_specs=[pl.BlockSpec((G, L, RWKV_W), lambda i, j: (i, j, 0)),
                   pl.BlockSpec((G, PAIRS, LANES, LANES), lambda i, j: (i, 0, 0, 0))],
        out_shape=[jax.ShapeDtypeStruct((B, T, RWKV_W), BF16),
                   jax.ShapeDtypeStruct((B, PAIRS, LANES, LANES), F32)],
        scratch_shapes=[pltpu.VMEM((G, 1, RWKV_COLS), F32), pltpu.VMEM((G, PAIRS, LANES, LANES), F32)]
                       + [buf() for _ in range(8)]
                       + [per_chunk(LANES, BF16), per_chunk(LANES, F32), per_chunk(LANES, BF16),
                          per_chunk(CHUNK, BF16), per_chunk(LANES, F32)],
        compiler_params=_params(),
        name="rwkv7",
    )(p, shift_in, s0_pairs, lw["mu"], lw["w0"], lw["w2p"], lw["a0"], lw["a2p"], lw["g2"], lw["kk"],
      lw["ka"], lw["rk"], lw["ln_g"], lw["ln_b"], e)


def _swa_kernel(sink_ref, q_ref, k_ref, v_ref, hk_ref, hv_ref, o_ref, kd, vd, *, has_cache):
    G, L, _ = q_ref.shape
    C = CHUNK
    nc = L // C
    KB = WINDOW + C
    GROUP = 4
    j = pl.program_id(1)
    lo = lax.broadcasted_iota(jnp.int32, (1, LANES), 1) < SWA_HD
    qi = lax.broadcasted_iota(jnp.int32, (C, KB), 0)
    kj = lax.broadcasted_iota(jnp.int32, (C, KB), 1)
    dist = jnp.abs(WINDOW + qi - kj).astype(F32)
    kj4 = lax.broadcasted_iota(jnp.int32, (GROUP * C, KB), 1)
    ones = jnp.ones((WINDOW + L, LANES), F32)
    neg_pad = jnp.full((GROUP * C, 2 * LANES - KB), -jnp.inf, F32)
    bias = [jnp.concatenate([(2.0 ** -(GROUP * kv + h + 1)) * dist for h in range(GROUP)], axis=0)
            for kv in range(2)]
    sink = [jnp.concatenate([jnp.full((C, LANES), sink_ref[GROUP * kv + h], F32) for h in range(GROUP)], axis=0)
            for kv in range(2)]
    for g in range(G):
        kc = jnp.concatenate([hk_ref[g], k_ref[g]], axis=0)
        vc = jnp.concatenate([hv_ref[g], v_ref[g]], axis=0)
        ks = pltpu.roll(kc, SWA_HD, 1)
        vs = pltpu.roll(vc, SWA_HD, 1)
        kd[g, 0] = jnp.where(lo, kc, ks).astype(BF16)
        kd[g, 1] = jnp.where(lo, ks, kc).astype(BF16)
        vd[g, 0] = jnp.concatenate([jnp.where(lo, vc, vs), ones], axis=1).astype(BF16)
        vd[g, 1] = jnp.concatenate([jnp.where(lo, vs, vc), ones], axis=1).astype(BF16)

    def chunks(t, carry, mask_halo):
        units = []
        for n in range(UNROLL):
            i = t * UNROLL + n
            for kv in range(2):
                units.append((i // nc, pl.multiple_of((i % nc) * C, C), kv))
        scs = []
        for g, off, kv in units:
            q0 = q_ref[g, pl.ds(off, C), (2 * kv) * LANES:(2 * kv + 1) * LANES]
            q1 = q_ref[g, pl.ds(off, C), (2 * kv + 1) * LANES:(2 * kv + 2) * LANES]
            zero = jnp.zeros_like(q0)
            lhs = jnp.concatenate([jnp.where(lo, q0, zero), jnp.where(lo, zero, q0),
                                   jnp.where(lo, q1, zero), jnp.where(lo, zero, q1)], axis=0)
            sc = lax.dot_general(lhs, kd[g, kv, pl.ds(off, KB), :], ((_NT), ((), ())),
                                 preferred_element_type=F32) - bias[kv]
            if mask_halo:
                sc = jnp.where(kj4 + off < WINDOW, -jnp.inf, sc)
            scs.append(sc)
        exs, sink_terms = [], []
        for (g, off, kv), sc in zip(units, scs):
            folded = jnp.maximum(sc[:, :LANES], jnp.concatenate([sc[:, LANES:], neg_pad], axis=1))
            mx = jnp.maximum(jnp.broadcast_to(jnp.max(folded, axis=-1, keepdims=True), (GROUP * C, LANES)),
                             sink[kv])
            ex = jnp.concatenate([jnp.exp(sc[:, :LANES] - mx), jnp.exp(sc[:, LANES:] - mx[:, :KB - LANES])],
                                 axis=1)
            exs.append(ex.astype(BF16))
            sink_terms.append(jnp.exp(sink[kv] - mx))
        for (g, off, kv), ex, st in zip(units, exs, sink_terms):
            pvd = jnp.dot(ex, vd[g, kv, pl.ds(off, KB), :], preferred_element_type=F32)
            pv = pvd[:, :LANES] / (pvd[:, LANES:] + st)
            pv = pv.astype(BF16)
            o_ref[g, pl.ds(off, C), (2 * kv) * LANES:(2 * kv + 1) * LANES] = jnp.where(lo, pv[0:C], pv[C:2 * C])
            o_ref[g, pl.ds(off, C), (2 * kv + 1) * LANES:(2 * kv + 2) * LANES] = jnp.where(
                lo, pv[2 * C:3 * C], pv[3 * C:4 * C])
        return carry

    steps = G * nc // UNROLL
    if has_cache:
        lax.fori_loop(0, steps, functools.partial(chunks, mask_halo=False), 0)
    else:
        @pl.when(j == 0)
        def _():
            lax.fori_loop(0, steps, functools.partial(chunks, mask_halo=True), 0)

        @pl.when(j != 0)
        def _():
            lax.fori_loop(0, steps, functools.partial(chunks, mask_halo=False), 0)


def _swa(q, k, v, halo_k, halo_v, sink, G, L, has_cache):
    B, T, _ = q.shape
    if has_cache:
        halo = pl.BlockSpec((G, WINDOW, LANES), lambda i, j: (i, 0, 0))
    else:
        per = L // WINDOW
        halo = pl.BlockSpec((G, WINDOW, LANES), lambda i, j: (i, jnp.maximum(j * per - 1, 0), 0))
    tile = lambda c: pl.BlockSpec((G, L, c), lambda i, j: (i, j, 0))
    cat = lambda cols: pltpu.VMEM((G, 2, WINDOW + L, cols), BF16)
    return pl.pallas_call(
        functools.partial(_swa_kernel, has_cache=has_cache),
        grid=(B // G, T // L),
        in_specs=[pl.BlockSpec(memory_space=pltpu.SMEM), tile(SWA_HEADS * SWA_HD), tile(LANES), tile(LANES),
                  halo, halo],
        out_specs=tile(SWA_HEADS * SWA_HD),
        out_shape=jax.ShapeDtypeStruct((B, T, SWA_HEADS * SWA_HD), BF16),
        scratch_shapes=[cat(LANES), cat(2 * LANES)],
        compiler_params=_params(),
        name="swa",
    )(sink, q, k, v, halo_k, halo_v)


def _mem_kv_kernel(m_ref, g_ref, w_ref, kng_ref, mk_ref, mv_ref):
    G, M, D = m_ref.shape
    x = m_ref[...].reshape(G * M, D)
    h = (x * lax.rsqrt(jnp.mean(x * x, axis=-1, keepdims=True) + RMS_EPS) * g_ref[...]).astype(BF16)
    for s in range(MEM_HEADS):
        z = jnp.dot(h, w_ref[:, s * MEM_HD:(s + 1) * MEM_HD], preferred_element_type=F32)
        z = z * lax.rsqrt(jnp.mean(z * z, axis=-1, keepdims=True) + RMS_EPS) * kng_ref[...]
        mk_ref[:, :, s * MEM_HD:(s + 1) * MEM_HD] = z.reshape(G, M, MEM_HD)
    mv_ref[...] = jnp.dot(h, w_ref[:, MEM_W:], preferred_element_type=F32).reshape(G, M, MEM_W)


def _mem_kv(mem, g, w_b, l, kng):
    B, M, D = mem.shape
    blk = lambda c: pl.BlockSpec((1, M, c), lambda i, j: (i, 0, 0))
    return pl.pallas_call(
        _mem_kv_kernel,
        grid=(B, 1),
        in_specs=[blk(D), _const_spec((1, D)), _layer_spec((D, 2 * MEM_W), l), _const_spec((1, MEM_HD))],
        out_specs=[blk(MEM_W), blk(MEM_W)],
        out_shape=[jax.ShapeDtypeStruct((B, M, MEM_W), F32)] * 2,
        compiler_params=_params(),
        name="mem_kv",
    )(mem, g, w_b, kng)


def _mem_att_kernel(q_ref, mk_ref, mv_ref, o_ref):
    G, L, _ = q_ref.shape
    for g in range(G):
        for s in range(MEM_HEADS):
            sl = slice(s * MEM_HD, (s + 1) * MEM_HD)
            sc = _dot(q_ref[g, :, sl], mk_ref[g, :, sl], _NT) * (MEM_HD ** -0.5)
            ex = jnp.exp(sc - jnp.max(sc, axis=-1, keepdims=True))
            pr = ex / jnp.sum(ex, axis=-1, keepdims=True)
            o_ref[g, :, sl] = _dot(pr, mv_ref[g, :, sl]).astype(BF16)


def _mem_att(qm, mk, mv, G, L):
    B, T, _ = qm.shape
    tile = pl.BlockSpec((G, L, MEM_W), lambda i, j: (i, j, 0))
    mem = pl.BlockSpec((G, MEM_TOKENS, MEM_W), lambda i, j: (i, 0, 0))
    return pl.pallas_call(
        _mem_att_kernel,
        grid=(B // G, T // L),
        in_specs=[tile, mem, mem],
        out_specs=tile,
        out_shape=jax.ShapeDtypeStruct((B, T, MEM_W), BF16),
        compiler_params=_params(),
        name="mem_att",
    )(qm, mk, mv)


def _merge_kernel(x_ref, oa_ref, ob_ref, om_ref, g1_ref, wg_ref, wb_ref, wo_ref, y_ref):
    G, L, D = x_ref.shape
    R = G * L
    x = x_ref[...].reshape(R, D)
    hb = (x * lax.rsqrt(jnp.mean(x * x, axis=-1, keepdims=True) + RMS_EPS) * g1_ref[...]).astype(BF16)
    mix = None
    for n, o_ref in enumerate((oa_ref, ob_ref, om_ref)):
        br = jnp.dot(o_ref[...].reshape(R, RWKV_W), wb_ref[n], preferred_element_type=F32)
        gate = _sigmoid(jnp.dot(hb, wg_ref[:, n * D:(n + 1) * D], preferred_element_type=F32))
        mix = gate * br if mix is None else mix + gate * br
    y_ref[...] = (x + jnp.dot(mix.astype(BF16), wo_ref[...], preferred_element_type=F32)).reshape(G, L, D)


def _merge(x, oa, ob, om, g1, w_in_b, wb_b, wo_b, l, G, L):
    B, T, D = x.shape
    tile = lambda c: pl.BlockSpec((G, L, c), lambda i, j: (i, j, 0))
    return pl.pallas_call(
        _merge_kernel,
        grid=(B // G, T // L),
        in_specs=[tile(D), tile(RWKV_W), tile(RWKV_W), tile(MEM_W), _const_spec((1, D)),
                  pl.BlockSpec((None, D, N_BRANCH * D), lambda *_: (l, 0, C_GT // (N_BRANCH * D)),
                               pipeline_mode=pl.Buffered(1)),
                  _layer_spec((N_BRANCH, RWKV_W, D), l), _layer_spec((D, D), l)],
        out_specs=tile(D),
        out_shape=jax.ShapeDtypeStruct((B, T, D), F32),
        compiler_params=_params(),
        name="merge",
    )(x, oa, ob, om, g1, w_in_b, wb_b, wo_b)


def _ffn_kernel(x_ref, cin_ref, g2_ref, wu_ref, cw_ref, cb_ref, wd_ref, y_ref, cout_ref, carry):
    G, L, D = x_ref.shape
    R = G * L
    j = pl.program_id(1)

    @pl.when(j == 0)
    def _():
        carry[...] = cin_ref[...]

    x = x_ref[...].reshape(R, D)
    hb = (x * lax.rsqrt(jnp.mean(x * x, axis=-1, keepdims=True) + RMS_EPS) * g2_ref[...]).astype(BF16)
    row = lax.broadcasted_iota(jnp.int32, (L, 1), 0)
    acc = x
    for c0, c1 in FF_BLOCKS:
        cs = slice(c0, c1)
        a_in = jnp.dot(hb, wu_ref[:, cs], preferred_element_type=F32)
        u = jnp.dot(hb, wu_ref[:, D_FF + c0:D_FF + c1], preferred_element_type=F32)
        convs = []
        for g in range(G):
            a = a_in[g * L:(g + 1) * L]
            prev = carry[g, :, cs]
            a1 = jnp.where(row == 0, prev[1:2], pltpu.roll(a, 1, 0))
            a2 = jnp.where(row == 0, prev[0:1], jnp.where(row == 1, prev[1:2], pltpu.roll(a, 2, 0)))
            carry[g, :, cs] = a[L - 2:L]
            convs.append(cb_ref[:, cs] + a2 * cw_ref[0:1, cs] + a1 * cw_ref[1:2, cs] + a * cw_ref[2:3, cs])
        c = convs[0] if G == 1 else jnp.concatenate(convs, axis=0)
        gelu = 0.5 * c * (1.0 + jnp.tanh(0.7978845608028654 * (c + 0.044715 * (c * c * c))))
        acc = acc + jnp.dot((gelu * u).astype(BF16), wd_ref[cs, :], preferred_element_type=F32)
    y_ref[...] = acc.reshape(G, L, D)

    @pl.when(j == pl.num_programs(1) - 1)
    def _():
        cout_ref[...] = carry[...]


def _ffn(x, conv_in, g2, wu_b, cw, cb, wd_b, l, G, L):
    B, T, D = x.shape
    tile = pl.BlockSpec((G, L, D), lambda i, j: (i, j, 0))
    st = pl.BlockSpec((G, CONV_W - 1, D_FF), lambda i, j: (i, 0, 0))
    return pl.pallas_call(
        _ffn_kernel,
        grid=(B // G, T // L),
        in_specs=[tile, st, _const_spec((1, D)), _layer_spec((D, 2 * D_FF), l), _const_spec((CONV_W, D_FF)),
                  _const_spec((1, D_FF)), _layer_spec((D_FF, D), l)],
        out_specs=[tile, st],
        out_shape=[jax.ShapeDtypeStruct((B, T, D), F32), jax.ShapeDtypeStruct((B, CONV_W - 1, D_FF), F32)],
        scratch_shapes=[pltpu.VMEM((G, CONV_W - 1, D_FF), F32)],
        compiler_params=_params(),
        name="conv_ffn",
    )(x, conv_in, g2, wu_b, cw, cb, wd_b)


def _state_to_pairs(s):
    B = s.shape[0]
    sr = s.reshape(B, PAIRS, 2, RWKV_HD, RWKV_HD)
    eye = jnp.eye(2, dtype=s.dtype)
    bd = sr[:, :, :, :, None, :] * eye[None, None, :, None, :, None]
    return bd.reshape(B, PAIRS, LANES, LANES)


def _pairs_to_state(sp):
    B = sp.shape[0]
    sr = sp.reshape(B, PAIRS, 2, RWKV_HD, 2, RWKV_HD)
    return jnp.stack([sr[:, :, 0, :, 0, :], sr[:, :, 1, :, 1, :]], axis=2).reshape(B, RWKV_HEADS, RWKV_HD, RWKV_HD)


def _layer(x, lw, e, mk, mv, shift_in, s0, conv_in, halo_k, halo_v, tiles):
    has_cache = halo_k is not None
    G, L_in, L_mix, L_out = tiles
    p, q, k, v, qm = _in_proj(x, lw["norm1_g"], lw["w_in"], lw["layer"], lw["qn_g"], lw["kn_g"], lw["mqn_g"], e, G, L_in)
    oa, h_pairs = _rwkv(p, shift_in, jnp.swapaxes(_state_to_pairs(s0), -1, -2), lw, e, G, L_mix)
    if has_cache:
        ob = _swa(q, k, v, halo_k, halo_v, lw["sink"], G, L_mix, True)
    else:
        ob = _swa(q, k, v, k, v, lw["sink"], G, L_mix, False)
    om = _mem_att(qm, mk, mv, G, L_mix)
    x = _merge(x, oa, ob, om, lw["norm1_g"], lw["w_in"], lw["w_branch"], lw["w_out"], lw["layer"], G, L_out)
    x, conv_new = _ffn(x, conv_in, lw["norm2_g"], lw["w_up"], lw["conv_w"], lw["conv_b"], lw["w_down"], lw["layer"], G, L_out)
    return x, (k, v, _pairs_to_state(jnp.swapaxes(h_pairs, -1, -2)), p[:, -1:, :], conv_new)


def kernel(x_prompt, x_sample, cache_swa_k, cache_swa_v, cache_mem_k, cache_mem_v, state_rwkv, state_shift, state_conv, mem_prompt, norm1_g, w_in, rwkv_mu, rwkv_w0, rwkv_w2, rwkv_a0, rwkv_a2, rwkv_g2, rwkv_kk, rwkv_ka, rwkv_rk, rwkv_ln_g, rwkv_ln_b, swa_qn_g, swa_kn_g, swa_sink, mem_norm_g, w_mem_kv, mem_qn_g, mem_kn_g, w_branch, w_out, norm2_g, w_up, conv_w, conv_b, w_down):
    Bp, Tp, _ = x_prompt.shape
    Bs, Ts, _ = x_sample.shape
    dt = x_prompt.dtype
    half = jnp.arange(LANES) // RWKV_HD
    e = (half[:, None] == half[None, :]).astype(BF16)
    row = lambda a: a.reshape(1, -1)
    zpad = jnp.zeros((LANES - 64, RWKV_W), dt)

    w_in_b, w_branch_b, w_out_b, w_up_b, w_down_b, w_mem_kv_b = (
        w.astype(BF16) for w in (w_in, w_branch, w_out, w_up, w_down, w_mem_kv))
    yp, ys = x_prompt, x_sample
    outs_p = [[] for _ in range(7)]
    outs_s = [[] for _ in range(5)]
    prompt_tiles = (1, 512, 256, 512)
    sample_tiles = (8, Ts, Ts, Ts)
    for l in range(DEPTH):
        lw = {
            "layer": l, "norm1_g": row(norm1_g[l]), "w_in": w_in_b,
            "qn_g": row(jnp.tile(swa_qn_g[l], 2)) * (SWA_HD ** -0.5), "kn_g": row(jnp.tile(swa_kn_g[l], 2)),
            "mqn_g": row(mem_qn_g[l]),
            "mu": row(rwkv_mu[l]), "w0": row(rwkv_w0[l]),
            "w2p": jnp.concatenate([rwkv_w2[l], zpad], axis=0),
            "a0": row(rwkv_a0[l]),
            "a2p": jnp.concatenate([zpad, rwkv_a2[l]], axis=0),
            "g2": rwkv_g2[l], "kk": row(rwkv_kk[l]), "ka": row(rwkv_ka[l]), "rk": row(rwkv_rk[l]),
            "ln_g": row(rwkv_ln_g[l]), "ln_b": row(rwkv_ln_b[l]),
            "sink": swa_sink[l],
            "w_branch": w_branch_b, "w_out": w_out_b,
            "norm2_g": row(norm2_g[l]), "w_up": w_up_b, "conv_w": conv_w[l],
            "conv_b": row(conv_b[l]), "w_down": w_down_b,
        }
        mk, mv = _mem_kv(mem_prompt, row(mem_norm_g[l]), w_mem_kv_b, l, row(mem_kn_g[l]))
        yp, (k, v, s_new, sh_new, cv_new) = _layer(
            yp, lw, e, mk, mv,
            jnp.zeros((Bp, 1, RWKV_COLS), dt),
            jnp.zeros((Bp, RWKV_HEADS, RWKV_HD, RWKV_HD), dt),
            jnp.zeros((Bp, CONV_W - 1, D_FF), dt), None, None, prompt_tiles)
        kv_shape = (Bp, WINDOW, 2, SWA_HD)
        for lst, val in zip(outs_p, (k[:, -WINDOW:].reshape(kv_shape), v[:, -WINDOW:].reshape(kv_shape),
                                     mk.reshape(Bp, MEM_TOKENS, MEM_HEADS, MEM_HD),
                                     mv.reshape(Bp, MEM_TOKENS, MEM_HEADS, MEM_HD), s_new, sh_new, cv_new)):
            lst.append(val)
        ck = cache_swa_k[l].reshape(Bs, WINDOW, LANES)
        cv = cache_swa_v[l].reshape(Bs, WINDOW, LANES)
        ys, (k, v, s_new, sh_new, cv_new) = _layer(
            ys, lw, e, cache_mem_k[l].reshape(Bs, MEM_TOKENS, MEM_W), cache_mem_v[l].reshape(Bs, MEM_TOKENS, MEM_W),
            state_shift[l], state_rwkv[l], state_conv[l], ck, cv, sample_tiles)
        kv_shape = (Bs, WINDOW, 2, SWA_HD)
        kf = jnp.concatenate([ck, k], axis=1)[:, -WINDOW:].reshape(kv_shape)
        vf = jnp.concatenate([cv, v], axis=1)[:, -WINDOW:].reshape(kv_shape)
        for lst, val in zip(outs_s, (kf, vf, s_new, sh_new, cv_new)):
            lst.append(val)
    return (yp, ys) + tuple(jnp.stack(o) for o in outs_p) + tuple(jnp.stack(o) for o in outs_s)
```

```python
import functools
import math

import jax
import jax.numpy as jnp
from jax import lax
from jax.experimental import pallas as pl
from jax.experimental.pallas import tpu as pltpu

F32 = jnp.float32
BF16 = jnp.bfloat16
HIGHEST = lax.Precision.HIGHEST

D_MODEL = 1024
DEPTH = 2
CHUNK = 64
RWKV_HEADS = 8
RWKV_HD = 64
RWKV_W = 512
RWKV_COLS = 1792
GN_EPS = 64e-5
SWA_HEADS = 8
SWA_HD = 64
WINDOW = 128
MEM_TOKENS = 256
MEM_HEADS = 4
MEM_HD = 128
MEM_W = 512
N_BRANCH = 3
D_FF = 2816
CONV_W = 3
RMS_EPS = 1e-6

LANES = 128
PAIRS = RWKV_W // LANES
VMEM_LIMIT = 56 * 1024 * 1024
MXU_DIM = 256
FF_BLOCKS = ((0, 6 * MXU_DIM), (6 * MXU_DIM, D_FF))
UNROLL = 2

C_Q = RWKV_COLS
C_K = C_Q + SWA_HEADS * SWA_HD
C_V = C_K + LANES
C_QM = C_V + LANES
C_GT = C_QM + MEM_W
IN_COLS = C_GT + N_BRANCH * D_MODEL


def _dot(a, b, dims=((1,), (0,)), exact=False):
    if exact:
        return lax.dot_general(a, b, (dims, ((), ())), precision=HIGHEST, preferred_element_type=F32)
    return lax.dot_general(a.astype(BF16), b.astype(BF16), (dims, ((), ())), preferred_element_type=F32)


_NT = ((1,), (1,))
_TN = ((0,), (0,))


def _seg_sum(x, e):
    return jnp.dot(x.astype(BF16), e, preferred_element_type=F32)


def _sigmoid(x):
    return 1.0 / (1.0 + jnp.exp(-x))


def _const_spec(shape):
    n = len(shape)
    return pl.BlockSpec(shape, lambda *_: (0,) * n, pipeline_mode=pl.Buffered(1))


def _layer_spec(shape, l):
    n = len(shape)
    return pl.BlockSpec((None,) + tuple(shape), lambda *_: (l,) + (0,) * n, pipeline_mode=pl.Buffered(1))


def _params():
    return pltpu.CompilerParams(dimension_semantics=("arbitrary", "arbitrary"), vmem_limit_bytes=VMEM_LIMIT)


def _in_kernel(x_ref, g1_ref, w_ref, qng_ref, kng_ref, mqg_ref, e_ref,
               p_ref, q_ref, k_ref, v_ref, qm_ref):
    G, L, D = x_ref.shape
    R = G * L
    x = x_ref[...].reshape(R, D)
    h = x * lax.rsqrt(jnp.mean(x * x, axis=-1, keepdims=True) + RMS_EPS) * g1_ref[...]
    hb = h.astype(BF16)
    e = e_ref[...]

    def proj(c0, c1):
        return jnp.dot(hb, w_ref[:, c0:c1], preferred_element_type=F32)

    def head_rms(z, gain):
        return z * lax.rsqrt(_seg_sum(z * z, e) * (1.0 / SWA_HD) + RMS_EPS) * gain

    p_ref[...] = proj(0, RWKV_COLS).reshape(G, L, RWKV_COLS)
    zq = proj(C_Q, C_K)
    for s in range(SWA_HEADS * SWA_HD // LANES):
        sl = slice(s * LANES, (s + 1) * LANES)
        q_ref[:, :, sl] = head_rms(zq[:, sl], qng_ref[...]).astype(BF16).reshape(G, L, LANES)
    zkv = proj(C_K, C_QM)
    k_ref[...] = head_rms(zkv[:, :LANES], kng_ref[...]).reshape(G, L, LANES)
    v_ref[...] = zkv[:, LANES:].reshape(G, L, LANES)
    zqm = proj(C_QM, C_GT)
    for s in range(MEM_HEADS):
        sl = slice(s * MEM_HD, (s + 1) * MEM_HD)
        z = zqm[:, sl]
        z = z * lax.rsqrt(jnp.mean(z * z, axis=-1, keepdims=True) + RMS_EPS) * mqg_ref[...]
        qm_ref[:, :, sl] = z.astype(BF16).reshape(G, L, MEM_HD)


def _in_proj(x, g1, w_in_b, l, qng, kng, mqg, e, G, L):
    B, T, D = x.shape
    tile = lambda c: pl.BlockSpec((G, L, c), lambda i, j: (i, j, 0))
    outs = ((RWKV_COLS, F32), (SWA_HEADS * SWA_HD, BF16), (LANES, F32), (LANES, F32), (MEM_W, BF16))
    return pl.pallas_call(
        _in_kernel,
        grid=(B // G, T // L),
        in_specs=[tile(D), _const_spec((1, D)), _layer_spec((D, C_GT), l), _const_spec((1, LANES)),
                  _const_spec((1, LANES)), _const_spec((1, MEM_HD)), _const_spec((LANES, LANES))],
        out_specs=[tile(c) for c, _ in outs],
        out_shape=[jax.ShapeDtypeStruct((B, T, c), dt) for c, dt in outs],
        compiler_params=_params(),
        name="in_proj",
    )(x, g1, w_in_b, qng, kng, mqg, e)


def _rwkv_kernel(p_ref, sh_ref, s0_ref, mu_ref, w0_ref, w2_ref, a0_ref, a2_ref, g2_ref, kk_ref, ka_ref,
                 rk_ref, lng_ref, lnb_ref, e_ref,
                 o_ref, sout_ref,
                 prev_scr, st_scr, r_scr, k_scr, v_scr, am_scr, b_scr, lw_scr, g_scr, o_scr,
                 wr_scr, uo_scr, bkt_scr, vb_scr, dcol_scr):
    G, L, _ = p_ref.shape
    C = CHUNK
    nc = L // C
    j = pl.program_id(1)
    e = e_ref[...]

    @pl.when(j == 0)
    def _():
        zero = jnp.zeros((RWKV_HD, RWKV_HD), F32)
        for g in range(G):
            for s in range(PAIRS):
                top = jnp.concatenate([s0_ref[g, 2 * s], zero], axis=1)
                bot = jnp.concatenate([zero, s0_ref[g, 2 * s + 1]], axis=1)
                st_scr[g, s] = jnp.concatenate([top, bot], axis=0).T
        prev_scr[...] = sh_ref[...]

    first_row = lax.broadcasted_iota(jnp.int32, (L, 1), 0) == 0
    for g in range(G):
        p = p_ref[g]
        shifted = jnp.where(first_row, prev_scr[g], pltpu.roll(p, 1, 0))
        pm = p + (shifted - p) * mu_ref[...]
        prev_scr[g] = p[L - 1:L, :]
        rows = slice(g * L, (g + 1) * L)
        r = pm[:, 0:RWKV_W]
        k = pm[:, RWKV_W:2 * RWKV_W]
        v = pm[:, 2 * RWKV_W:3 * RWKV_W]
        xwa = pm[:, 3 * RWKV_W:3 * RWKV_W + LANES]
        xg = pm[:, 3 * RWKV_W + LANES:RWKV_COLS]
        z = w0_ref[...] + _dot(jnp.tanh(xwa), w2_ref[...])
        a = _sigmoid(a0_ref[...] + _dot(xwa, a2_ref[...]))
        kkv = k * kk_ref[...]
        for s in range(PAIRS):
            sl = slice(s * LANES, (s + 1) * LANES)
            kks = kkv[:, sl]
            kkn = kks * lax.rsqrt(jnp.maximum(_seg_sum(kks * kks, e), 1e-24))
            am_scr[rows, sl] = -kkn
            b_scr[rows, sl] = kkn * a[:, sl]
        r_scr[rows, :] = r
        k_scr[rows, :] = k * (1.0 + (a - 1.0) * ka_ref[...])
        v_scr[rows, :] = v
        lw_scr[rows, :] = (-math.exp(-0.5)) * _sigmoid(z)
        g_scr[rows, :] = _dot(_sigmoid(xg), g2_ref[...])

    ri = lax.broadcasted_iota(jnp.int32, (C, C), 0)
    ci = lax.broadcasted_iota(jnp.int32, (C, C), 1)
    cumsum_mat = (ri >= ci).astype(BF16)
    rq = lax.broadcasted_iota(jnp.int32, (C, 2 * C), 0)
    cq = lax.broadcasted_iota(jnp.int32, (C, 2 * C), 1)
    strict_l = jnp.logical_and(cq < C, rq > cq).astype(F32)
    strict_r = jnp.logical_and(cq >= C, rq > cq - C).astype(F32)
    incl = (rq >= cq % C).astype(F32)
    incl_l = jnp.logical_and(cq < C, rq >= cq).astype(F32)
    incl_r = jnp.logical_and(cq >= C, rq >= cq - C).astype(F32)
    r2 = lax.broadcasted_iota(jnp.int32, (2 * C, 2 * C), 0)
    c2 = lax.broadcasted_iota(jnp.int32, (2 * C, 2 * C), 1)
    bd_mask = ((r2 // C) == (c2 // C)).astype(F32)
    lo = lax.broadcasted_iota(jnp.int32, (1, LANES), 1) < RWKV_HD

    def stack(z):
        return jnp.concatenate([jnp.where(lo, z, 0.0), jnp.where(lo, 0.0, z)], axis=0)

    def stack_other(z):
        return jnp.concatenate([jnp.where(lo, 0.0, z), jnp.where(lo, z, 0.0)], axis=0)

    def prepare(t, carry):
        units = []
        for n in range(UNROLL):
            i = t * UNROLL + n
            rows = pl.ds(pl.multiple_of(i * C, C), C)
            lw = lw_scr[rows, :]
            lw_hi = lw.astype(BF16)
            lw_lo = (lw - lw_hi.astype(F32)).astype(BF16)
            cum = (jnp.dot(cumsum_mat, lw_hi, preferred_element_type=F32)
                   + jnp.dot(cumsum_mat, lw_lo, preferred_element_type=F32))
            cum_end = cum[C - 1:C, :]
            inv = jnp.exp(-cum)
            dec_rest = jnp.exp(cum_end - cum)
            dec_end = jnp.exp(cum_end)
            kc = k_scr[rows, :]
            bc = b_scr[rows, :]
            r_t = r_scr[rows, :] * jnp.exp(cum)
            a_t = am_scr[rows, :] * jnp.exp(cum - lw)
            b_t = bc * inv
            k_t = kc * inv
            b_e = bc * dec_rest
            k_e = kc * dec_rest
            vc = v_scr[rows, :]
            for s in range(PAIRS):
                sl = slice(s * LANES, (s + 1) * LANES)
                bkt_scr[i, s] = jnp.concatenate([b_e[:, sl], k_e[:, sl]], axis=0).T.astype(BF16)
                vb_scr[i, s] = vc[:, sl].astype(BF16)
                dcol_scr[i, s] = jnp.broadcast_to(dec_end[:, sl], (LANES, LANES)).T
                units.append((i, s, a_t[:, sl], r_t[:, sl], b_t[:, sl], k_t[:, sl], vc[:, sl]))
        N = range(len(units))
        lhs = [jnp.concatenate([u[2], u[3]], axis=0).astype(BF16) for u in units]
        nb = [_dot(lhs[n], stack(units[n][4]), _NT) for n in N]
        nk = [_dot(lhs[n], stack(units[n][5]), _NT) for n in N]
        m = [jnp.concatenate([nb[n][0:C] * strict_l, nb[n][0:C] * strict_r], axis=0) for n in N]
        mk = [jnp.concatenate([nk[n][0:C] * strict_l, nk[n][0:C] * strict_r], axis=0) for n in N]
        vs = [stack(units[n][6]).astype(BF16) for n in N]
        vsw = [stack_other(pltpu.roll(units[n][6], RWKV_HD, 1)).astype(BF16) for n in N]
        y = [stack(units[n][2]) + _dot(mk[n], vsw[n]) for n in N]
        for it in range(6):
            if it < 5:
                my = [_dot(m[n], jnp.concatenate([m[n], y[n]], axis=1)) for n in N]
                y = [y[n] + my[n][:, LANES:] for n in N]
                m = [my[n][:, :LANES] for n in N]
            else:
                my = [_dot(m[n], y[n]) for n in N]
                y = [y[n] + my[n] for n in N]
        arb = [jnp.concatenate([nb[n][C:2 * C] * incl_l, nb[n][C:2 * C] * incl_r], axis=0) for n in N]
        ay = [_dot(arb[n], y[n]) for n in N]
        akv = [_dot(nk[n][C:2 * C] * incl, vs[n]) for n in N]
        for n in N:
            i, s = units[n][0], units[n][1]
            wf = jnp.where(lo, y[n][0:C], y[n][C:2 * C])
            rw = units[n][3] + jnp.where(lo, ay[n][0:C], ay[n][C:2 * C])
            swapped = jnp.concatenate([jnp.where(lo, y[n][C:2 * C], y[n][0:C]),
                                       jnp.where(lo, ay[n][C:2 * C], ay[n][0:C])], axis=0)
            uo = pltpu.roll(swapped, RWKV_HD, 1)
            wr_scr[i, s] = jnp.concatenate([wf, rw], axis=0).astype(BF16)
            uo_scr[i, s] = jnp.concatenate([uo[0:C], uo[C:2 * C] + akv[n]], axis=0)
        return carry

    lax.fori_loop(0, G * nc // UNROLL, prepare, 0)

    def advance(c, carry):
        units = [(g, s, g * nc + c) for g in range(G) for s in range(PAIRS)]
        hs = [st_scr[g, s] for g, s, _ in units]
        ys = [jnp.dot(wr_scr[i, s], h.astype(BF16), preferred_element_type=F32) + uo_scr[i, s]
              for (g, s, i), h in zip(units, hs)]
        uvb = [jnp.concatenate([y[0:C].astype(BF16), vb_scr[i, s]], axis=0) for (g, s, i), y in zip(units, ys)]
        upd = [jnp.dot(bkt_scr[i, s], z, preferred_element_type=F32) for (g, s, i), z in zip(units, uvb)]
        for (g, s, i), h, y, up in zip(units, hs, ys, upd):
            st_scr[g, s] = h * dcol_scr[i, s] + up * bd_mask
            rows = pl.ds(pl.multiple_of(i * C, C), C)
            o_scr[rows, s * LANES:(s + 1) * LANES] = y[C:2 * C]
        return carry

    lax.fori_loop(0, nc, advance, 0)

    for s in range(PAIRS):
        sl = slice(s * LANES, (s + 1) * LANES)
        o = o_scr[:, sl]
        d = o - _seg_sum(o, e) * (1.0 / RWKV_HD)
        var = _seg_sum(d * d, e) * (1.0 / RWKV_HD)
        y = d * lax.rsqrt(var + GN_EPS) * lng_ref[:, sl] + lnb_ref[:, sl]
        bonus = _seg_sum(r_scr[:, sl] * k_scr[:, sl] * rk_ref[:, sl], e) * v_scr[:, sl]
        o_ref[:, :, sl] = ((y + bonus) * g_scr[:, sl]).astype(BF16).reshape(G, L, LANES)

    @pl.when(j == pl.num_programs(1) - 1)
    def _():
        for g in range(G):
            for s in range(PAIRS):
                sp = st_scr[g, s].T
                sout_ref[g, 2 * s] = sp[0:RWKV_HD, 0:RWKV_HD]
                sout_ref[g, 2 * s + 1] = sp[RWKV_HD:, RWKV_HD:]


def _rwkv(p, shift_in, s0, lw, e, G, L):
    B, T, _ = p.shape
    R = G * L
    vec = lambda c: _const_spec((1, c))
    buf = lambda: pltpu.VMEM((R, RWKV_W), F32)
    per_chunk = lambda rows, dt: pltpu.VMEM((R // CHUNK, PAIRS, rows, LANES), dt)
    state = pl.BlockSpec((G, RWKV_HEADS, RWKV_HD, RWKV_HD), lambda i, j: (i, 0, 0, 0))
    return pl.pallas_call(
        _rwkv_kernel,
        grid=(B // G, T // L),
        in_specs=[pl.BlockSpec((G, L, RWKV_COLS), lambda i, j: (i, j, 0)),
                  pl.BlockSpec((G, 1, RWKV_COLS), lambda i, j: (i, 0, 0)),
                  state,
                  vec(RWKV_COLS), vec(RWKV_W), _const_spec((LANES, RWKV_W)), vec(RWKV_W),
                  _const_spec((LANES, RWKV_W)), _const_spec((LANES, RWKV_W)), vec(RWKV_W), vec(RWKV_W),
                  vec(RWKV_W), vec(RWKV_W), vec(RWKV_W), _const_spec((LANES, LANES))],
        out_specs=[pl.BlockSpec((G, L, RWKV_W), lambda i, j: (i, j, 0)),
                   state],
        out_shape=[jax.ShapeDtypeStruct((B, T, RWKV_W), BF16),
                   jax.ShapeDtypeStruct((B, RWKV_HEADS, RWKV_HD, RWKV_HD), F32)],
        scratch_shapes=[pltpu.VMEM((G, 1, RWKV_COLS), F32), pltpu.VMEM((G, PAIRS, LANES, LANES), F32)]
                       + [buf() for _ in range(8)]
                       + [per_chunk(LANES, BF16), per_chunk(LANES, F32), per_chunk(LANES, BF16),
                          per_chunk(CHUNK, BF16), per_chunk(LANES, F32)],
        compiler_params=_params(),
        name="rwkv7",
    )(p, shift_in, s0, lw["mu"], lw["w0"], lw["w2p"], lw["a0"], lw["a2p"], lw["g2"], lw["kk"],
      lw["ka"], lw["rk"], lw["ln_g"], lw["ln_b"], e)


def _swa_kernel(sink_ref, q_ref, k_ref, v_ref, hk_ref, hv_ref, o_ref, kd, vd, *, has_cache):
    G, L, _ = q_ref.shape
    C = CHUNK
    nc = L // C
    KB = WINDOW + C
    GROUP = 4
    j = pl.program_id(1)
    lo = lax.broadcasted_iota(jnp.int32, (1, LANES), 1) < SWA_HD
    qi = lax.broadcasted_iota(jnp.int32, (C, KB), 0)
    kj = lax.broadcasted_iota(jnp.int32, (C, KB), 1)
    dist = jnp.abs(WINDOW + qi - kj).astype(F32)
    kj4 = lax.broadcasted_iota(jnp.int32, (GROUP * C, KB), 1)
    ones = jnp.ones((WINDOW + L, LANES), F32)
    neg_pad = jnp.full((GROUP * C, 2 * LANES - KB), -jnp.inf, F32)
    bias = [jnp.concatenate([(2.0 ** -(GROUP * kv + h + 1)) * dist for h in range(GROUP)], axis=0)
            for kv in range(2)]
    sink = [jnp.concatenate([jnp.full((C, LANES), sink_ref[GROUP * kv + h], F32) for h in range(GROUP)], axis=0)
            for kv in range(2)]
    for g in range(G):
        kc = jnp.concatenate([hk_ref[g], k_ref[g]], axis=0)
        vc = jnp.concatenate([hv_ref[g], v_ref[g]], axis=0)
        ks = pltpu.roll(kc, SWA_HD, 1)
        vs = pltpu.roll(vc, SWA_HD, 1)
        kd[g, 0] = jnp.where(lo, kc, ks).astype(BF16)
        kd[g, 1] = jnp.where(lo, ks, kc).astype(BF16)
        vd[g, 0] = jnp.concatenate([jnp.where(lo, vc, vs), ones], axis=1).astype(BF16)
        vd[g, 1] = jnp.concatenate([jnp.where(lo, vs, vc), ones], axis=1).astype(BF16)

    def chunks(t, carry, mask_halo):
        units = []
        for n in range(UNROLL):
            i = t * UNROLL + n
            for kv in range(2):
                units.append((i // nc, pl.multiple_of((i % nc) * C, C), kv))
        scs = []
        for g, off, kv in units:
            q0 = q_ref[g, pl.ds(off, C), (2 * kv) * LANES:(2 * kv + 1) * LANES]
            q1 = q_ref[g, pl.ds(off, C), (2 * kv + 1) * LANES:(2 * kv + 2) * LANES]
            zero = jnp.zeros_like(q0)
            lhs = jnp.concatenate([jnp.where(lo, q0, zero), jnp.where(lo, zero, q0),
                                   jnp.where(lo, q1, zero), jnp.where(lo, zero, q1)], axis=0)
            sc = lax.dot_general(lhs, kd[g, kv, pl.ds(off, KB), :], ((_NT), ((), ())),
                                 preferred_element_type=F32) - bias[kv]
            if mask_halo:
                sc = jnp.where(kj4 + off < WINDOW, -jnp.inf, sc)
            scs.append(sc)
        exs, sink_terms = [], []
        for (g, off, kv), sc in zip(units, scs):
            folded = jnp.maximum(sc[:, :LANES], jnp.concatenate([sc[:, LANES:], neg_pad], axis=1))
            mx = jnp.maximum(jnp.broadcast_to(jnp.max(folded, axis=-1, keepdims=True), (GROUP * C, LANES)),
                             sink[kv])
            ex = jnp.concatenate([jnp.exp(sc[:, :LANES] - mx), jnp.exp(sc[:, LANES:] - mx[:, :KB - LANES])],
                                 axis=1)
            exs.append(ex.astype(BF16))
            sink_terms.append(jnp.exp(sink[kv] - mx))
        for (g, off, kv), ex, st in zip(units, exs, sink_terms):
            pvd = jnp.dot(ex, vd[g, kv, pl.ds(off, KB), :], preferred_element_type=F32)
            pv = pvd[:, :LANES] / (pvd[:, LANES:] + st)
            pv = pv.astype(BF16)
            o_ref[g, pl.ds(off, C), (2 * kv) * LANES:(2 * kv + 1) * LANES] = jnp.where(lo, pv[0:C], pv[C:2 * C])
            o_ref[g, pl.ds(off, C), (2 * kv + 1) * LANES:(2 * kv + 2) * LANES] = jnp.where(
                lo, pv[2 * C:3 * C], pv[3 * C:4 * C])
        return carry

    steps = G * nc // UNROLL
    if has_cache:
        lax.fori_loop(0, steps, functools.partial(chunks, mask_halo=False), 0)
    else:
        @pl.when(j == 0)
        def _():
            lax.fori_loop(0, steps, functools.partial(chunks, mask_halo=True), 0)

        @pl.when(j != 0)
        def _():
            lax.fori_loop(0, steps, functools.partial(chunks, mask_halo=False), 0)


def _swa(q, k, v, halo_k, halo_v, sink, G, L, has_cache):
    B, T, _ = q.shape
    if has_cache:
        halo = pl.BlockSpec((G, WINDOW, LANES), lambda i, j: (i, 0, 0))
    else:
        per = L // WINDOW
        halo = pl.BlockSpec((G, WINDOW, LANES), lambda i, j: (i, jnp.maximum(j * per - 1, 0), 0))
    tile = lambda c: pl.BlockSpec((G, L, c), lambda i, j: (i, j, 0))
    cat = lambda cols: pltpu.VMEM((G, 2, WINDOW + L, cols), BF16)
    return pl.pallas_call(
        functools.partial(_swa_kernel, has_cache=has_cache),
        grid=(B // G, T // L),
        in_specs=[pl.BlockSpec(memory_space=pltpu.SMEM), tile(SWA_HEADS * SWA_HD), tile(LANES), tile(LANES),
                  halo, halo],
        out_specs=tile(SWA_HEADS * SWA_HD),
        out_shape=jax.ShapeDtypeStruct((B, T, SWA_HEADS * SWA_HD), BF16),
        scratch_shapes=[cat(LANES), cat(2 * LANES)],
        compiler_params=_params(),
        name="swa",
    )(sink, q, k, v, halo_k, halo_v)


def _mem_kv_kernel(m_ref, g_ref, w_ref, kng_ref, mk_ref, mv_ref):
    G, M, D = m_ref.shape
    x = m_ref[...].reshape(G * M, D)
    h = (x * lax.rsqrt(jnp.mean(x * x, axis=-1, keepdims=True) + RMS_EPS) * g_ref[...]).astype(BF16)
    for s in range(MEM_HEADS):
        z = jnp.dot(h, w_ref[:, s * MEM_HD:(s + 1) * MEM_HD], preferred_element_type=F32)
        z = z * lax.rsqrt(jnp.mean(z * z, axis=-1, keepdims=True) + RMS_EPS) * kng_ref[...]
        mk_ref[:, :, s * MEM_HD:(s + 1) * MEM_HD] = z.reshape(G, M, MEM_HD)
    mv_ref[...] = jnp.dot(h, w_ref[:, MEM_W:], preferred_element_type=F32).reshape(G, M, MEM_W)


def _mem_kv(mem, g, w_b, l, kng):
    B, M, D = mem.shape
    blk = lambda c: pl.BlockSpec((1, M, c), lambda i, j: (i, 0, 0))
    return pl.pallas_call(
        _mem_kv_kernel,
        grid=(B, 1),
        in_specs=[blk(D), _const_spec((1, D)), _layer_spec((D, 2 * MEM_W), l), _const_spec((1, MEM_HD))],
        out_specs=[blk(MEM_W), blk(MEM_W)],
        out_shape=[jax.ShapeDtypeStruct((B, M, MEM_W), F32)] * 2,
        compiler_params=_params(),
        name="mem_kv",
    )(mem, g, w_b, kng)


def _mem_att_kernel(q_ref, mk_ref, mv_ref, o_ref):
    G, L, _ = q_ref.shape
    for g in range(G):
        for s in range(MEM_HEADS):
            sl = slice(s * MEM_HD, (s + 1) * MEM_HD)
            sc = _dot(q_ref[g, :, sl], mk_ref[g, :, sl], _NT) * (MEM_HD ** -0.5)
            ex = jnp.exp(sc - jnp.max(sc, axis=-1, keepdims=True))
            pr = ex / jnp.sum(ex, axis=-1, keepdims=True)
            o_ref[g, :, sl] = _dot(pr, mv_ref[g, :, sl]).astype(BF16)


def _mem_att(qm, mk, mv, G, L):
    B, T, _ = qm.shape
    tile = pl.BlockSpec((G, L, MEM_W), lambda i, j: (i, j, 0))
    mem = pl.BlockSpec((G, MEM_TOKENS, MEM_W), lambda i, j: (i, 0, 0))
    return pl.pallas_call(
        _mem_att_kernel,
        grid=(B // G, T // L),
        in_specs=[tile, mem, mem],
        out_specs=tile,
        out_shape=jax.ShapeDtypeStruct((B, T, MEM_W), BF16),
        compiler_params=_params(),
        name="mem_att",
    )(qm, mk, mv)


def _merge_kernel(x_ref, oa_ref, ob_ref, om_ref, g1_ref, wg_ref, wb_ref, wo_ref, y_ref):
    G, L, D = x_ref.shape
    R = G * L
    x = x_ref[...].reshape(R, D)
    hb = (x * lax.rsqrt(jnp.mean(x * x, axis=-1, keepdims=True) + RMS_EPS) * g1_ref[...]).astype(BF16)
    mix = None
    for n, o_ref in enumerate((oa_ref, ob_ref, om_ref)):
        br = jnp.dot(o_ref[...].reshape(R, RWKV_W), wb_ref[n], preferred_element_type=F32)
        gate = _sigmoid(jnp.dot(hb, wg_ref[:, n * D:(n + 1) * D], preferred_element_type=F32))
        mix = gate * br if mix is None else mix + gate * br
    y_ref[...] = (x + jnp.dot(mix.astype(BF16), wo_ref[...], preferred_element_type=F32)).reshape(G, L, D)


def _merge(x, oa, ob, om, g1, w_in_b, wb_b, wo_b, l, G, L):
    B, T, D = x.shape
    tile = lambda c: pl.BlockSpec((G, L, c), lambda i, j: (i, j, 0))
    return pl.pallas_call(
        _merge_kernel,
        grid=(B // G, T // L),
        in_specs=[tile(D), tile(RWKV_W), tile(RWKV_W), tile(MEM_W), _const_spec((1, D)),
                  pl.BlockSpec((None, D, N_BRANCH * D), lambda *_: (l, 0, C_GT // (N_BRANCH * D)),
                               pipeline_mode=pl.Buffered(1)),
                  _layer_spec((N_BRANCH, RWKV_W, D), l), _layer_spec((D, D), l)],
        out_specs=tile(D),
        out_shape=jax.ShapeDtypeStruct((B, T, D), F32),
        compiler_params=_params(),
        name="merge",
    )(x, oa, ob, om, g1, w_in_b, wb_b, wo_b)


def _ffn_kernel(x_ref, cin_ref, g2_ref, wu_ref, cw_ref, cb_ref, wd_ref, y_ref, cout_ref, carry):
    G, L, D = x_ref.shape
    R = G * L
    j = pl.program_id(1)

    @pl.when(j == 0)
    def _():
        carry[...] = cin_ref[...]

    x = x_ref[...].reshape(R, D)
    hb = (x * lax.rsqrt(jnp.mean(x * x, axis=-1, keepdims=True) + RMS_EPS) * g2_ref[...]).astype(BF16)
    row = lax.broadcasted_iota(jnp.int32, (L, 1), 0)
    acc = x
    for c0, c1 in FF_BLOCKS:
        cs = slice(c0, c1)
        a_in = jnp.dot(hb, wu_ref[:, cs], preferred_element_type=F32)
        u = jnp.dot(hb, wu_ref[:, D_FF + c0:D_FF + c1], preferred_element_type=F32)
        convs = []
        for g in range(G):
            a = a_in[g * L:(g + 1) * L]
            prev = carry[g, :, cs]
            a1 = jnp.where(row == 0, prev[1:2], pltpu.roll(a, 1, 0))
            a2 = jnp.where(row == 0, prev[0:1], jnp.where(row == 1, prev[1:2], pltpu.roll(a, 2, 0)))
            carry[g, :, cs] = a[L - 2:L]
            convs.append(cb_ref[:, cs] + a2 * cw_ref[0:1, cs] + a1 * cw_ref[1:2, cs] + a * cw_ref[2:3, cs])
        c = convs[0] if G == 1 else jnp.concatenate(convs, axis=0)
        gelu = 0.5 * c * (1.0 + jnp.tanh(0.7978845608028654 * (c + 0.044715 * (c * c * c))))
        acc = acc + jnp.dot((gelu * u).astype(BF16), wd_ref[cs, :], preferred_element_type=F32)
    y_ref[...] = acc.reshape(G, L, D)

    @pl.when(j == pl.num_programs(1) - 1)
    def _():
        cout_ref[...] = carry[...]


def _ffn(x, conv_in, g2, wu_b, cw, cb, wd_b, l, G, L):
    B, T, D = x.shape
    tile = pl.BlockSpec((G, L, D), lambda i, j: (i, j, 0))
    st = pl.BlockSpec((G, CONV_W - 1, D_FF), lambda i, j: (i, 0, 0))
    return pl.pallas_call(
        _ffn_kernel,
        grid=(B // G, T // L),
        in_specs=[tile, st, _const_spec((1, D)), _layer_spec((D, 2 * D_FF), l), _const_spec((CONV_W, D_FF)),
                  _const_spec((1, D_FF)), _layer_spec((D_FF, D), l)],
        out_specs=[tile, st],
        out_shape=[jax.ShapeDtypeStruct((B, T, D), F32), jax.ShapeDtypeStruct((B, CONV_W - 1, D_FF), F32)],
        scratch_shapes=[pltpu.VMEM((G, CONV_W - 1, D_FF), F32)],
        compiler_params=_params(),
        name="conv_ffn",
    )(x, conv_in, g2, wu_b, cw, cb, wd_b)


def _layer(x, lw, e, mk, mv, shift_in, s0, conv_in, halo_k, halo_v, tiles):
    has_cache = halo_k is not None
    p, q, k, v, qm = _in_proj(x, lw["norm1_g"], lw["w_in"], lw["layer"], lw["qn_g"], lw["kn_g"], lw["mqn_g"], e,
                              *tiles["dense"])
    oa, s_new = _rwkv(p, shift_in, s0, lw, e, *tiles["rwkv"])
    if has_cache:
        ob = _swa(q, k, v, halo_k, halo_v, lw["sink"], *tiles["att"], True)
    else:
        ob = _swa(q, k, v, k, v, lw["sink"], *tiles["att"], False)
    om = _mem_att(qm, mk, mv, *tiles["att"])
    x = _merge(x, oa, ob, om, lw["norm1_g"], lw["w_in"], lw["w_branch"], lw["w_out"], lw["layer"], *tiles["dense"])
    x, conv_new = _ffn(x, conv_in, lw["norm2_g"], lw["w_up"], lw["conv_w"], lw["conv_b"], lw["w_down"], lw["layer"],
                       *tiles["dense"])
    return x, (k, v, s_new, p[:, -1:, :], conv_new)


def kernel(x_prompt, x_sample, cache_swa_k, cache_swa_v, cache_mem_k, cache_mem_v, state_rwkv, state_shift, state_conv, mem_prompt, norm1_g, w_in, rwkv_mu, rwkv_w0, rwkv_w2, rwkv_a0, rwkv_a2, rwkv_g2, rwkv_kk, rwkv_ka, rwkv_rk, rwkv_ln_g, rwkv_ln_b, swa_qn_g, swa_kn_g, swa_sink, mem_norm_g, w_mem_kv, mem_qn_g, mem_kn_g, w_branch, w_out, norm2_g, w_up, conv_w, conv_b, w_down):
    Bp, Tp, _ = x_prompt.shape
    Bs, Ts, _ = x_sample.shape
    dt = x_prompt.dtype
    half = jnp.arange(LANES) // RWKV_HD
    e = (half[:, None] == half[None, :]).astype(BF16)
    row = lambda a: a.reshape(1, -1)
    zpad = jnp.zeros((LANES - 64, RWKV_W), dt)

    w_in_b, w_branch_b, w_out_b, w_up_b, w_down_b, w_mem_kv_b = (
        w.astype(BF16) for w in (w_in, w_branch, w_out, w_up, w_down, w_mem_kv))
    yp, ys = x_prompt, x_sample
    outs_p = [[] for _ in range(7)]
    outs_s = [[] for _ in range(5)]
    prompt_tiles = {"dense": (1, 512), "rwkv": (Bp, 128), "att": (1, 256)}
    sample_tiles = {"dense": (8, Ts), "rwkv": (8, Ts), "att": (8, Ts)}
    for l in range(DEPTH):
        lw = {
            "layer": l, "norm1_g": row(norm1_g[l]), "w_in": w_in_b,
            "qn_g": row(jnp.tile(swa_qn_g[l], 2)) * (SWA_HD ** -0.5), "kn_g": row(jnp.tile(swa_kn_g[l], 2)),
            "mqn_g": row(mem_qn_g[l]),
            "mu": row(rwkv_mu[l]), "w0": row(rwkv_w0[l]),
            "w2p": jnp.concatenate([rwkv_w2[l], zpad], axis=0),
            "a0": row(rwkv_a0[l]),
            "a2p": jnp.concatenate([zpad, rwkv_a2[l]], axis=0),
            "g2": rwkv_g2[l], "kk": row(rwkv_kk[l]), "ka": row(rwkv_ka[l]), "rk": row(rwkv_rk[l]),
            "ln_g": row(rwkv_ln_g[l]), "ln_b": row(rwkv_ln_b[l]),
            "sink": swa_sink[l],
            "w_branch": w_branch_b, "w_out": w_out_b,
            "norm2_g": row(norm2_g[l]), "w_up": w_up_b, "conv_w": conv_w[l],
            "conv_b": row(conv_b[l]), "w_down": w_down_b,
        }
        mk, mv = _mem_kv(mem_prompt, row(mem_norm_g[l]), w_mem_kv_b, l, row(mem_kn_g[l]))
        yp, (k, v, s_new, sh_new, cv_new) = _layer(
            yp, lw, e, mk, mv,
            jnp.zeros((Bp, 1, RWKV_COLS), dt),
            jnp.zeros((Bp, RWKV_HEADS, RWKV_HD, RWKV_HD), dt),
            jnp.zeros((Bp, CONV_W - 1, D_FF), dt), None, None, prompt_tiles)
        kv_shape = (Bp, WINDOW, 2, SWA_HD)
        for lst, val in zip(outs_p, (k[:, -WINDOW:].reshape(kv_shape), v[:, -WINDOW:].reshape(kv_shape),
                                     mk.reshape(Bp, MEM_TOKENS, MEM_HEADS, MEM_HD),
                                     mv.reshape(Bp, MEM_TOKENS, MEM_HEADS, MEM_HD), s_new, sh_new, cv_new)):
            lst.append(val)
        ck = cache_swa_k[l].reshape(Bs, WINDOW, LANES)
        cv = cache_swa_v[l].reshape(Bs, WINDOW, LANES)
        ys, (k, v, s_new, sh_new, cv_new) = _layer(
            ys, lw, e, cache_mem_k[l].reshape(Bs, MEM_TOKENS, MEM_W), cache_mem_v[l].reshape(Bs, MEM_TOKENS, MEM_W),
            state_shift[l], state_rwkv[l], state_conv[l], ck, cv, sample_tiles)
        kv_shape = (Bs, WINDOW, 2, SWA_HD)
        kf = jnp.concatenate([ck, k], axis=1)[:, -WINDOW:].reshape(kv_shape)
        vf = jnp.concatenate([cv, v], axis=1)[:, -WINDOW:].reshape(kv_shape)
        for lst, val in zip(outs_s, (kf, vf, s_new, sh_new, cv_new)):
            lst.append(val)
    return (yp, ys) + tuple(jnp.stack(o) for o in outs_p) + tuple(jnp.stack(o) for o in outs_s)
```

```python
import functools
import math

import jax
import jax.numpy as jnp
from jax import lax
from jax.experimental import pallas as pl
from jax.experimental.pallas import tpu as pltpu

F32 = jnp.float32
BF16 = jnp.bfloat16
HIGHEST = lax.Precision.HIGHEST

D_MODEL = 1024
DEPTH = 2
CHUNK = 64
RWKV_HEADS = 8
RWKV_HD = 64
RWKV_W = 512
RWKV_COLS = 1792
GN_EPS = 64e-5
SWA_HEADS = 8
SWA_HD = 64
WINDOW = 128
MEM_TOKENS = 256
MEM_HEADS = 4
MEM_HD = 128
MEM_W = 512
N_BRANCH = 3
D_FF = 2816
CONV_W = 3
RMS_EPS = 1e-6

LANES = 128
PAIRS = RWKV_W // LANES
VMEM_LIMIT = 56 * 1024 * 1024
MXU_DIM = 256
FF_BLOCKS = ((0, 6 * MXU_DIM), (6 * MXU_DIM, D_FF))
UNROLL = 2

C_Q = RWKV_COLS
C_K = C_Q + SWA_HEADS * SWA_HD
C_V = C_K + LANES
C_QM = C_V + LANES
C_GT = C_QM + MEM_W
IN_COLS = C_GT + N_BRANCH * D_MODEL


def _dot(a, b, dims=((1,), (0,)), exact=False):
    if exact:
        return lax.dot_general(a, b, (dims, ((), ())), precision=HIGHEST, preferred_element_type=F32)
    return lax.dot_general(a.astype(BF16), b.astype(BF16), (dims, ((), ())), preferred_element_type=F32)


_NT = ((1,), (1,))
_TN = ((0,), (0,))


def _seg_sum(x, e):
    return jnp.dot(x.astype(BF16), e, preferred_element_type=F32)


def _sigmoid(x):
    return 1.0 / (1.0 + jnp.exp(-x))


def _const_spec(shape):
    n = len(shape)
    return pl.BlockSpec(shape, lambda *_: (0,) * n, pipeline_mode=pl.Buffered(1))


def _layer_spec(shape, l):
    n = len(shape)
    return pl.BlockSpec((None,) + tuple(shape), lambda *_: (l,) + (0,) * n, pipeline_mode=pl.Buffered(1))


def _params():
    return pltpu.CompilerParams(dimension_semantics=("arbitrary", "arbitrary"), vmem_limit_bytes=VMEM_LIMIT)


def _in_kernel(x_ref, g1_ref, w_ref, qng_ref, kng_ref, mqg_ref, e_ref,
               p_ref, q_ref, k_ref, v_ref, qm_ref):
    G, L, D = x_ref.shape
    R = G * L
    x = x_ref[...].reshape(R, D)
    h = x * lax.rsqrt(jnp.mean(x * x, axis=-1, keepdims=True) + RMS_EPS) * g1_ref[...]
    hb = h.astype(BF16)
    e = e_ref[...]

    def proj(c0, c1):
        return jnp.dot(hb, w_ref[:, c0:c1], preferred_element_type=F32)

    def head_rms(z, gain):
        return z * lax.rsqrt(_seg_sum(z * z, e) * (1.0 / SWA_HD) + RMS_EPS) * gain

    p_ref[...] = proj(0, RWKV_COLS).reshape(G, L, RWKV_COLS)
    zq = proj(C_Q, C_K)
    for s in range(SWA_HEADS * SWA_HD // LANES):
        sl = slice(s * LANES, (s + 1) * LANES)
        q_ref[:, :, sl] = head_rms(zq[:, sl], qng_ref[...]).astype(BF16).reshape(G, L, LANES)
    zkv = proj(C_K, C_QM)
    k_ref[...] = head_rms(zkv[:, :LANES], kng_ref[...]).reshape(G, L, LANES)
    v_ref[...] = zkv[:, LANES:].reshape(G, L, LANES)
    zqm = proj(C_QM, C_GT)
    for s in range(MEM_HEADS):
        sl = slice(s * MEM_HD, (s + 1) * MEM_HD)
        z = zqm[:, sl]
        z = z * lax.rsqrt(jnp.mean(z * z, axis=-1, keepdims=True) + RMS_EPS) * mqg_ref[...]
        qm_ref[:, :, sl] = z.astype(BF16).reshape(G, L, MEM_HD)


def _in_proj(x, g1, w_in_b, l, qng, kng, mqg, e, G, L):
    B, T, D = x.shape
    tile = lambda c: pl.BlockSpec((G, L, c), lambda i, j: (i, j, 0))
    outs = ((RWKV_COLS, F32), (SWA_HEADS * SWA_HD, BF16), (LANES, F32), (LANES, F32), (MEM_W, BF16))
    return pl.pallas_call(
        _in_kernel,
        grid=(B // G, T // L),
        in_specs=[tile(D), _const_spec((1, D)), _layer_spec((D, C_GT), l), _const_spec((1, LANES)),
                  _const_spec((1, LANES)), _const_spec((1, MEM_HD)), _const_spec((LANES, LANES))],
        out_specs=[tile(c) for c, _ in outs],
        out_shape=[jax.ShapeDtypeStruct((B, T, c), dt) for c, dt in outs],
        compiler_params=_params(),
        name="in_proj",
    )(x, g1, w_in_b, qng, kng, mqg, e)


def _rwkv_kernel(p_ref, sh_ref, s0_ref, mu_ref, w0_ref, w2_ref, a0_ref, a2_ref, g2_ref, kk_ref, ka_ref,
                 rk_ref, lng_ref, lnb_ref, e_ref,
                 o_ref, sout_ref,
                 prev_scr, st_scr, r_scr, k_scr, v_scr, am_scr, b_scr, lw_scr, g_scr, o_scr,
                 wr_scr, uo_scr, bkt_scr, vb_scr, dcol_scr):
    G, L, _ = p_ref.shape
    C = CHUNK
    nc = L // C
    j = pl.program_id(1)
    e = e_ref[...]

    @pl.when(j == 0)
    def _():
        zero = jnp.zeros((RWKV_HD, RWKV_HD), F32)
        for g in range(G):
            for s in range(PAIRS):
                top = jnp.concatenate([s0_ref[g, 2 * s], zero], axis=1)
                bot = jnp.concatenate([zero, s0_ref[g, 2 * s + 1]], axis=1)
                st_scr[g, s] = jnp.concatenate([top, bot], axis=0).T
        prev_scr[...] = sh_ref[...]

    first_row = lax.broadcasted_iota(jnp.int32, (L, 1), 0) == 0
    for g in range(G):
        p = p_ref[g]
        shifted = jnp.where(first_row, prev_scr[g], pltpu.roll(p, 1, 0))
        pm = p + (shifted - p) * mu_ref[...]
        prev_scr[g] = p[L - 1:L, :]
        rows = slice(g * L, (g + 1) * L)
        r = pm[:, 0:RWKV_W]
        k = pm[:, RWKV_W:2 * RWKV_W]
        v = pm[:, 2 * RWKV_W:3 * RWKV_W]
        xwa = pm[:, 3 * RWKV_W:3 * RWKV_W + LANES]
        xg = pm[:, 3 * RWKV_W + LANES:RWKV_COLS]
        z = w0_ref[...] + _dot(jnp.tanh(xwa), w2_ref[...])
        a = _sigmoid(a0_ref[...] + _dot(xwa, a2_ref[...]))
        kkv = k * kk_ref[...]
        for s in range(PAIRS):
            sl = slice(s * LANES, (s + 1) * LANES)
            kks = kkv[:, sl]
            kkn = kks * lax.rsqrt(jnp.maximum(_seg_sum(kks * kks, e), 1e-24))
            am_scr[rows, sl] = -kkn
            b_scr[rows, sl] = kkn * a[:, sl]
        r_scr[rows, :] = r
        k_scr[rows, :] = k * (1.0 + (a - 1.0) * ka_ref[...])
        v_scr[rows, :] = v
        lw_scr[rows, :] = (-math.exp(-0.5)) * _sigmoid(z)
        g_scr[rows, :] = _dot(_sigmoid(xg), g2_ref[...])

    ri = lax.broadcasted_iota(jnp.int32, (C, C), 0)
    ci = lax.broadcasted_iota(jnp.int32, (C, C), 1)
    cumsum_mat = (ri >= ci).astype(BF16)
    rq = lax.broadcasted_iota(jnp.int32, (C, 2 * C), 0)
    cq = lax.broadcasted_iota(jnp.int32, (C, 2 * C), 1)
    strict_l = jnp.logical_and(cq < C, rq > cq).astype(F32)
    strict_r = jnp.logical_and(cq >= C, rq > cq - C).astype(F32)
    incl = (rq >= cq % C).astype(F32)
    incl_l = jnp.logical_and(cq < C, rq >= cq).astype(F32)
    incl_r = jnp.logical_and(cq >= C, rq >= cq - C).astype(F32)
    r2 = lax.broadcasted_iota(jnp.int32, (2 * C, 2 * C), 0)
    c2 = lax.broadcasted_iota(jnp.int32, (2 * C, 2 * C), 1)
    bd_mask = ((r2 // C) == (c2 // C)).astype(F32)
    lo = lax.broadcasted_iota(jnp.int32, (1, LANES), 1) < RWKV_HD

    def stack(z):
        return jnp.concatenate([jnp.where(lo, z, 0.0), jnp.where(lo, 0.0, z)], axis=0)

    def stack_other(z):
        return jnp.concatenate([jnp.where(lo, 0.0, z), jnp.where(lo, z, 0.0)], axis=0)

    def prepare(t, carry):
        units = []
        for n in range(UNROLL):
            i = t * UNROLL + n
            rows = pl.ds(pl.multiple_of(i * C, C), C)
            lw = lw_scr[rows, :]
            lw_hi = lw.astype(BF16)
            lw_lo = (lw - lw_hi.astype(F32)).astype(BF16)
            cum = (jnp.dot(cumsum_mat, lw_hi, preferred_element_type=F32)
                   + jnp.dot(cumsum_mat, lw_lo, preferred_element_type=F32))
            cum_end = cum[C - 1:C, :]
            inv = jnp.exp(-cum)
            dec_rest = jnp.exp(cum_end - cum)
            dec_end = jnp.exp(cum_end)
            kc = k_scr[rows, :]
            bc = b_scr[rows, :]
            r_t = r_scr[rows, :] * jnp.exp(cum)
            a_t = am_scr[rows, :] * jnp.exp(cum - lw)
            b_t = bc * inv
            k_t = kc * inv
            b_e = bc * dec_rest
            k_e = kc * dec_rest
            vc = v_scr[rows, :]
            for s in range(PAIRS):
                sl = slice(s * LANES, (s + 1) * LANES)
                bkt_scr[i, s] = jnp.concatenate([b_e[:, sl], k_e[:, sl]], axis=0).T.astype(BF16)
                vb_scr[i, s] = vc[:, sl].astype(BF16)
                dcol_scr[i, s] = jnp.broadcast_to(dec_end[:, sl], (LANES, LANES)).T
                units.append((i, s, a_t[:, sl], r_t[:, sl], b_t[:, sl], k_t[:, sl], vc[:, sl]))
        N = range(len(units))
        lhs = [jnp.concatenate([u[2], u[3]], axis=0).astype(BF16) for u in units]
        nb = [_dot(lhs[n], stack(units[n][4]), _NT) for n in N]
        nk = [_dot(lhs[n], stack(units[n][5]), _NT) for n in N]
        m = [jnp.concatenate([nb[n][0:C] * strict_l, nb[n][0:C] * strict_r], axis=0) for n in N]
        mk = [jnp.concatenate([nk[n][0:C] * strict_l, nk[n][0:C] * strict_r], axis=0) for n in N]
        vs = [stack(units[n][6]).astype(BF16) for n in N]
        vsw = [stack_other(pltpu.roll(units[n][6], RWKV_HD, 1)).astype(BF16) for n in N]
        y = [stack(units[n][2]) + _dot(mk[n], vsw[n]) for n in N]
        for it in range(6):
            if it < 5:
                my = [_dot(m[n], jnp.concatenate([m[n], y[n]], axis=1)) for n in N]
                y = [y[n] + my[n][:, LANES:] for n in N]
                m = [my[n][:, :LANES] for n in N]
            else:
                my = [_dot(m[n], y[n]) for n in N]
                y = [y[n] + my[n] for n in N]
        arb = [jnp.concatenate([nb[n][C:2 * C] * incl_l, nb[n][C:2 * C] * incl_r], axis=0) for n in N]
        ay = [_dot(arb[n], y[n]) for n in N]
        akv = [_dot(nk[n][C:2 * C] * incl, vs[n]) for n in N]
        for n in N:
            i, s = units[n][0], units[n][1]
            wf = jnp.where(lo, y[n][0:C], y[n][C:2 * C])
            rw = units[n][3] + jnp.where(lo, ay[n][0:C], ay[n][C:2 * C])
            swapped = jnp.concatenate([jnp.where(lo, y[n][C:2 * C], y[n][0:C]),
                                       jnp.where(lo, ay[n][C:2 * C], ay[n][0:C])], axis=0)
            uo = pltpu.roll(swapped, RWKV_HD, 1)
            wr_scr[i, s] = jnp.concatenate([wf, rw], axis=0).astype(BF16)
            uo_scr[i, s] = jnp.concatenate([uo[0:C], uo[C:2 * C] + akv[n]], axis=0)
        return carry

    lax.fori_loop(0, G * nc // UNROLL, prepare, 0)

    def advance(c, carry):
        units = [(g, s, g * nc + c) for g in range(G) for s in range(PAIRS)]
        hs = [st_scr[g, s] for g, s, _ in units]
        ys = [jnp.dot(wr_scr[i, s], h.astype(BF16), preferred_element_type=F32) + uo_scr[i, s]
              for (g, s, i), h in zip(units, hs)]
        uvb = [jnp.concatenate([y[0:C].astype(BF16), vb_scr[i, s]], axis=0) for (g, s, i), y in zip(units, ys)]
        upd = [jnp.dot(bkt_scr[i, s], z, preferred_element_type=F32) for (g, s, i), z in zip(units, uvb)]
        for (g, s, i), h, y, up in zip(units, hs, ys, upd):
            st_scr[g, s] = h * dcol_scr[i, s] + up * bd_mask
            rows = pl.ds(pl.multiple_of(i * C, C), C)
            o_scr[rows, s * LANES:(s + 1) * LANES] = y[C:2 * C]
        return carry

    lax.fori_loop(0, nc, advance, 0)

    for s in range(PAIRS):
        sl = slice(s * LANES, (s + 1) * LANES)
        o = o_scr[:, sl]
        d = o - _seg_sum(o, e) * (1.0 / RWKV_HD)
        var = _seg_sum(d * d, e) * (1.0 / RWKV_HD)
        y = d * lax.rsqrt(var + GN_EPS) * lng_ref[:, sl] + lnb_ref[:, sl]
        bonus = _seg_sum(r_scr[:, sl] * k_scr[:, sl] * rk_ref[:, sl], e) * v_scr[:, sl]
        o_ref[:, :, sl] = ((y + bonus) * g_scr[:, sl]).astype(BF16).reshape(G, L, LANES)

    @pl.when(j == pl.num_programs(1) - 1)
    def _():
        for g in range(G):
            for s in range(PAIRS):
                sp = st_scr[g, s].T
                sout_ref[g, 2 * s] = sp[0:RWKV_HD, 0:RWKV_HD]
                sout_ref[g, 2 * s + 1] = sp[RWKV_HD:, RWKV_HD:]


def _rwkv(p, shift_in, s0, lw, e, G, L):
    B, T, _ = p.shape
    R = G * L
    vec = lambda c: _const_spec((1, c))
    buf = lambda: pltpu.VMEM((R, RWKV_W), F32)
    per_chunk = lambda rows, dt: pltpu.VMEM((R // CHUNK, PAIRS, rows, LANES), dt)
    state = pl.BlockSpec((G, RWKV_HEADS, RWKV_HD, RWKV_HD), lambda i, j: (i, 0, 0, 0))
    return pl.pallas_call(
        _rwkv_kernel,
        grid=(B // G, T // L),
        in_specs=[pl.BlockSpec((G, L, RWKV_COLS), lambda i, j: (i, j, 0)),
                  pl.BlockSpec((G, 1, RWKV_COLS), lambda i, j: (i, 0, 0)),
                  state,
                  vec(RWKV_COLS), vec(RWKV_W), _const_spec((LANES, RWKV_W)), vec(RWKV_W),
                  _const_spec((LANES, RWKV_W)), _const_spec((LANES, RWKV_W)), vec(RWKV_W), vec(RWKV_W),
                  vec(RWKV_W), vec(RWKV_W), vec(RWKV_W), _const_spec((LANES, LANES))],
        out_specs=[pl.BlockSpec((G, L, RWKV_W), lambda i, j: (i, j, 0)),
                   state],
        out_shape=[jax.ShapeDtypeStruct((B, T, RWKV_W), BF16),
                   jax.ShapeDtypeStruct((B, RWKV_HEADS, RWKV_HD, RWKV_HD), F32)],
        scratch_shapes=[pltpu.VMEM((G, 1, RWKV_COLS), F32), pltpu.VMEM((G, PAIRS, LANES, LANES), F32)]
                       + [buf() for _ in range(8)]
                       + [per_chunk(LANES, BF16), per_chunk(LANES, F32), per_chunk(LANES, BF16),
                          per_chunk(CHUNK, BF16), per_chunk(LANES, F32)],
        compiler_params=_params(),
        name="rwkv7",
    )(p, shift_in, s0, lw["mu"], lw["w0"], lw["w2p"], lw["a0"], lw["a2p"], lw["g2"], lw["kk"],
      lw["ka"], lw["rk"], lw["ln_g"], lw["ln_b"], e)


def _swa_kernel(sink_ref, q_ref, k_ref, v_ref, hk_ref, hv_ref, o_ref, kd, vd, *, has_cache):
    G, L, _ = q_ref.shape
    C = CHUNK
    nc = L // C
    KB = WINDOW + C
    GROUP = 4
    j = pl.program_id(1)
    lo = lax.broadcasted_iota(jnp.int32, (1, LANES), 1) < SWA_HD
    qi = lax.broadcasted_iota(jnp.int32, (C, KB), 0)
    kj = lax.broadcasted_iota(jnp.int32, (C, KB), 1)
    dist = jnp.abs(WINDOW + qi - kj).astype(F32)
    kj4 = lax.broadcasted_iota(jnp.int32, (GROUP * C, KB), 1)
    ones = jnp.ones((WINDOW + L, LANES), F32)
    neg_pad = jnp.full((GROUP * C, 2 * LANES - KB), -jnp.inf, F32)
    bias = [jnp.concatenate([(2.0 ** -(GROUP * kv + h + 1)) * dist for h in range(GROUP)], axis=0)
            for kv in range(2)]
    sink = [jnp.concatenate([jnp.full((C, LANES), sink_ref[GROUP * kv + h], F32) for h in range(GROUP)], axis=0)
            for kv in range(2)]
    for g in range(G):
        kc = jnp.concatenate([hk_ref[g], k_ref[g]], axis=0)
        vc = jnp.concatenate([hv_ref[g], v_ref[g]], axis=0)
        ks = pltpu.roll(kc, SWA_HD, 1)
        vs = pltpu.roll(vc, SWA_HD, 1)
        kd[g, 0] = jnp.where(lo, kc, ks).astype(BF16)
        kd[g, 1] = jnp.where(lo, ks, kc).astype(BF16)
        vd[g, 0] = jnp.concatenate([jnp.where(lo, vc, vs), ones], axis=1).astype(BF16)
        vd[g, 1] = jnp.concatenate([jnp.where(lo, vs, vc), ones], axis=1).astype(BF16)

    def chunks(t, carry, mask_halo):
        units = []
        for n in range(UNROLL):
            i = t * UNROLL + n
            for kv in range(2):
                units.append((i // nc, pl.multiple_of((i % nc) * C, C), kv))
        scs = []
        for g, off, kv in units:
            q0 = q_ref[g, pl.ds(off, C), (2 * kv) * LANES:(2 * kv + 1) * LANES]
            q1 = q_ref[g, pl.ds(off, C), (2 * kv + 1) * LANES:(2 * kv + 2) * LANES]
            zero = jnp.zeros_like(q0)
            lhs = jnp.concatenate([jnp.where(lo, q0, zero), jnp.where(lo, zero, q0),
                                   jnp.where(lo, q1, zero), jnp.where(lo, zero, q1)], axis=0)
            sc = lax.dot_general(lhs, kd[g, kv, pl.ds(off, KB), :], ((_NT), ((), ())),
                                 preferred_element_type=F32) - bias[kv]
            if mask_halo:
                sc = jnp.where(kj4 + off < WINDOW, -jnp.inf, sc)
            scs.append(sc)
        exs, sink_terms = [], []
        for (g, off, kv), sc in zip(units, scs):
            folded = jnp.maximum(sc[:, :LANES], jnp.concatenate([sc[:, LANES:], neg_pad], axis=1))
            mx = jnp.maximum(jnp.broadcast_to(jnp.max(folded, axis=-1, keepdims=True), (GROUP * C, LANES)),
                             sink[kv])
            ex = jnp.concatenate([jnp.exp(sc[:, :LANES] - mx), jnp.exp(sc[:, LANES:] - mx[:, :KB - LANES])],
                                 axis=1)
            exs.append(ex.astype(BF16))
            sink_terms.append(jnp.exp(sink[kv] - mx))
        for (g, off, kv), ex, st in zip(units, exs, sink_terms):
            pvd = jnp.dot(ex, vd[g, kv, pl.ds(off, KB), :], preferred_element_type=F32)
            pv = pvd[:, :LANES] / (pvd[:, LANES:] + st)
            pv = pv.astype(BF16)
            o_ref[g, pl.ds(off, C), (2 * kv) * LANES:(2 * kv + 1) * LANES] = jnp.where(lo, pv[0:C], pv[C:2 * C])
            o_ref[g, pl.ds(off, C), (2 * kv + 1) * LANES:(2 * kv + 2) * LANES] = jnp.where(
                lo, pv[2 * C:3 * C], pv[3 * C:4 * C])
        return carry

    steps = G * nc // UNROLL
    if has_cache:
        lax.fori_loop(0, steps, functools.partial(chunks, mask_halo=False), 0)
    else:
        @pl.when(j == 0)
        def _():
            lax.fori_loop(0, steps, functools.partial(chunks, mask_halo=True), 0)

        @pl.when(j != 0)
        def _():
            lax.fori_loop(0, steps, functools.partial(chunks, mask_halo=False), 0)


def _swa(q, k, v, halo_k, halo_v, sink, G, L, has_cache):
    B, T, _ = q.shape
    if has_cache:
        halo = pl.BlockSpec((G, WINDOW, LANES), lambda i, j: (i, 0, 0))
    else:
        per = L // WINDOW
        halo = pl.BlockSpec((G, WINDOW, LANES), lambda i, j: (i, jnp.maximum(j * per - 1, 0), 0))
    tile = lambda c: pl.BlockSpec((G, L, c), lambda i, j: (i, j, 0))
    cat = lambda cols: pltpu.VMEM((G, 2, WINDOW + L, cols), BF16)
    return pl.pallas_call(
        functools.partial(_swa_kernel, has_cache=has_cache),
        grid=(B // G, T // L),
        in_specs=[pl.BlockSpec(memory_space=pltpu.SMEM), tile(SWA_HEADS * SWA_HD), tile(LANES), tile(LANES),
                  halo, halo],
        out_specs=tile(SWA_HEADS * SWA_HD),
        out_shape=jax.ShapeDtypeStruct((B, T, SWA_HEADS * SWA_HD), BF16),
        scratch_shapes=[cat(LANES), cat(2 * LANES)],
        compiler_params=_params(),
        name="swa",
    )(sink, q, k, v, halo_k, halo_v)


def _mem_kv_kernel(m_ref, g_ref, w_ref, kng_ref, mk_ref, mv_ref):
    G, M, D = m_ref.shape
    x = m_ref[...].reshape(G * M, D)
    h = (x * lax.rsqrt(jnp.mean(x * x, axis=-1, keepdims=True) + RMS_EPS) * g_ref[...]).astype(BF16)
    for s in range(MEM_HEADS):
        z = jnp.dot(h, w_ref[:, s * MEM_HD:(s + 1) * MEM_HD], preferred_element_type=F32)
        z = z * lax.rsqrt(jnp.mean(z * z, axis=-1, keepdims=True) + RMS_EPS) * kng_ref[...]
        zv = jnp.dot(h, w_ref[:, MEM_W + s * MEM_HD:MEM_W + (s + 1) * MEM_HD], preferred_element_type=F32)
        for g in range(G):
            mk_ref[g, pl.ds(s, M, stride=MEM_HEADS), :] = z[g * M:(g + 1) * M]
            mv_ref[g, pl.ds(s, M, stride=MEM_HEADS), :] = zv[g * M:(g + 1) * M]


def _mem_kv(mem, g, w_b, l, kng):
    B, M, D = mem.shape
    heads = pl.BlockSpec((1, M * MEM_HEADS, MEM_HD), lambda i, j: (i, 0, 0))
    return pl.pallas_call(
        _mem_kv_kernel,
        grid=(B, 1),
        in_specs=[pl.BlockSpec((1, M, D), lambda i, j: (i, 0, 0)), _const_spec((1, D)),
                  _layer_spec((D, 2 * MEM_W), l), _const_spec((1, MEM_HD))],
        out_specs=[heads, heads],
        out_shape=[jax.ShapeDtypeStruct((B, M * MEM_HEADS, MEM_HD), F32)] * 2,
        compiler_params=_params(),
        name="mem_kv",
    )(mem, g, w_b, kng)


def _mem_att_kernel(q_ref, mk_ref, mv_ref, o_ref, kb, vb):
    G, L, _ = q_ref.shape
    units = [(g, h) for g in range(G) for h in range(MEM_HEADS)]
    sl = lambda h: slice(h * MEM_HD, (h + 1) * MEM_HD)

    @pl.when(pl.program_id(1) == 0)
    def _():
        for g, h in units:
            kb[g, h] = mk_ref[g, pl.ds(h, MEM_TOKENS, stride=MEM_HEADS), :].astype(BF16)
            vb[g, h] = mv_ref[g, pl.ds(h, MEM_TOKENS, stride=MEM_HEADS), :].astype(BF16)

    ones = jnp.ones((MEM_TOKENS, MEM_HD), BF16)
    scs = [lax.dot_general(q_ref[g, :, sl(h)], kb[g, h], (_NT, ((), ())), preferred_element_type=F32)
           for g, h in units]
    exs = [jnp.exp(sc - jnp.max(sc, axis=-1, keepdims=True)).astype(BF16) for sc in scs]
    for (g, h), ex in zip(units, exs):
        den = jnp.dot(ex, ones, preferred_element_type=F32)
        pv = jnp.dot(ex, vb[g, h], preferred_element_type=F32)
        o_ref[g, :, sl(h)] = (pv / den).astype(BF16)


def _mem_att(qm, mk, mv, l, G, L):
    B, T, _ = qm.shape
    tile = pl.BlockSpec((G, L, MEM_W), lambda i, j: (i, j, 0))
    if l is None:
        mem = pl.BlockSpec((G, MEM_TOKENS * MEM_HEADS, MEM_HD), lambda i, j: (i, 0, 0))
    else:
        mem = pl.BlockSpec((None, G, MEM_TOKENS * MEM_HEADS, MEM_HD), lambda i, j: (l, i, 0, 0))
    gathered = pltpu.VMEM((G, MEM_HEADS, MEM_TOKENS, MEM_HD), BF16)
    return pl.pallas_call(
        _mem_att_kernel,
        grid=(B // G, T // L),
        in_specs=[tile, mem, mem],
        out_specs=tile,
        out_shape=jax.ShapeDtypeStruct((B, T, MEM_W), BF16),
        scratch_shapes=[gathered, gathered],
        compiler_params=_params(),
        name="mem_att",
    )(qm, mk, mv)


def _merge_kernel(x_ref, oa_ref, ob_ref, om_ref, g1_ref, wg_ref, wb_ref, wo_ref, y_ref):
    G, L, D = x_ref.shape
    R = G * L
    x = x_ref[...].reshape(R, D)
    hb = (x * lax.rsqrt(jnp.mean(x * x, axis=-1, keepdims=True) + RMS_EPS) * g1_ref[...]).astype(BF16)
    mix = None
    for n, o_ref in enumerate((oa_ref, ob_ref, om_ref)):
        br = jnp.dot(o_ref[...].reshape(R, RWKV_W), wb_ref[n], preferred_element_type=F32)
        gate = _sigmoid(jnp.dot(hb, wg_ref[:, n * D:(n + 1) * D], preferred_element_type=F32))
        mix = gate * br if mix is None else mix + gate * br
    y_ref[...] = (x + jnp.dot(mix.astype(BF16), wo_ref[...], preferred_element_type=F32)).reshape(G, L, D)


def _merge(x, oa, ob, om, g1, w_in_b, wb_b, wo_b, l, G, L):
    B, T, D = x.shape
    tile = lambda c: pl.BlockSpec((G, L, c), lambda i, j: (i, j, 0))
    return pl.pallas_call(
        _merge_kernel,
        grid=(B // G, T // L),
        in_specs=[tile(D), tile(RWKV_W), tile(RWKV_W), tile(MEM_W), _const_spec((1, D)),
                  pl.BlockSpec((None, D, N_BRANCH * D), lambda *_: (l, 0, C_GT // (N_BRANCH * D)),
                               pipeline_mode=pl.Buffered(1)),
                  _layer_spec((N_BRANCH, RWKV_W, D), l), _layer_spec((D, D), l)],
        out_specs=tile(D),
        out_shape=jax.ShapeDtypeStruct((B, T, D), F32),
        compiler_params=_params(),
        name="merge",
    )(x, oa, ob, om, g1, w_in_b, wb_b, wo_b)


def _ffn_kernel(x_ref, cin_ref, g2_ref, wu_ref, cw_ref, cb_ref, wd_ref, y_ref, cout_ref, carry):
    G, L, D = x_ref.shape
    R = G * L
    j = pl.program_id(1)

    @pl.when(j == 0)
    def _():
        carry[...] = cin_ref[...]

    x = x_ref[...].reshape(R, D)
    hb = (x * lax.rsqrt(jnp.mean(x * x, axis=-1, keepdims=True) + RMS_EPS) * g2_ref[...]).astype(BF16)
    row = lax.broadcasted_iota(jnp.int32, (L, 1), 0)
    acc = x
    for c0, c1 in FF_BLOCKS:
        cs = slice(c0, c1)
        a_in = jnp.dot(hb, wu_ref[:, cs], preferred_element_type=F32)
        u = jnp.dot(hb, wu_ref[:, D_FF + c0:D_FF + c1], preferred_element_type=F32)
        convs = []
        for g in range(G):
            a = a_in[g * L:(g + 1) * L]
            prev = carry[g, :, cs]
            a1 = jnp.where(row == 0, prev[1:2], pltpu.roll(a, 1, 0))
            a2 = jnp.where(row == 0, prev[0:1], jnp.where(row == 1, prev[1:2], pltpu.roll(a, 2, 0)))
            carry[g, :, cs] = a[L - 2:L]
            convs.append(cb_ref[:, cs] + a2 * cw_ref[0:1, cs] + a1 * cw_ref[1:2, cs] + a * cw_ref[2:3, cs])
        c = convs[0] if G == 1 else jnp.concatenate(convs, axis=0)
        gelu = 0.5 * c * (1.0 + jnp.tanh(0.7978845608028654 * (c + 0.044715 * (c * c * c))))
        acc = acc + jnp.dot((gelu * u).astype(BF16), wd_ref[cs, :], preferred_element_type=F32)
    y_ref[...] = acc.reshape(G, L, D)

    @pl.when(j == pl.num_programs(1) - 1)
    def _():
        cout_ref[...] = carry[...]


def _ffn(x, conv_in, g2, wu_b, cw, cb, wd_b, l, G, L):
    B, T, D = x.shape
    tile = pl.BlockSpec((G, L, D), lambda i, j: (i, j, 0))
    st = pl.BlockSpec((G, CONV_W - 1, D_FF), lambda i, j: (i, 0, 0))
    return pl.pallas_call(
        _ffn_kernel,
        grid=(B // G, T // L),
        in_specs=[tile, st, _const_spec((1, D)), _layer_spec((D, 2 * D_FF), l), _const_spec((CONV_W, D_FF)),
                  _const_spec((1, D_FF)), _layer_spec((D_FF, D), l)],
        out_specs=[tile, st],
        out_shape=[jax.ShapeDtypeStruct((B, T, D), F32), jax.ShapeDtypeStruct((B, CONV_W - 1, D_FF), F32)],
        scratch_shapes=[pltpu.VMEM((G, CONV_W - 1, D_FF), F32)],
        compiler_params=_params(),
        name="conv_ffn",
    )(x, conv_in, g2, wu_b, cw, cb, wd_b)


def _layer(x, lw, e, mk, mv, mem_layer, shift_in, s0, conv_in, halo_k, halo_v, tiles):
    has_cache = halo_k is not None
    p, q, k, v, qm = _in_proj(x, lw["norm1_g"], lw["w_in"], lw["layer"], lw["qn_g"], lw["kn_g"], lw["mqn_g"], e,
                              *tiles["dense"])
    oa, s_new = _rwkv(p, shift_in, s0, lw, e, *tiles["rwkv"])
    if has_cache:
        ob = _swa(q, k, v, halo_k, halo_v, lw["sink"], *tiles["att"], True)
    else:
        ob = _swa(q, k, v, k, v, lw["sink"], *tiles["att"], False)
    om = _mem_att(qm, mk, mv, mem_layer, *tiles["att"])
    x = _merge(x, oa, ob, om, lw["norm1_g"], lw["w_in"], lw["w_branch"], lw["w_out"], lw["layer"], *tiles["dense"])
    x, conv_new = _ffn(x, conv_in, lw["norm2_g"], lw["w_up"], lw["conv_w"], lw["conv_b"], lw["w_down"], lw["layer"],
                       *tiles["dense"])
    return x, (k, v, s_new, p[:, -1:, :], conv_new)


def kernel(x_prompt, x_sample, cache_swa_k, cache_swa_v, cache_mem_k, cache_mem_v, state_rwkv, state_shift, state_conv, mem_prompt, norm1_g, w_in, rwkv_mu, rwkv_w0, rwkv_w2, rwkv_a0, rwkv_a2, rwkv_g2, rwkv_kk, rwkv_ka, rwkv_rk, rwkv_ln_g, rwkv_ln_b, swa_qn_g, swa_kn_g, swa_sink, mem_norm_g, w_mem_kv, mem_qn_g, mem_kn_g, w_branch, w_out, norm2_g, w_up, conv_w, conv_b, w_down):
    Bp, Tp, _ = x_prompt.shape
    Bs, Ts, _ = x_sample.shape
    dt = x_prompt.dtype
    half = jnp.arange(LANES) // RWKV_HD
    e = (half[:, None] == half[None, :]).astype(BF16)
    row = lambda a: a.reshape(1, -1)
    zpad = jnp.zeros((LANES - 64, RWKV_W), dt)

    w_in_b, w_branch_b, w_out_b, w_up_b, w_down_b, w_mem_kv_b = (
        w.astype(BF16) for w in (w_in, w_branch, w_out, w_up, w_down, w_mem_kv))
    mem_k_rows = cache_mem_k.reshape(DEPTH, Bs, MEM_TOKENS * MEM_HEADS, MEM_HD)
    mem_v_rows = cache_mem_v.reshape(DEPTH, Bs, MEM_TOKENS * MEM_HEADS, MEM_HD)
    yp, ys = x_prompt, x_sample
    outs_p = [[] for _ in range(7)]
    outs_s = [[] for _ in range(5)]
    prompt_tiles = {"dense": (1, 512), "rwkv": (Bp, 128), "att": (1, 256)}
    sample_tiles = {"dense": (8, Ts), "rwkv": (8, Ts), "att": (8, Ts)}
    for l in range(DEPTH):
        lw = {
            "layer": l, "norm1_g": row(norm1_g[l]), "w_in": w_in_b,
            "qn_g": row(jnp.tile(swa_qn_g[l], 2)) * (SWA_HD ** -0.5), "kn_g": row(jnp.tile(swa_kn_g[l], 2)),
            "mqn_g": row(mem_qn_g[l]) * (MEM_HD ** -0.5),
            "mu": row(rwkv_mu[l]), "w0": row(rwkv_w0[l]),
            "w2p": jnp.concatenate([rwkv_w2[l], zpad], axis=0),
            "a0": row(rwkv_a0[l]),
            "a2p": jnp.concatenate([zpad, rwkv_a2[l]], axis=0),
            "g2": rwkv_g2[l], "kk": row(rwkv_kk[l]), "ka": row(rwkv_ka[l]), "rk": row(rwkv_rk[l]),
            "ln_g": row(rwkv_ln_g[l]), "ln_b": row(rwkv_ln_b[l]),
            "sink": swa_sink[l],
            "w_branch": w_branch_b, "w_out": w_out_b,
            "norm2_g": row(norm2_g[l]), "w_up": w_up_b, "conv_w": conv_w[l],
            "conv_b": row(conv_b[l]), "w_down": w_down_b,
        }
        mk, mv = _mem_kv(mem_prompt, row(mem_norm_g[l]), w_mem_kv_b, l, row(mem_kn_g[l]))
        yp, (k, v, s_new, sh_new, cv_new) = _layer(
            yp, lw, e, mk, mv, None,
            jnp.zeros((Bp, 1, RWKV_COLS), dt),
            jnp.zeros((Bp, RWKV_HEADS, RWKV_HD, RWKV_HD), dt),
            jnp.zeros((Bp, CONV_W - 1, D_FF), dt), None, None, prompt_tiles)
        kv_shape = (Bp, WINDOW, 2, SWA_HD)
        for lst, val in zip(outs_p, (k[:, -WINDOW:].reshape(kv_shape), v[:, -WINDOW:].reshape(kv_shape),
                                     mk.reshape(Bp, MEM_TOKENS, MEM_HEADS, MEM_HD),
                                     mv.reshape(Bp, MEM_TOKENS, MEM_HEADS, MEM_HD), s_new, sh_new, cv_new)):
            lst.append(val)
        ck = cache_swa_k[l].reshape(Bs, WINDOW, LANES)
        cv = cache_swa_v[l].reshape(Bs, WINDOW, LANES)
        ys, (k, v, s_new, sh_new, cv_new) = _layer(
            ys, lw, e, mem_k_rows, mem_v_rows, l,
            state_shift[l], state_rwkv[l], state_conv[l], ck, cv, sample_tiles)
        kv_shape = (Bs, WINDOW, 2, SWA_HD)
        kf = jnp.concatenate([ck, k], axis=1)[:, -WINDOW:].reshape(kv_shape)
        vf = jnp.concatenate([cv, v], axis=1)[:, -WINDOW:].reshape(kv_shape)
        for lst, val in zip(outs_s, (kf, vf, s_new, sh_new, cv_new)):
            lst.append(val)
    return (yp, ys) + tuple(jnp.stack(o) for o in outs_p) + tuple(jnp.stack(o) for o in outs_s)
```

```python
import functools
import math

import jax
import jax.numpy as jnp
from jax import lax
from jax.experimental import pallas as pl
from jax.experimental.pallas import tpu as pltpu

F32 = jnp.float32
BF16 = jnp.bfloat16
HIGHEST = lax.Precision.HIGHEST

D_MODEL = 1024
DEPTH = 2
CHUNK = 64
RWKV_HEADS = 8
RWKV_HD = 64
RWKV_W = 512
RWKV_COLS = 1792
GN_EPS = 64e-5
SWA_HEADS = 8
SWA_HD = 64
WINDOW = 128
MEM_TOKENS = 256
MEM_HEADS = 4
MEM_HD = 128
MEM_W = 512
N_BRANCH = 3
D_FF = 2816
CONV_W = 3
RMS_EPS = 1e-6

LANES = 128
PAIRS = RWKV_W // LANES
VMEM_LIMIT = 56 * 1024 * 1024
MXU_DIM = 256
FF_BLOCKS = ((0, 6 * MXU_DIM), (6 * MXU_DIM, D_FF))
UNROLL = 4

C_Q = RWKV_COLS
C_K = C_Q + SWA_HEADS * SWA_HD
C_V = C_K + LANES
C_QM = C_V + LANES
C_GT = C_QM + MEM_W
IN_COLS = C_GT + N_BRANCH * D_MODEL


def _dot(a, b, dims=((1,), (0,)), exact=False):
    if exact:
        return lax.dot_general(a, b, (dims, ((), ())), precision=HIGHEST, preferred_element_type=F32)
    return lax.dot_general(a.astype(BF16), b.astype(BF16), (dims, ((), ())), preferred_element_type=F32)


_NT = ((1,), (1,))
_TN = ((0,), (0,))


def _seg_sum(x, e):
    return jnp.dot(x.astype(BF16), e, preferred_element_type=F32)


def _sigmoid(x):
    return 1.0 / (1.0 + jnp.exp(-x))


def _const_spec(shape):
    n = len(shape)
    return pl.BlockSpec(shape, lambda *_: (0,) * n, pipeline_mode=pl.Buffered(1))


def _layer_spec(shape, l):
    n = len(shape)
    return pl.BlockSpec((None,) + tuple(shape), lambda *_: (l,) + (0,) * n, pipeline_mode=pl.Buffered(1))


def _params():
    return pltpu.CompilerParams(dimension_semantics=("arbitrary", "arbitrary"), vmem_limit_bytes=VMEM_LIMIT)


def _in_kernel(x_ref, g1_ref, w_ref, qng_ref, kng_ref, mqg_ref, e_ref,
               p_ref, q_ref, k_ref, v_ref, qm_ref):
    G, L, D = x_ref.shape
    R = G * L
    x = x_ref[...].reshape(R, D)
    h = x * lax.rsqrt(jnp.mean(x * x, axis=-1, keepdims=True) + RMS_EPS) * g1_ref[...]
    hb = h.astype(BF16)
    e = e_ref[...]

    def proj(c0, c1):
        return jnp.dot(hb, w_ref[:, c0:c1], preferred_element_type=F32)

    def head_rms(z, gain):
        return z * lax.rsqrt(_seg_sum(z * z, e) * (1.0 / SWA_HD) + RMS_EPS) * gain

    p_ref[...] = proj(0, RWKV_COLS).reshape(G, L, RWKV_COLS)
    zq = proj(C_Q, C_K)
    for s in range(SWA_HEADS * SWA_HD // LANES):
        sl = slice(s * LANES, (s + 1) * LANES)
        q_ref[:, :, sl] = head_rms(zq[:, sl], qng_ref[...]).astype(BF16).reshape(G, L, LANES)
    zkv = proj(C_K, C_QM)
    k_ref[...] = head_rms(zkv[:, :LANES], kng_ref[...]).reshape(G, L, LANES)
    v_ref[...] = zkv[:, LANES:].reshape(G, L, LANES)
    zqm = proj(C_QM, C_GT)
    for s in range(MEM_HEADS):
        sl = slice(s * MEM_HD, (s + 1) * MEM_HD)
        z = zqm[:, sl]
        z = z * lax.rsqrt(jnp.mean(z * z, axis=-1, keepdims=True) + RMS_EPS) * mqg_ref[...]
        qm_ref[:, :, sl] = z.astype(BF16).reshape(G, L, MEM_HD)


def _in_proj(x, g1, w_in_b, l, qng, kng, mqg, e, G, L):
    B, T, D = x.shape
    tile = lambda c: pl.BlockSpec((G, L, c), lambda i, j: (i, j, 0))
    outs = ((RWKV_COLS, F32), (SWA_HEADS * SWA_HD, BF16), (LANES, F32), (LANES, F32), (MEM_W, BF16))
    return pl.pallas_call(
        _in_kernel,
        grid=(B // G, T // L),
        in_specs=[tile(D), _const_spec((1, D)), _layer_spec((D, C_GT), l), _const_spec((1, LANES)),
                  _const_spec((1, LANES)), _const_spec((1, MEM_HD)), _const_spec((LANES, LANES))],
        out_specs=[tile(c) for c, _ in outs],
        out_shape=[jax.ShapeDtypeStruct((B, T, c), dt) for c, dt in outs],
        compiler_params=_params(),
        name="in_proj",
    )(x, g1, w_in_b, qng, kng, mqg, e)


def _rwkv_kernel(p_ref, sh_ref, s0_ref, mu_ref, w0_ref, w2_ref, a0_ref, a2_ref, g2_ref, kk_ref, ka_ref,
                 rk_ref, lng_ref, lnb_ref, e_ref,
                 o_ref, sout_ref,
                 prev_scr, st_scr, r_scr, k_scr, v_scr, am_scr, b_scr, lw_scr, g_scr, o_scr,
                 wr_scr, uo_scr, bkt_scr, vb_scr, dcol_scr):
    G, L, _ = p_ref.shape
    C = CHUNK
    nc = L // C
    j = pl.program_id(1)
    e = e_ref[...]

    @pl.when(j == 0)
    def _():
        zero = jnp.zeros((RWKV_HD, RWKV_HD), F32)
        for g in range(G):
            for s in range(PAIRS):
                top = jnp.concatenate([s0_ref[g, 2 * s], zero], axis=1)
                bot = jnp.concatenate([zero, s0_ref[g, 2 * s + 1]], axis=1)
                st_scr[g, s] = jnp.concatenate([top, bot], axis=0).T
        prev_scr[...] = sh_ref[...]

    first_row = lax.broadcasted_iota(jnp.int32, (L, 1), 0) == 0

    def per_token(g):
        p = p_ref[g]
        shifted = jnp.where(first_row, prev_scr[g], pltpu.roll(p, 1, 0))
        pm = p + (shifted - p) * mu_ref[...]
        prev_scr[g] = p[L - 1:L, :]
        rows = slice(g * L, (g + 1) * L)
        r = pm[:, 0:RWKV_W]
        k = pm[:, RWKV_W:2 * RWKV_W]
        v = pm[:, 2 * RWKV_W:3 * RWKV_W]
        xwa = pm[:, 3 * RWKV_W:3 * RWKV_W + LANES]
        xg = pm[:, 3 * RWKV_W + LANES:RWKV_COLS]
        z = w0_ref[...] + _dot(jnp.tanh(xwa), w2_ref[...])
        a = _sigmoid(a0_ref[...] + _dot(xwa, a2_ref[...]))
        kkv = k * kk_ref[...]
        for s in range(PAIRS):
            sl = slice(s * LANES, (s + 1) * LANES)
            kks = kkv[:, sl]
            kkn = kks * lax.rsqrt(jnp.maximum(_seg_sum(kks * kks, e), 1e-24))
            am_scr[rows, sl] = -kkn
            b_scr[rows, sl] = kkn * a[:, sl]
        r_scr[rows, :] = r
        k_scr[rows, :] = k * (1.0 + (a - 1.0) * ka_ref[...])
        v_scr[rows, :] = v
        lw_scr[rows, :] = (-math.exp(-0.5)) * _sigmoid(z)
        g_scr[rows, :] = _dot(_sigmoid(xg), g2_ref[...])

    ri = lax.broadcasted_iota(jnp.int32, (C, C), 0)
    ci = lax.broadcasted_iota(jnp.int32, (C, C), 1)
    cumsum_mat = (ri >= ci).astype(BF16)
    rq = lax.broadcasted_iota(jnp.int32, (C, 2 * C), 0)
    cq = lax.broadcasted_iota(jnp.int32, (C, 2 * C), 1)
    strict_l = jnp.logical_and(cq < C, rq > cq).astype(F32)
    strict_r = jnp.logical_and(cq >= C, rq > cq - C).astype(F32)
    incl = (rq >= cq % C).astype(F32)
    incl_l = jnp.logical_and(cq < C, rq >= cq).astype(F32)
    incl_r = jnp.logical_and(cq >= C, rq >= cq - C).astype(F32)
    r2 = lax.broadcasted_iota(jnp.int32, (2 * C, 2 * C), 0)
    c2 = lax.broadcasted_iota(jnp.int32, (2 * C, 2 * C), 1)
    bd_mask = ((r2 // C) == (c2 // C)).astype(F32)
    lo = lax.broadcasted_iota(jnp.int32, (1, LANES), 1) < RWKV_HD

    def stack(z):
        return jnp.concatenate([jnp.where(lo, z, 0.0), jnp.where(lo, 0.0, z)], axis=0)

    def stack_other(z):
        return jnp.concatenate([jnp.where(lo, 0.0, z), jnp.where(lo, z, 0.0)], axis=0)

    def prepare(t):
        units = []
        for n in range(UNROLL):
            i = t * UNROLL + n
            rows = slice(i * C, (i + 1) * C)
            lw = lw_scr[rows, :]
            lw_hi = lw.astype(BF16)
            lw_lo = (lw - lw_hi.astype(F32)).astype(BF16)
            cum = (jnp.dot(cumsum_mat, lw_hi, preferred_element_type=F32)
                   + jnp.dot(cumsum_mat, lw_lo, preferred_element_type=F32))
            cum_end = cum[C - 1:C, :]
            inv = jnp.exp(-cum)
            dec_rest = jnp.exp(cum_end - cum)
            dec_end = jnp.exp(cum_end)
            kc = k_scr[rows, :]
            bc = b_scr[rows, :]
            r_t = r_scr[rows, :] * jnp.exp(cum)
            a_t = am_scr[rows, :] * jnp.exp(cum - lw)
            b_t = bc * inv
            k_t = kc * inv
            b_e = bc * dec_rest
            k_e = kc * dec_rest
            vc = v_scr[rows, :]
            for s in range(PAIRS):
                sl = slice(s * LANES, (s + 1) * LANES)
                bkt_scr[i, s] = jnp.concatenate([b_e[:, sl], k_e[:, sl]], axis=0).T.astype(BF16)
                vb_scr[i, s] = vc[:, sl].astype(BF16)
                dcol_scr[i, s] = jnp.broadcast_to(dec_end[:, sl], (LANES, LANES)).T
                units.append((i, s, a_t[:, sl], r_t[:, sl], b_t[:, sl], k_t[:, sl], vc[:, sl]))
        N = range(len(units))
        lhs = [jnp.concatenate([u[2], u[3]], axis=0).astype(BF16) for u in units]
        nbk = [_dot(lhs[n], jnp.concatenate([stack(units[n][4]), stack(units[n][5])], axis=0), _NT) for n in N]
        nb = [z[:, :LANES] for z in nbk]
        nk = [z[:, LANES:] for z in nbk]
        m = [jnp.concatenate([nb[n][0:C] * strict_l, nb[n][0:C] * strict_r], axis=0) for n in N]
        mk = [jnp.concatenate([nk[n][0:C] * strict_l, nk[n][0:C] * strict_r], axis=0) for n in N]
        vs = [stack(units[n][6]).astype(BF16) for n in N]
        vsw = [stack_other(pltpu.roll(units[n][6], RWKV_HD, 1)).astype(BF16) for n in N]
        y = [stack(units[n][2]) + _dot(mk[n], vsw[n]) for n in N]
        for it in range(6):
            if it < 5:
                my = [_dot(m[n], jnp.concatenate([m[n], y[n]], axis=1)) for n in N]
                y = [y[n] + my[n][:, LANES:] for n in N]
                m = [my[n][:, :LANES] for n in N]
            else:
                my = [_dot(m[n], y[n]) for n in N]
                y = [y[n] + my[n] for n in N]
        arb = [jnp.concatenate([nb[n][C:2 * C] * incl_l, nb[n][C:2 * C] * incl_r], axis=0) for n in N]
        ay = [_dot(arb[n], y[n]) for n in N]
        akv = [_dot(nk[n][C:2 * C] * incl, vs[n]) for n in N]
        for n in N:
            i, s = units[n][0], units[n][1]
            wf = jnp.where(lo, y[n][0:C], y[n][C:2 * C])
            rw = units[n][3] + jnp.where(lo, ay[n][0:C], ay[n][C:2 * C])
            swapped = jnp.concatenate([jnp.where(lo, y[n][C:2 * C], y[n][0:C]),
                                       jnp.where(lo, ay[n][C:2 * C], ay[n][0:C])], axis=0)
            uo = pltpu.roll(swapped, RWKV_HD, 1)
            wr_scr[i, s] = jnp.concatenate([wf, rw], axis=0).astype(BF16)
            uo_scr[i, s] = jnp.concatenate([uo[0:C], uo[C:2 * C] + akv[n]], axis=0)

    assert (UNROLL * C) % L == 0 and (G * nc) % UNROLL == 0
    seqs_per_block = UNROLL * C // L
    for t in range(G * nc // UNROLL):
        for g in range(t * seqs_per_block, (t + 1) * seqs_per_block):
            per_token(g)
        prepare(t)

    def advance(c):
        units = [(g, s, g * nc + c) for g in range(G) for s in range(PAIRS)]
        hs = [st_scr[g, s] for g, s, _ in units]
        ys = [jnp.dot(wr_scr[i, s], h.astype(BF16), preferred_element_type=F32) + uo_scr[i, s]
              for (g, s, i), h in zip(units, hs)]
        uvb = [jnp.concatenate([y[0:C].astype(BF16), vb_scr[i, s]], axis=0) for (g, s, i), y in zip(units, ys)]
        upd = [jnp.dot(bkt_scr[i, s], z, preferred_element_type=F32) for (g, s, i), z in zip(units, uvb)]
        for (g, s, i), h, y, up in zip(units, hs, ys, upd):
            st_scr[g, s] = h * dcol_scr[i, s] + up * bd_mask
            o_scr[i * C:(i + 1) * C, s * LANES:(s + 1) * LANES] = y[C:2 * C]

    for c in range(nc):
        advance(c)

    for s in range(PAIRS):
        sl = slice(s * LANES, (s + 1) * LANES)
        o = o_scr[:, sl]
        d = o - _seg_sum(o, e) * (1.0 / RWKV_HD)
        var = _seg_sum(d * d, e) * (1.0 / RWKV_HD)
        y = d * lax.rsqrt(var + GN_EPS) * lng_ref[:, sl] + lnb_ref[:, sl]
        bonus = _seg_sum(r_scr[:, sl] * k_scr[:, sl] * rk_ref[:, sl], e) * v_scr[:, sl]
        o_ref[:, :, sl] = ((y + bonus) * g_scr[:, sl]).astype(BF16).reshape(G, L, LANES)

    @pl.when(j == pl.num_programs(1) - 1)
    def _():
        for g in range(G):
            for s in range(PAIRS):
                sp = st_scr[g, s].T
                sout_ref[g, 2 * s] = sp[0:RWKV_HD, 0:RWKV_HD]
                sout_ref[g, 2 * s + 1] = sp[RWKV_HD:, RWKV_HD:]


def _rwkv(p, shift_in, s0, lw, e, G, L):
    B, T, _ = p.shape
    R = G * L
    vec = lambda c: _const_spec((1, c))
    buf = lambda: pltpu.VMEM((R, RWKV_W), F32)
    per_chunk = lambda rows, dt: pltpu.VMEM((R // CHUNK, PAIRS, rows, LANES), dt)
    state = pl.BlockSpec((G, RWKV_HEADS, RWKV_HD, RWKV_HD), lambda i, j: (i, 0, 0, 0))
    return pl.pallas_call(
        _rwkv_kernel,
        grid=(B // G, T // L),
        in_specs=[pl.BlockSpec((G, L, RWKV_COLS), lambda i, j: (i, j, 0)),
                  pl.BlockSpec((G, 1, RWKV_COLS), lambda i, j: (i, 0, 0)),
                  state,
                  vec(RWKV_COLS), vec(RWKV_W), _const_spec((LANES, RWKV_W)), vec(RWKV_W),
                  _const_spec((LANES, RWKV_W)), _const_spec((LANES, RWKV_W)), vec(RWKV_W), vec(RWKV_W),
                  vec(RWKV_W), vec(RWKV_W), vec(RWKV_W), _const_spec((LANES, LANES))],
        out_specs=[pl.BlockSpec((G, L, RWKV_W), lambda i, j: (i, j, 0)),
                   state],
        out_shape=[jax.ShapeDtypeStruct((B, T, RWKV_W), BF16),
                   jax.ShapeDtypeStruct((B, RWKV_HEADS, RWKV_HD, RWKV_HD), F32)],
        scratch_shapes=[pltpu.VMEM((G, 1, RWKV_COLS), F32), pltpu.VMEM((G, PAIRS, LANES, LANES), F32)]
                       + [buf() for _ in range(8)]
                       + [per_chunk(LANES, BF16), per_chunk(LANES, F32), per_chunk(LANES, BF16),
                          per_chunk(CHUNK, BF16), per_chunk(LANES, F32)],
        compiler_params=_params(),
        name="rwkv7",
    )(p, shift_in, s0, lw["mu"], lw["w0"], lw["w2p"], lw["a0"], lw["a2p"], lw["g2"], lw["kk"],
      lw["ka"], lw["rk"], lw["ln_g"], lw["ln_b"], e)


def _swa_kernel(sink_ref, q_ref, k_ref, v_ref, hk_ref, hv_ref, o_ref, kd, vd, *, has_cache):
    G, L, _ = q_ref.shape
    C = CHUNK
    nc = L // C
    KB = WINDOW + C
    GROUP = 4
    j = pl.program_id(1)
    lo = lax.broadcasted_iota(jnp.int32, (1, LANES), 1) < SWA_HD
    qi = lax.broadcasted_iota(jnp.int32, (C, KB), 0)
    kj = lax.broadcasted_iota(jnp.int32, (C, KB), 1)
    dist = jnp.abs(WINDOW + qi - kj).astype(F32)
    kj4 = lax.broadcasted_iota(jnp.int32, (GROUP * C, KB), 1)
    ones = jnp.ones((WINDOW + L, LANES), F32)
    neg_pad = jnp.full((GROUP * C, 2 * LANES - KB), -jnp.inf, F32)
    bias = [jnp.concatenate([(2.0 ** -(GROUP * kv + h + 1)) * dist for h in range(GROUP)], axis=0)
            for kv in range(2)]
    sink = [jnp.concatenate([jnp.full((C, LANES), sink_ref[GROUP * kv + h], F32) for h in range(GROUP)], axis=0)
            for kv in range(2)]
    for g in range(G):
        kc = jnp.concatenate([hk_ref[g], k_ref[g]], axis=0)
        vc = jnp.concatenate([hv_ref[g], v_ref[g]], axis=0)
        ks = pltpu.roll(kc, SWA_HD, 1)
        vs = pltpu.roll(vc, SWA_HD, 1)
        kd[g, 0] = jnp.where(lo, kc, ks).astype(BF16)
        kd[g, 1] = jnp.where(lo, ks, kc).astype(BF16)
        vd[g, 0] = jnp.concatenate([jnp.where(lo, vc, vs), ones], axis=1).astype(BF16)
        vd[g, 1] = jnp.concatenate([jnp.where(lo, vs, vc), ones], axis=1).astype(BF16)

    def chunks(t, carry, mask_halo):
        units = []
        for n in range(UNROLL):
            i = t * UNROLL + n
            for kv in range(2):
                units.append((i // nc, pl.multiple_of((i % nc) * C, C), kv))
        scs = []
        for g, off, kv in units:
            q0 = q_ref[g, pl.ds(off, C), (2 * kv) * LANES:(2 * kv + 1) * LANES]
            q1 = q_ref[g, pl.ds(off, C), (2 * kv + 1) * LANES:(2 * kv + 2) * LANES]
            zero = jnp.zeros_like(q0)
            lhs = jnp.concatenate([jnp.where(lo, q0, zero), jnp.where(lo, zero, q0),
                                   jnp.where(lo, q1, zero), jnp.where(lo, zero, q1)], axis=0)
            sc = lax.dot_general(lhs, kd[g, kv, pl.ds(off, KB), :], ((_NT), ((), ())),
                                 preferred_element_type=F32) - bias[kv]
            if mask_halo:
                sc = jnp.where(kj4 + off < WINDOW, -jnp.inf, sc)
            scs.append(sc)
        exs, sink_terms = [], []
        for (g, off, kv), sc in zip(units, scs):
            folded = jnp.maximum(sc[:, :LANES], jnp.concatenate([sc[:, LANES:], neg_pad], axis=1))
            mx = jnp.maximum(jnp.broadcast_to(jnp.max(folded, axis=-1, keepdims=True), (GROUP * C, LANES)),
                             sink[kv])
            ex = jnp.concatenate([jnp.exp(sc[:, :LANES] - mx), jnp.exp(sc[:, LANES:] - mx[:, :KB - LANES])],
                                 axis=1)
            exs.append(ex.astype(BF16))
            sink_terms.append(jnp.exp(sink[kv] - mx))
        for (g, off, kv), ex, st in zip(units, exs, sink_terms):
            pvd = jnp.dot(ex, vd[g, kv, pl.ds(off, KB), :], preferred_element_type=F32)
            pv = pvd[:, :LANES] / (pvd[:, LANES:] + st)
            pv = pv.astype(BF16)
            o_ref[g, pl.ds(off, C), (2 * kv) * LANES:(2 * kv + 1) * LANES] = jnp.where(lo, pv[0:C], pv[C:2 * C])
            o_ref[g, pl.ds(off, C), (2 * kv + 1) * LANES:(2 * kv + 2) * LANES] = jnp.where(
                lo, pv[2 * C:3 * C], pv[3 * C:4 * C])
        return carry

    steps = G * nc // UNROLL
    if has_cache:
        lax.fori_loop(0, steps, functools.partial(chunks, mask_halo=False), 0)
    else:
        @pl.when(j == 0)
        def _():
            lax.fori_loop(0, steps, functools.partial(chunks, mask_halo=True), 0)

        @pl.when(j != 0)
        def _():
            lax.fori_loop(0, steps, functools.partial(chunks, mask_halo=False), 0)


def _swa(q, k, v, halo_k, halo_v, sink, G, L, has_cache):
    B, T, _ = q.shape
    if has_cache:
        halo = pl.BlockSpec((G, WINDOW, LANES), lambda i, j: (i, 0, 0))
    else:
        per = L // WINDOW
        halo = pl.BlockSpec((G, WINDOW, LANES), lambda i, j: (i, jnp.maximum(j * per - 1, 0), 0))
    tile = lambda c: pl.BlockSpec((G, L, c), lambda i, j: (i, j, 0))
    cat = lambda cols: pltpu.VMEM((G, 2, WINDOW + L, cols), BF16)
    return pl.pallas_call(
        functools.partial(_swa_kernel, has_cache=has_cache),
        grid=(B // G, T // L),
        in_specs=[pl.BlockSpec(memory_space=pltpu.SMEM), tile(SWA_HEADS * SWA_HD), tile(LANES), tile(LANES),
                  halo, halo],
        out_specs=tile(SWA_HEADS * SWA_HD),
        out_shape=jax.ShapeDtypeStruct((B, T, SWA_HEADS * SWA_HD), BF16),
        scratch_shapes=[cat(LANES), cat(2 * LANES)],
        compiler_params=_params(),
        name="swa",
    )(sink, q, k, v, halo_k, halo_v)


def _mem_kv_kernel(m_ref, g_ref, w_ref, kng_ref, mk_ref, mv_ref):
    G, M, D = m_ref.shape
    x = m_ref[...].reshape(G * M, D)
    h = (x * lax.rsqrt(jnp.mean(x * x, axis=-1, keepdims=True) + RMS_EPS) * g_ref[...]).astype(BF16)
    for s in range(MEM_HEADS):
        z = jnp.dot(h, w_ref[:, s * MEM_HD:(s + 1) * MEM_HD], preferred_element_type=F32)
        z = z * lax.rsqrt(jnp.mean(z * z, axis=-1, keepdims=True) + RMS_EPS) * kng_ref[...]
        zv = jnp.dot(h, w_ref[:, MEM_W + s * MEM_HD:MEM_W + (s + 1) * MEM_HD], preferred_element_type=F32)
        for g in range(G):
            mk_ref[g, pl.ds(s, M, stride=MEM_HEADS), :] = z[g * M:(g + 1) * M]
            mv_ref[g, pl.ds(s, M, stride=MEM_HEADS), :] = zv[g * M:(g + 1) * M]


def _mem_kv(mem, g, w_b, l, kng):
    B, M, D = mem.shape
    heads = pl.BlockSpec((1, M * MEM_HEADS, MEM_HD), lambda i, j: (i, 0, 0))
    return pl.pallas_call(
        _mem_kv_kernel,
        grid=(B, 1),
        in_specs=[pl.BlockSpec((1, M, D), lambda i, j: (i, 0, 0)), _const_spec((1, D)),
                  _layer_spec((D, 2 * MEM_W), l), _const_spec((1, MEM_HD))],
        out_specs=[heads, heads],
        out_shape=[jax.ShapeDtypeStruct((B, M * MEM_HEADS, MEM_HD), F32)] * 2,
        compiler_params=_params(),
        name="mem_kv",
    )(mem, g, w_b, kng)


def _mem_att_kernel(q_ref, mk_ref, mv_ref, o_ref, kb, vb):
    G, L, _ = q_ref.shape
    units = [(g, h) for g in range(G) for h in range(MEM_HEADS)]
    sl = lambda h: slice(h * MEM_HD, (h + 1) * MEM_HD)

    @pl.when(pl.program_id(1) == 0)
    def _():
        for g, h in units:
            kb[g, h] = mk_ref[g, pl.ds(h, MEM_TOKENS, stride=MEM_HEADS), :].astype(BF16)
            vb[g, h] = mv_ref[g, pl.ds(h, MEM_TOKENS, stride=MEM_HEADS), :].astype(BF16)

    ones = jnp.ones((MEM_TOKENS, MEM_HD), BF16)
    scs = [lax.dot_general(q_ref[g, :, sl(h)], kb[g, h], (_NT, ((), ())), preferred_element_type=F32)
           for g, h in units]
    exs = [jnp.exp(sc - jnp.max(sc, axis=-1, keepdims=True)).astype(BF16) for sc in scs]
    for (g, h), ex in zip(units, exs):
        den = jnp.dot(ex, ones, preferred_element_type=F32)
        pv = jnp.dot(ex, vb[g, h], preferred_element_type=F32)
        o_ref[g, :, sl(h)] = (pv / den).astype(BF16)


def _mem_att(qm, mk, mv, l, G, L):
    B, T, _ = qm.shape
    tile = pl.BlockSpec((G, L, MEM_W), lambda i, j: (i, j, 0))
    if l is None:
        mem = pl.BlockSpec((G, MEM_TOKENS * MEM_HEADS, MEM_HD), lambda i, j: (i, 0, 0))
    else:
        mem = pl.BlockSpec((None, G, MEM_TOKENS * MEM_HEADS, MEM_HD), lambda i, j: (l, i, 0, 0))
    gathered = pltpu.VMEM((G, MEM_HEADS, MEM_TOKENS, MEM_HD), BF16)
    return pl.pallas_call(
        _mem_att_kernel,
        grid=(B // G, T // L),
        in_specs=[tile, mem, mem],
        out_specs=tile,
        out_shape=jax.ShapeDtypeStruct((B, T, MEM_W), BF16),
        scratch_shapes=[gathered, gathered],
        compiler_params=_params(),
        name="mem_att",
    )(qm, mk, mv)


def _merge_kernel(x_ref, oa_ref, ob_ref, om_ref, g1_ref, wg_ref, wb_ref, wo_ref, y_ref):
    G, L, D = x_ref.shape
    R = G * L
    x = x_ref[...].reshape(R, D)
    hb = (x * lax.rsqrt(jnp.mean(x * x, axis=-1, keepdims=True) + RMS_EPS) * g1_ref[...]).astype(BF16)
    mix = None
    for n, o_ref in enumerate((oa_ref, ob_ref, om_ref)):
        br = jnp.dot(o_ref[...].reshape(R, RWKV_W), wb_ref[n], preferred_element_type=F32)
        gate = _sigmoid(jnp.dot(hb, wg_ref[:, n * D:(n + 1) * D], preferred_element_type=F32))
        mix = gate * br if mix is None else mix + gate * br
    y_ref[...] = (x + jnp.dot(mix.astype(BF16), wo_ref[...], preferred_element_type=F32)).reshape(G, L, D)


def _merge(x, oa, ob, om, g1, w_in_b, wb_b, wo_b, l, G, L):
    B, T, D = x.shape
    tile = lambda c: pl.BlockSpec((G, L, c), lambda i, j: (i, j, 0))
    return pl.pallas_call(
        _merge_kernel,
        grid=(B // G, T // L),
        in_specs=[tile(D), tile(RWKV_W), tile(RWKV_W), tile(MEM_W), _const_spec((1, D)),
                  pl.BlockSpec((None, D, N_BRANCH * D), lambda *_: (l, 0, C_GT // (N_BRANCH * D)),
                               pipeline_mode=pl.Buffered(1)),
                  _layer_spec((N_BRANCH, RWKV_W, D), l), _layer_spec((D, D), l)],
        out_specs=tile(D),
        out_shape=jax.ShapeDtypeStruct((B, T, D), F32),
        compiler_params=_params(),
        name="merge",
    )(x, oa, ob, om, g1, w_in_b, wb_b, wo_b)


def _ffn_kernel(x_ref, cin_ref, g2_ref, wu_ref, cw_ref, cb_ref, wd_ref, y_ref, cout_ref, carry):
    G, L, D = x_ref.shape
    R = G * L
    j = pl.program_id(1)

    @pl.when(j == 0)
    def _():
        carry[...] = cin_ref[...]

    x = x_ref[...].reshape(R, D)
    hb = (x * lax.rsqrt(jnp.mean(x * x, axis=-1, keepdims=True) + RMS_EPS) * g2_ref[...]).astype(BF16)
    row = lax.broadcasted_iota(jnp.int32, (L, 1), 0)
    acc = x
    for c0, c1 in FF_BLOCKS:
        cs = slice(c0, c1)
        a_in = jnp.dot(hb, wu_ref[:, cs], preferred_element_type=F32)
        u = jnp.dot(hb, wu_ref[:, D_FF + c0:D_FF + c1], preferred_element_type=F32)
        convs = []
        for g in range(G):
            a = a_in[g * L:(g + 1) * L]
            prev = carry[g, :, cs]
            a1 = jnp.where(row == 0, prev[1:2], pltpu.roll(a, 1, 0))
            a2 = jnp.where(row == 0, prev[0:1], jnp.where(row == 1, prev[1:2], pltpu.roll(a, 2, 0)))
            carry[g, :, cs] = a[L - 2:L]
            convs.append(cb_ref[:, cs] + a2 * cw_ref[0:1, cs] + a1 * cw_ref[1:2, cs] + a * cw_ref[2:3, cs])
        c = convs[0] if G == 1 else jnp.concatenate(convs, axis=0)
        gelu = 0.5 * c * (1.0 + jnp.tanh(0.7978845608028654 * (c + 0.044715 * (c * c * c))))
        acc = acc + jnp.dot((gelu * u).astype(BF16), wd_ref[cs, :], preferred_element_type=F32)
    y_ref[...] = acc.reshape(G, L, D)

    @pl.when(j == pl.num_programs(1) - 1)
    def _():
        cout_ref[...] = carry[...]


def _ffn(x, conv_in, g2, wu_b, cw, cb, wd_b, l, G, L):
    B, T, D = x.shape
    tile = pl.BlockSpec((G, L, D), lambda i, j: (i, j, 0))
    st = pl.BlockSpec((G, CONV_W - 1, D_FF), lambda i, j: (i, 0, 0))
    return pl.pallas_call(
        _ffn_kernel,
        grid=(B // G, T // L),
        in_specs=[tile, st, _const_spec((1, D)), _layer_spec((D, 2 * D_FF), l), _const_spec((CONV_W, D_FF)),
                  _const_spec((1, D_FF)), _layer_spec((D_FF, D), l)],
        out_specs=[tile, st],
        out_shape=[jax.ShapeDtypeStruct((B, T, D), F32), jax.ShapeDtypeStruct((B, CONV_W - 1, D_FF), F32)],
        scratch_shapes=[pltpu.VMEM((G, CONV_W - 1, D_FF), F32)],
        compiler_params=_params(),
        name="conv_ffn",
    )(x, conv_in, g2, wu_b, cw, cb, wd_b)


def _layer(x, lw, e, mk, mv, mem_layer, shift_in, s0, conv_in, halo_k, halo_v, tiles):
    has_cache = halo_k is not None
    p, q, k, v, qm = _in_proj(x, lw["norm1_g"], lw["w_in"], lw["layer"], lw["qn_g"], lw["kn_g"], lw["mqn_g"], e,
                              *tiles["dense"])
    oa, s_new = _rwkv(p, shift_in, s0, lw, e, *tiles["rwkv"])
    if has_cache:
        ob = _swa(q, k, v, halo_k, halo_v, lw["sink"], *tiles["att"], True)
    else:
        ob = _swa(q, k, v, k, v, lw["sink"], *tiles["att"], False)
    om = _mem_att(qm, mk, mv, mem_layer, *tiles["att"])
    x = _merge(x, oa, ob, om, lw["norm1_g"], lw["w_in"], lw["w_branch"], lw["w_out"], lw["layer"], *tiles["dense"])
    x, conv_new = _ffn(x, conv_in, lw["norm2_g"], lw["w_up"], lw["conv_w"], lw["conv_b"], lw["w_down"], lw["layer"],
                       *tiles["dense"])
    return x, (k, v, s_new, p[:, -1:, :], conv_new)


def kernel(x_prompt, x_sample, cache_swa_k, cache_swa_v, cache_mem_k, cache_mem_v, state_rwkv, state_shift, state_conv, mem_prompt, norm1_g, w_in, rwkv_mu, rwkv_w0, rwkv_w2, rwkv_a0, rwkv_a2, rwkv_g2, rwkv_kk, rwkv_ka, rwkv_rk, rwkv_ln_g, rwkv_ln_b, swa_qn_g, swa_kn_g, swa_sink, mem_norm_g, w_mem_kv, mem_qn_g, mem_kn_g, w_branch, w_out, norm2_g, w_up, conv_w, conv_b, w_down):
    Bp, Tp, _ = x_prompt.shape
    Bs, Ts, _ = x_sample.shape
    dt = x_prompt.dtype
    half = jnp.arange(LANES) // RWKV_HD
    e = (half[:, None] == half[None, :]).astype(BF16)
    row = lambda a: a.reshape(1, -1)
    zpad = jnp.zeros((LANES - 64, RWKV_W), dt)

    w_in_b, w_branch_b, w_out_b, w_up_b, w_down_b, w_mem_kv_b = (
        w.astype(BF16) for w in (w_in, w_branch, w_out, w_up, w_down, w_mem_kv))
    mem_k_rows = cache_mem_k.reshape(DEPTH, Bs, MEM_TOKENS * MEM_HEADS, MEM_HD)
    mem_v_rows = cache_mem_v.reshape(DEPTH, Bs, MEM_TOKENS * MEM_HEADS, MEM_HD)
    yp, ys = x_prompt, x_sample
    outs_p = [[] for _ in range(7)]
    outs_s = [[] for _ in range(5)]
    prompt_tiles = {"dense": (1, 512), "rwkv": (Bp, 128), "att": (1, 256)}
    sample_tiles = {"dense": (8, Ts), "rwkv": (8, Ts), "att": (8, Ts)}
    for l in range(DEPTH):
        lw = {
            "layer": l, "norm1_g": row(norm1_g[l]), "w_in": w_in_b,
            "qn_g": row(jnp.tile(swa_qn_g[l], 2)) * (SWA_HD ** -0.5), "kn_g": row(jnp.tile(swa_kn_g[l], 2)),
            "mqn_g": row(mem_qn_g[l]) * (MEM_HD ** -0.5),
            "mu": row(rwkv_mu[l]), "w0": row(rwkv_w0[l]),
            "w2p": jnp.concatenate([rwkv_w2[l], zpad], axis=0),
            "a0": row(rwkv_a0[l]),
            "a2p": jnp.concatenate([zpad, rwkv_a2[l]], axis=0),
            "g2": rwkv_g2[l], "kk": row(rwkv_kk[l]), "ka": row(rwkv_ka[l]), "rk": row(rwkv_rk[l]),
            "ln_g": row(rwkv_ln_g[l]), "ln_b": row(rwkv_ln_b[l]),
            "sink": swa_sink[l],
            "w_branch": w_branch_b, "w_out": w_out_b,
            "norm2_g": row(norm2_g[l]), "w_up": w_up_b, "conv_w": conv_w[l],
            "conv_b": row(conv_b[l]), "w_down": w_down_b,
        }
        mk, mv = _mem_kv(mem_prompt, row(mem_norm_g[l]), w_mem_kv_b, l, row(mem_kn_g[l]))
        yp, (k, v, s_new, sh_new, cv_new) = _layer(
            yp, lw, e, mk, mv, None,
            jnp.zeros((Bp, 1, RWKV_COLS), dt),
            jnp.zeros((Bp, RWKV_HEADS, RWKV_HD, RWKV_HD), dt),
            jnp.zeros((Bp, CONV_W - 1, D_FF), dt), None, None, prompt_tiles)
        kv_shape = (Bp, WINDOW, 2, SWA_HD)
        for lst, val in zip(outs_p, (k[:, -WINDOW:].reshape(kv_shape), v[:, -WINDOW:].reshape(kv_shape),
                                     mk.reshape(Bp, MEM_TOKENS, MEM_HEADS, MEM_HD),
                                     mv.reshape(Bp, MEM_TOKENS, MEM_HEADS, MEM_HD), s_new, sh_new, cv_new)):
            lst.append(val)
        ck = cache_swa_k[l].reshape(Bs, WINDOW, LANES)
        cv = cache_swa_v[l].reshape(Bs, WINDOW, LANES)
        ys, (k, v, s_new, sh_new, cv_new) = _layer(
            ys, lw, e, mem_k_rows, mem_v_rows, l,
            state_shift[l], state_rwkv[l], state_conv[l], ck, cv, sample_tiles)
        kv_shape = (Bs, WINDOW, 2, SWA_HD)
        kf = jnp.concatenate([ck, k], axis=1)[:, -WINDOW:].reshape(kv_shape)
        vf = jnp.concatenate([cv, v], axis=1)[:, -WINDOW:].reshape(kv_shape)
        for lst, val in zip(outs_s, (kf, vf, s_new, sh_new, cv_new)):
            lst.append(val)
    return (yp, ys) + tuple(jnp.stack(o) for o in outs_p) + tuple(jnp.stack(o) for o in outs_s)
```

```python
import functools
import math

import jax
import jax.numpy as jnp
from jax import lax
from jax.experimental import pallas as pl
from jax.experimental.pallas import tpu as pltpu

F32 = jnp.float32
BF16 = jnp.bfloat16
HIGHEST = lax.Precision.HIGHEST

D_MODEL = 1024
DEPTH = 2
CHUNK = 64
RWKV_HEADS = 8
RWKV_HD = 64
RWKV_W = 512
RWKV_COLS = 1792
GN_EPS = 64e-5
SWA_HEADS = 8
SWA_HD = 64
WINDOW = 128
MEM_TOKENS = 256
MEM_HEADS = 4
MEM_HD = 128
MEM_W = 512
N_BRANCH = 3
D_FF = 2816
CONV_W = 3
RMS_EPS = 1e-6

LANES = 128
PAIRS = RWKV_W // LANES
VMEM_LIMIT = 56 * 1024 * 1024
MXU_DIM = 256
FF_BLOCKS = ((0, 6 * MXU_DIM), (6 * MXU_DIM, D_FF))
UNROLL = 4

C_Q = RWKV_COLS
C_K = C_Q + SWA_HEADS * SWA_HD
C_V = C_K + LANES
C_QM = C_V + LANES
C_GT = C_QM + MEM_W
IN_COLS = C_GT + N_BRANCH * D_MODEL


def _dot(a, b, dims=((1,), (0,)), exact=False):
    if exact:
        return lax.dot_general(a, b, (dims, ((), ())), precision=HIGHEST, preferred_element_type=F32)
    return lax.dot_general(a.astype(BF16), b.astype(BF16), (dims, ((), ())), preferred_element_type=F32)


_NT = ((1,), (1,))
_TN = ((0,), (0,))


def _seg_sum(x, e):
    return jnp.dot(x.astype(BF16), e, preferred_element_type=F32)


def _sigmoid(x):
    return 1.0 / (1.0 + jnp.exp(-x))


def _const_spec(shape):
    n = len(shape)
    return pl.BlockSpec(shape, lambda *_: (0,) * n, pipeline_mode=pl.Buffered(1))


def _layer_spec(shape, l):
    n = len(shape)
    return pl.BlockSpec((None,) + tuple(shape), lambda *_: (l,) + (0,) * n, pipeline_mode=pl.Buffered(1))


def _params():
    return pltpu.CompilerParams(dimension_semantics=("arbitrary", "arbitrary"), vmem_limit_bytes=VMEM_LIMIT)


def _in_kernel(x_ref, g1_ref, w_ref, qng_ref, kng_ref, mqg_ref, e_ref,
               p_ref, q_ref, k_ref, v_ref, qm_ref):
    G, L, D = x_ref.shape
    R = G * L
    x = x_ref[...].reshape(R, D)
    h = x * lax.rsqrt(jnp.mean(x * x, axis=-1, keepdims=True) + RMS_EPS) * g1_ref[...]
    hb = h.astype(BF16)
    e = e_ref[...]

    def proj(c0, c1):
        return jnp.dot(hb, w_ref[:, c0:c1], preferred_element_type=F32)

    def head_rms(z, gain):
        return z * lax.rsqrt(_seg_sum(z * z, e) * (1.0 / SWA_HD) + RMS_EPS) * gain

    p_ref[...] = proj(0, RWKV_COLS).reshape(G, L, RWKV_COLS)
    zq = proj(C_Q, C_K)
    zkv = proj(C_K, C_QM)
    zqm = proj(C_QM, C_GT)
    for s in range(SWA_HEADS * SWA_HD // LANES):
        sl = slice(s * LANES, (s + 1) * LANES)
        q_ref[:, :, sl] = head_rms(zq[:, sl], qng_ref[...]).astype(BF16).reshape(G, L, LANES)
    k_ref[...] = head_rms(zkv[:, :LANES], kng_ref[...]).reshape(G, L, LANES)
    v_ref[...] = zkv[:, LANES:].reshape(G, L, LANES)
    for s in range(MEM_HEADS):
        sl = slice(s * MEM_HD, (s + 1) * MEM_HD)
        z = zqm[:, sl]
        z = z * lax.rsqrt(jnp.mean(z * z, axis=-1, keepdims=True) + RMS_EPS) * mqg_ref[...]
        qm_ref[:, :, sl] = z.astype(BF16).reshape(G, L, MEM_HD)


def _in_proj(x, g1, w_in_b, l, qng, kng, mqg, e, G, L):
    B, T, D = x.shape
    tile = lambda c: pl.BlockSpec((G, L, c), lambda i, j: (i, j, 0))
    outs = ((RWKV_COLS, F32), (SWA_HEADS * SWA_HD, BF16), (LANES, F32), (LANES, F32), (MEM_W, BF16))
    return pl.pallas_call(
        _in_kernel,
        grid=(B // G, T // L),
        in_specs=[tile(D), _const_spec((1, D)), _layer_spec((D, C_GT), l), _const_spec((1, LANES)),
                  _const_spec((1, LANES)), _const_spec((1, MEM_HD)), _const_spec((LANES, LANES))],
        out_specs=[tile(c) for c, _ in outs],
        out_shape=[jax.ShapeDtypeStruct((B, T, c), dt) for c, dt in outs],
        compiler_params=_params(),
        name="in_proj",
    )(x, g1, w_in_b, qng, kng, mqg, e)


def _rwkv_kernel(p_ref, sh_ref, s0_ref, mu_ref, w0_ref, w2_ref, a0_ref, a2_ref, g2_ref, kk_ref, ka_ref,
                 rk_ref, lng_ref, lnb_ref, e_ref,
                 o_ref, sout_ref,
                 prev_scr, st_scr, r_scr, k_scr, v_scr, am_scr, b_scr, lw_scr, g_scr, o_scr,
                 wr_scr, uo_scr, bkt_scr, vb_scr, dcol_scr):
    G, L, _ = p_ref.shape
    C = CHUNK
    nc = L // C
    j = pl.program_id(1)
    e = e_ref[...]

    @pl.when(j == 0)
    def _():
        zero = jnp.zeros((RWKV_HD, RWKV_HD), F32)
        for g in range(G):
            for s in range(PAIRS):
                top = jnp.concatenate([s0_ref[g, 2 * s], zero], axis=1)
                bot = jnp.concatenate([zero, s0_ref[g, 2 * s + 1]], axis=1)
                st_scr[g, s] = jnp.concatenate([top, bot], axis=0).T
        prev_scr[...] = sh_ref[...]

    first_row = lax.broadcasted_iota(jnp.int32, (L, 1), 0) == 0

    def per_token(g):
        p = p_ref[g]
        shifted = jnp.where(first_row, prev_scr[g], pltpu.roll(p, 1, 0))
        pm = p + (shifted - p) * mu_ref[...]
        prev_scr[g] = p[L - 1:L, :]
        rows = slice(g * L, (g + 1) * L)
        r = pm[:, 0:RWKV_W]
        k = pm[:, RWKV_W:2 * RWKV_W]
        v = pm[:, 2 * RWKV_W:3 * RWKV_W]
        xwa = pm[:, 3 * RWKV_W:3 * RWKV_W + LANES]
        xg = pm[:, 3 * RWKV_W + LANES:RWKV_COLS]
        z = w0_ref[...] + _dot(jnp.tanh(xwa), w2_ref[...])
        a = _sigmoid(a0_ref[...] + _dot(xwa, a2_ref[...]))
        kkv = k * kk_ref[...]
        for s in range(PAIRS):
            sl = slice(s * LANES, (s + 1) * LANES)
            kks = kkv[:, sl]
            kkn = kks * lax.rsqrt(jnp.maximum(_seg_sum(kks * kks, e), 1e-24))
            am_scr[rows, sl] = -kkn
            b_scr[rows, sl] = kkn * a[:, sl]
        r_scr[rows, :] = r
        k_scr[rows, :] = k * (1.0 + (a - 1.0) * ka_ref[...])
        v_scr[rows, :] = v
        lw_scr[rows, :] = (-math.exp(-0.5)) * _sigmoid(z)
        g_scr[rows, :] = _dot(_sigmoid(xg), g2_ref[...])

    ri = lax.broadcasted_iota(jnp.int32, (C, C), 0)
    ci = lax.broadcasted_iota(jnp.int32, (C, C), 1)
    cumsum_mat = (ri >= ci).astype(BF16)
    rq = lax.broadcasted_iota(jnp.int32, (C, 2 * C), 0)
    cq = lax.broadcasted_iota(jnp.int32, (C, 2 * C), 1)
    strict_l = jnp.logical_and(cq < C, rq > cq).astype(F32)
    strict_r = jnp.logical_and(cq >= C, rq > cq - C).astype(F32)
    incl = (rq >= cq % C).astype(F32)
    incl_l = jnp.logical_and(cq < C, rq >= cq).astype(F32)
    incl_r = jnp.logical_and(cq >= C, rq >= cq - C).astype(F32)
    r2 = lax.broadcasted_iota(jnp.int32, (2 * C, 2 * C), 0)
    c2 = lax.broadcasted_iota(jnp.int32, (2 * C, 2 * C), 1)
    bd_mask = ((r2 // C) == (c2 // C)).astype(F32)
    lo = lax.broadcasted_iota(jnp.int32, (1, LANES), 1) < RWKV_HD

    def stack(z):
        return jnp.concatenate([jnp.where(lo, z, 0.0), jnp.where(lo, 0.0, z)], axis=0)

    def stack_other(z):
        return jnp.concatenate([jnp.where(lo, 0.0, z), jnp.where(lo, z, 0.0)], axis=0)

    def prepare(t):
        units = []
        for n in range(UNROLL):
            i = t * UNROLL + n
            rows = slice(i * C, (i + 1) * C)
            lw = lw_scr[rows, :]
            lw_hi = lw.astype(BF16)
            lw_lo = (lw - lw_hi.astype(F32)).astype(BF16)
            cum = (jnp.dot(cumsum_mat, lw_hi, preferred_element_type=F32)
                   + jnp.dot(cumsum_mat, lw_lo, preferred_element_type=F32))
            cum_end = cum[C - 1:C, :]
            inv = jnp.exp(-cum)
            dec_rest = jnp.exp(cum_end - cum)
            dec_end = jnp.exp(cum_end)
            kc = k_scr[rows, :]
            bc = b_scr[rows, :]
            r_t = r_scr[rows, :] * jnp.exp(cum)
            a_t = am_scr[rows, :] * jnp.exp(cum - lw)
            b_t = bc * inv
            k_t = kc * inv
            b_e = bc * dec_rest
            k_e = kc * dec_rest
            vc = v_scr[rows, :]
            for s in range(PAIRS):
                sl = slice(s * LANES, (s + 1) * LANES)
                bkt_scr[i, s] = jnp.concatenate([b_e[:, sl], k_e[:, sl]], axis=0).T.astype(BF16)
                vb_scr[i, s] = vc[:, sl].astype(BF16)
                dcol_scr[i, s] = jnp.broadcast_to(dec_end[:, sl], (LANES, LANES)).T
                units.append((i, s, a_t[:, sl], r_t[:, sl], b_t[:, sl], k_t[:, sl], vc[:, sl]))
        N = range(len(units))
        lhs = [jnp.concatenate([u[2], u[3]], axis=0).astype(BF16) for u in units]
        nbk = [_dot(lhs[n], jnp.concatenate([stack(units[n][4]), stack(units[n][5])], axis=0), _NT) for n in N]
        nb = [z[:, :LANES] for z in nbk]
        nk = [z[:, LANES:] for z in nbk]
        m = [jnp.concatenate([nb[n][0:C] * strict_l, nb[n][0:C] * strict_r], axis=0) for n in N]
        mk = [jnp.concatenate([nk[n][0:C] * strict_l, nk[n][0:C] * strict_r], axis=0) for n in N]
        vs = [stack(units[n][6]).astype(BF16) for n in N]
        vsw = [stack_other(pltpu.roll(units[n][6], RWKV_HD, 1)).astype(BF16) for n in N]
        y = [stack(units[n][2]) + _dot(mk[n], vsw[n]) for n in N]
        for it in range(6):
            if it < 5:
                my = [_dot(m[n], jnp.concatenate([m[n], y[n]], axis=1)) for n in N]
                y = [y[n] + my[n][:, LANES:] for n in N]
                m = [my[n][:, :LANES] for n in N]
            else:
                my = [_dot(m[n], y[n]) for n in N]
                y = [y[n] + my[n] for n in N]
        arb = [jnp.concatenate([nb[n][C:2 * C] * incl_l, nb[n][C:2 * C] * incl_r], axis=0) for n in N]
        ay = [_dot(arb[n], y[n]) for n in N]
        akv = [_dot(nk[n][C:2 * C] * incl, vs[n]) for n in N]
        for n in N:
            i, s = units[n][0], units[n][1]
            wf = jnp.where(lo, y[n][0:C], y[n][C:2 * C])
            rw = units[n][3] + jnp.where(lo, ay[n][0:C], ay[n][C:2 * C])
            swapped = jnp.concatenate([jnp.where(lo, y[n][C:2 * C], y[n][0:C]),
                                       jnp.where(lo, ay[n][C:2 * C], ay[n][0:C])], axis=0)
            uo = pltpu.roll(swapped, RWKV_HD, 1)
            wr_scr[i, s] = jnp.concatenate([wf, rw], axis=0).astype(BF16)
            uo_scr[i, s] = jnp.concatenate([uo[0:C], uo[C:2 * C] + akv[n]], axis=0)

    assert (UNROLL * C) % L == 0 and (G * nc) % UNROLL == 0
    seqs_per_block = UNROLL * C // L
    for t in range(G * nc // UNROLL):
        for g in range(t * seqs_per_block, (t + 1) * seqs_per_block):
            per_token(g)
        prepare(t)

    def advance(c):
        units = [(g, s, g * nc + c) for g in range(G) for s in range(PAIRS)]
        hs = [st_scr[g, s] for g, s, _ in units]
        ys = [jnp.dot(wr_scr[i, s], h.astype(BF16), preferred_element_type=F32) + uo_scr[i, s]
              for (g, s, i), h in zip(units, hs)]
        uvb = [jnp.concatenate([y[0:C].astype(BF16), vb_scr[i, s]], axis=0) for (g, s, i), y in zip(units, ys)]
        upd = [jnp.dot(bkt_scr[i, s], z, preferred_element_type=F32) for (g, s, i), z in zip(units, uvb)]
        for (g, s, i), h, y, up in zip(units, hs, ys, upd):
            st_scr[g, s] = h * dcol_scr[i, s] + up * bd_mask
            o_scr[i * C:(i + 1) * C, s * LANES:(s + 1) * LANES] = y[C:2 * C]

    for c in range(nc):
        advance(c)

    for s in range(PAIRS):
        sl = slice(s * LANES, (s + 1) * LANES)
        o = o_scr[:, sl]
        d = o - _seg_sum(o, e) * (1.0 / RWKV_HD)
        var = _seg_sum(d * d, e) * (1.0 / RWKV_HD)
        y = d * lax.rsqrt(var + GN_EPS) * lng_ref[:, sl] + lnb_ref[:, sl]
        bonus = _seg_sum(r_scr[:, sl] * k_scr[:, sl] * rk_ref[:, sl], e) * v_scr[:, sl]
        o_ref[:, :, sl] = ((y + bonus) * g_scr[:, sl]).astype(BF16).reshape(G, L, LANES)

    @pl.when(j == pl.num_programs(1) - 1)
    def _():
        for g in range(G):
            for s in range(PAIRS):
                sp = st_scr[g, s].T
                sout_ref[g, 2 * s] = sp[0:RWKV_HD, 0:RWKV_HD]
                sout_ref[g, 2 * s + 1] = sp[RWKV_HD:, RWKV_HD:]


def _rwkv(p, shift_in, s0, lw, e, G, L):
    B, T, _ = p.shape
    R = G * L
    vec = lambda c: _const_spec((1, c))
    buf = lambda: pltpu.VMEM((R, RWKV_W), F32)
    per_chunk = lambda rows, dt: pltpu.VMEM((R // CHUNK, PAIRS, rows, LANES), dt)
    state = pl.BlockSpec((G, RWKV_HEADS, RWKV_HD, RWKV_HD), lambda i, j: (i, 0, 0, 0))
    return pl.pallas_call(
        _rwkv_kernel,
        grid=(B // G, T // L),
        in_specs=[pl.BlockSpec((G, L, RWKV_COLS), lambda i, j: (i, j, 0)),
                  pl.BlockSpec((G, 1, RWKV_COLS), lambda i, j: (i, 0, 0)),
                  state,
                  vec(RWKV_COLS), vec(RWKV_W), _const_spec((LANES, RWKV_W)), vec(RWKV_W),
                  _const_spec((LANES, RWKV_W)), _const_spec((LANES, RWKV_W)), vec(RWKV_W), vec(RWKV_W),
                  vec(RWKV_W), vec(RWKV_W), vec(RWKV_W), _const_spec((LANES, LANES))],
        out_specs=[pl.BlockSpec((G, L, RWKV_W), lambda i, j: (i, j, 0)),
                   state],
        out_shape=[jax.ShapeDtypeStruct((B, T, RWKV_W), BF16),
                   jax.ShapeDtypeStruct((B, RWKV_HEADS, RWKV_HD, RWKV_HD), F32)],
        scratch_shapes=[pltpu.VMEM((G, 1, RWKV_COLS), F32), pltpu.VMEM((G, PAIRS, LANES, LANES), F32)]
                       + [buf() for _ in range(8)]
                       + [per_chunk(LANES, BF16), per_chunk(LANES, F32), per_chunk(LANES, BF16),
                          per_chunk(CHUNK, BF16), per_chunk(LANES, F32)],
        compiler_params=_params(),
        name="rwkv7",
    )(p, shift_in, s0, lw["mu"], lw["w0"], lw["w2p"], lw["a0"], lw["a2p"], lw["g2"], lw["kk"],
      lw["ka"], lw["rk"], lw["ln_g"], lw["ln_b"], e)


def _swa_kernel(sink_ref, q_ref, k_ref, v_ref, hk_ref, hv_ref, o_ref, kd, vd, *, has_cache):
    G, L, _ = q_ref.shape
    C = CHUNK
    nc = L // C
    KB = WINDOW + C
    GROUP = 4
    j = pl.program_id(1)
    lo = lax.broadcasted_iota(jnp.int32, (1, LANES), 1) < SWA_HD
    qi = lax.broadcasted_iota(jnp.int32, (C, KB), 0)
    kj = lax.broadcasted_iota(jnp.int32, (C, KB), 1)
    dist = jnp.abs(WINDOW + qi - kj).astype(F32)
    kj4 = lax.broadcasted_iota(jnp.int32, (GROUP * C, KB), 1)
    ones = jnp.ones((WINDOW + L, LANES), F32)
    neg_pad = jnp.full((GROUP * C, 2 * LANES - KB), -jnp.inf, F32)
    bias = [jnp.concatenate([(2.0 ** -(GROUP * kv + h + 1)) * dist for h in range(GROUP)], axis=0)
            for kv in range(2)]
    sink = [jnp.concatenate([jnp.full((C, LANES), sink_ref[GROUP * kv + h], F32) for h in range(GROUP)], axis=0)
            for kv in range(2)]
    for g in range(G):
        kc = jnp.concatenate([hk_ref[g], k_ref[g]], axis=0)
        vc = jnp.concatenate([hv_ref[g], v_ref[g]], axis=0)
        ks = pltpu.roll(kc, SWA_HD, 1)
        vs = pltpu.roll(vc, SWA_HD, 1)
        kd[g, 0] = jnp.where(lo, kc, ks).astype(BF16)
        kd[g, 1] = jnp.where(lo, ks, kc).astype(BF16)
        vd[g, 0] = jnp.concatenate([jnp.where(lo, vc, vs), ones], axis=1).astype(BF16)
        vd[g, 1] = jnp.concatenate([jnp.where(lo, vs, vc), ones], axis=1).astype(BF16)

    def chunks(t, carry, mask_halo):
        units = []
        for n in range(UNROLL):
            i = t * UNROLL + n
            for kv in range(2):
                units.append((i // nc, pl.multiple_of((i % nc) * C, C), kv))
        scs = []
        for g, off, kv in units:
            q0 = q_ref[g, pl.ds(off, C), (2 * kv) * LANES:(2 * kv + 1) * LANES]
            q1 = q_ref[g, pl.ds(off, C), (2 * kv + 1) * LANES:(2 * kv + 2) * LANES]
            zero = jnp.zeros_like(q0)
            lhs = jnp.concatenate([jnp.where(lo, q0, zero), jnp.where(lo, zero, q0),
                                   jnp.where(lo, q1, zero), jnp.where(lo, zero, q1)], axis=0)
            sc = lax.dot_general(lhs, kd[g, kv, pl.ds(off, KB), :], ((_NT), ((), ())),
                                 preferred_element_type=F32) - bias[kv]
            if mask_halo:
                sc = jnp.where(kj4 + off < WINDOW, -jnp.inf, sc)
            scs.append(sc)
        exs, sink_terms = [], []
        for (g, off, kv), sc in zip(units, scs):
            folded = jnp.maximum(sc[:, :LANES], jnp.concatenate([sc[:, LANES:], neg_pad], axis=1))
            mx = jnp.maximum(jnp.broadcast_to(jnp.max(folded, axis=-1, keepdims=True), (GROUP * C, LANES)),
                             sink[kv])
            ex = jnp.concatenate([jnp.exp(sc[:, :LANES] - mx), jnp.exp(sc[:, LANES:] - mx[:, :KB - LANES])],
                                 axis=1)
            exs.append(ex.astype(BF16))
            sink_terms.append(jnp.exp(sink[kv] - mx))
        for (g, off, kv), ex, st in zip(units, exs, sink_terms):
            pvd = jnp.dot(ex, vd[g, kv, pl.ds(off, KB), :], preferred_element_type=F32)
            pv = pvd[:, :LANES] / (pvd[:, LANES:] + st)
            pv = pv.astype(BF16)
            o_ref[g, pl.ds(off, C), (2 * kv) * LANES:(2 * kv + 1) * LANES] = jnp.where(lo, pv[0:C], pv[C:2 * C])
            o_ref[g, pl.ds(off, C), (2 * kv + 1) * LANES:(2 * kv + 2) * LANES] = jnp.where(
                lo, pv[2 * C:3 * C], pv[3 * C:4 * C])
        return carry

    steps = G * nc // UNROLL
    if has_cache:
        lax.fori_loop(0, steps, functools.partial(chunks, mask_halo=False), 0)
    else:
        @pl.when(j == 0)
        def _():
            lax.fori_loop(0, steps, functools.partial(chunks, mask_halo=True), 0)

        @pl.when(j != 0)
        def _():
            lax.fori_loop(0, steps, functools.partial(chunks, mask_halo=False), 0)


def _swa(q, k, v, halo_k, halo_v, sink, G, L, has_cache):
    B, T, _ = q.shape
    if has_cache:
        halo = pl.BlockSpec((G, WINDOW, LANES), lambda i, j: (i, 0, 0))
    else:
        per = L // WINDOW
        halo = pl.BlockSpec((G, WINDOW, LANES), lambda i, j: (i, jnp.maximum(j * per - 1, 0), 0))
    tile = lambda c: pl.BlockSpec((G, L, c), lambda i, j: (i, j, 0))
    cat = lambda cols: pltpu.VMEM((G, 2, WINDOW + L, cols), BF16)
    return pl.pallas_call(
        functools.partial(_swa_kernel, has_cache=has_cache),
        grid=(B // G, T // L),
        in_specs=[pl.BlockSpec(memory_space=pltpu.SMEM), tile(SWA_HEADS * SWA_HD), tile(LANES), tile(LANES),
                  halo, halo],
        out_specs=tile(SWA_HEADS * SWA_HD),
        out_shape=jax.ShapeDtypeStruct((B, T, SWA_HEADS * SWA_HD), BF16),
        scratch_shapes=[cat(LANES), cat(2 * LANES)],
        compiler_params=_params(),
        name="swa",
    )(sink, q, k, v, halo_k, halo_v)


def _mem_kv_kernel(m_ref, g_ref, w_ref, kng_ref, mk_ref, mv_ref):
    G, M, D = m_ref.shape
    x = m_ref[...].reshape(G * M, D)
    h = (x * lax.rsqrt(jnp.mean(x * x, axis=-1, keepdims=True) + RMS_EPS) * g_ref[...]).astype(BF16)
    for s in range(MEM_HEADS):
        z = jnp.dot(h, w_ref[:, s * MEM_HD:(s + 1) * MEM_HD], preferred_element_type=F32)
        z = z * lax.rsqrt(jnp.mean(z * z, axis=-1, keepdims=True) + RMS_EPS) * kng_ref[...]
        zv = jnp.dot(h, w_ref[:, MEM_W + s * MEM_HD:MEM_W + (s + 1) * MEM_HD], preferred_element_type=F32)
        for g in range(G):
            mk_ref[g, pl.ds(s, M, stride=MEM_HEADS), :] = z[g * M:(g + 1) * M]
            mv_ref[g, pl.ds(s, M, stride=MEM_HEADS), :] = zv[g * M:(g + 1) * M]


def _mem_kv(mem, g, w_b, l, kng):
    B, M, D = mem.shape
    heads = pl.BlockSpec((1, M * MEM_HEADS, MEM_HD), lambda i, j: (i, 0, 0))
    return pl.pallas_call(
        _mem_kv_kernel,
        grid=(B, 1),
        in_specs=[pl.BlockSpec((1, M, D), lambda i, j: (i, 0, 0)), _const_spec((1, D)),
                  _layer_spec((D, 2 * MEM_W), l), _const_spec((1, MEM_HD))],
        out_specs=[heads, heads],
        out_shape=[jax.ShapeDtypeStruct((B, M * MEM_HEADS, MEM_HD), F32)] * 2,
        compiler_params=_params(),
        name="mem_kv",
    )(mem, g, w_b, kng)


def _mem_att_kernel(q_ref, mk_ref, mv_ref, o_ref, kb, vb):
    G, L, _ = q_ref.shape
    units = [(g, h) for g in range(G) for h in range(MEM_HEADS)]
    sl = lambda h: slice(h * MEM_HD, (h + 1) * MEM_HD)

    @pl.when(pl.program_id(1) == 0)
    def _():
        for g, h in units:
            kb[g, h] = mk_ref[g, pl.ds(h, MEM_TOKENS, stride=MEM_HEADS), :].astype(BF16)
            vb[g, h] = mv_ref[g, pl.ds(h, MEM_TOKENS, stride=MEM_HEADS), :].astype(BF16)

    ones = jnp.ones((MEM_TOKENS, MEM_HD), BF16)
    scs = [lax.dot_general(q_ref[g, :, sl(h)], kb[g, h], (_NT, ((), ())), preferred_element_type=F32)
           for g, h in units]
    exs = [jnp.exp(sc - jnp.max(sc, axis=-1, keepdims=True)).astype(BF16) for sc in scs]
    for (g, h), ex in zip(units, exs):
        den = jnp.dot(ex, ones, preferred_element_type=F32)
        pv = jnp.dot(ex, vb[g, h], preferred_element_type=F32)
        o_ref[g, :, sl(h)] = (pv / den).astype(BF16)


def _mem_att(qm, mk, mv, l, G, L):
    B, T, _ = qm.shape
    tile = pl.BlockSpec((G, L, MEM_W), lambda i, j: (i, j, 0))
    if l is None:
        mem = pl.BlockSpec((G, MEM_TOKENS * MEM_HEADS, MEM_HD), lambda i, j: (i, 0, 0))
    else:
        mem = pl.BlockSpec((None, G, MEM_TOKENS * MEM_HEADS, MEM_HD), lambda i, j: (l, i, 0, 0))
    gathered = pltpu.VMEM((G, MEM_HEADS, MEM_TOKENS, MEM_HD), BF16)
    return pl.pallas_call(
        _mem_att_kernel,
        grid=(B // G, T // L),
        in_specs=[tile, mem, mem],
        out_specs=tile,
        out_shape=jax.ShapeDtypeStruct((B, T, MEM_W), BF16),
        scratch_shapes=[gathered, gathered],
        compiler_params=_params(),
        name="mem_att",
    )(qm, mk, mv)


def _merge_kernel(x_ref, oa_ref, ob_ref, om_ref, g1_ref, wg_ref, wb_ref, wo_ref, y_ref):
    G, L, D = x_ref.shape
    R = G * L
    x = x_ref[...].reshape(R, D)
    hb = (x * lax.rsqrt(jnp.mean(x * x, axis=-1, keepdims=True) + RMS_EPS) * g1_ref[...]).astype(BF16)
    mix = None
    for n, o_ref in enumerate((oa_ref, ob_ref, om_ref)):
        br = jnp.dot(o_ref[...].reshape(R, RWKV_W), wb_ref[n], preferred_element_type=F32)
        gate = _sigmoid(jnp.dot(hb, wg_ref[:, n * D:(n + 1) * D], preferred_element_type=F32))
        mix = gate * br if mix is None else mix + gate * br
    y_ref[...] = (x + jnp.dot(mix.astype(BF16), wo_ref[...], preferred_element_type=F32)).reshape(G, L, D)


def _merge(x, oa, ob, om, g1, w_in_b, wb_b, wo_b, l, G, L):
    B, T, D = x.shape
    tile = lambda c: pl.BlockSpec((G, L, c), lambda i, j: (i, j, 0))
    return pl.pallas_call(
        _merge_kernel,
        grid=(B // G, T // L),
        in_specs=[tile(D), tile(RWKV_W), tile(RWKV_W), tile(MEM_W), _const_spec((1, D)),
                  pl.BlockSpec((None, D, N_BRANCH * D), lambda *_: (l, 0, C_GT // (N_BRANCH * D)),
                               pipeline_mode=pl.Buffered(1)),
                  _layer_spec((N_BRANCH, RWKV_W, D), l), _layer_spec((D, D), l)],
        out_specs=tile(D),
        out_shape=jax.ShapeDtypeStruct((B, T, D), F32),
        compiler_params=_params(),
        name="merge",
    )(x, oa, ob, om, g1, w_in_b, wb_b, wo_b)


def _ffn_kernel(x_ref, cin_ref, g2_ref, wu_ref, cw_ref, cb_ref, wd_ref, y_ref, cout_ref, carry):
    G, L, D = x_ref.shape
    R = G * L
    j = pl.program_id(1)

    @pl.when(j == 0)
    def _():
        carry[...] = cin_ref[...]

    x = x_ref[...].reshape(R, D)
    hb = (x * lax.rsqrt(jnp.mean(x * x, axis=-1, keepdims=True) + RMS_EPS) * g2_ref[...]).astype(BF16)
    row = lax.broadcasted_iota(jnp.int32, (L, 1), 0)
    acc = x
    for c0, c1 in FF_BLOCKS:
        cs = slice(c0, c1)
        a_in = jnp.dot(hb, wu_ref[:, cs], preferred_element_type=F32)
        u = jnp.dot(hb, wu_ref[:, D_FF + c0:D_FF + c1], preferred_element_type=F32)
        convs = []
        for g in range(G):
            a = a_in[g * L:(g + 1) * L]
            prev = carry[g, :, cs]
            a1 = jnp.where(row == 0, prev[1:2], pltpu.roll(a, 1, 0))
            a2 = jnp.where(row == 0, prev[0:1], jnp.where(row == 1, prev[1:2], pltpu.roll(a, 2, 0)))
            carry[g, :, cs] = a[L - 2:L]
            convs.append(cb_ref[:, cs] + a2 * cw_ref[0:1, cs] + a1 * cw_ref[1:2, cs] + a * cw_ref[2:3, cs])
        c = convs[0] if G == 1 else jnp.concatenate(convs, axis=0)
        gelu = 0.5 * c * (1.0 + jnp.tanh(0.7978845608028654 * (c + 0.044715 * (c * c * c))))
        acc = acc + jnp.dot((gelu * u).astype(BF16), wd_ref[cs, :], preferred_element_type=F32)
    y_ref[...] = acc.reshape(G, L, D)

    @pl.when(j == pl.num_programs(1) - 1)
    def _():
        cout_ref[...] = carry[...]


def _ffn(x, conv_in, g2, wu_b, cw, cb, wd_b, l, G, L):
    B, T, D = x.shape
    tile = pl.BlockSpec((G, L, D), lambda i, j: (i, j, 0))
    st = pl.BlockSpec((G, CONV_W - 1, D_FF), lambda i, j: (i, 0, 0))
    return pl.pallas_call(
        _ffn_kernel,
        grid=(B // G, T // L),
        in_specs=[tile, st, _const_spec((1, D)), _layer_spec((D, 2 * D_FF), l), _const_spec((CONV_W, D_FF)),
                  _const_spec((1, D_FF)), _layer_spec((D_FF, D), l)],
        out_specs=[tile, st],
        out_shape=[jax.ShapeDtypeStruct((B, T, D), F32), jax.ShapeDtypeStruct((B, CONV_W - 1, D_FF), F32)],
        scratch_shapes=[pltpu.VMEM((G, CONV_W - 1, D_FF), F32)],
        compiler_params=_params(),
        name="conv_ffn",
    )(x, conv_in, g2, wu_b, cw, cb, wd_b)


def _layer(x, lw, e, mk, mv, mem_layer, shift_in, s0, conv_in, halo_k, halo_v, tiles):
    has_cache = halo_k is not None
    p, q, k, v, qm = _in_proj(x, lw["norm1_g"], lw["w_in"], lw["layer"], lw["qn_g"], lw["kn_g"], lw["mqn_g"], e,
                              *tiles["dense"])
    oa, s_new = _rwkv(p, shift_in, s0, lw, e, *tiles["rwkv"])
    if has_cache:
        ob = _swa(q, k, v, halo_k, halo_v, lw["sink"], *tiles["att"], True)
    else:
        ob = _swa(q, k, v, k, v, lw["sink"], *tiles["att"], False)
    om = _mem_att(qm, mk, mv, mem_layer, *tiles["att"])
    x = _merge(x, oa, ob, om, lw["norm1_g"], lw["w_in"], lw["w_branch"], lw["w_out"], lw["layer"], *tiles["dense"])
    x, conv_new = _ffn(x, conv_in, lw["norm2_g"], lw["w_up"], lw["conv_w"], lw["conv_b"], lw["w_down"], lw["layer"],
                       *tiles["dense"])
    return x, (k, v, s_new, p[:, -1:, :], conv_new)


def kernel(x_prompt, x_sample, cache_swa_k, cache_swa_v, cache_mem_k, cache_mem_v, state_rwkv, state_shift, state_conv, mem_prompt, norm1_g, w_in, rwkv_mu, rwkv_w0, rwkv_w2, rwkv_a0, rwkv_a2, rwkv_g2, rwkv_kk, rwkv_ka, rwkv_rk, rwkv_ln_g, rwkv_ln_b, swa_qn_g, swa_kn_g, swa_sink, mem_norm_g, w_mem_kv, mem_qn_g, mem_kn_g, w_branch, w_out, norm2_g, w_up, conv_w, conv_b, w_down):
    Bp, Tp, _ = x_prompt.shape
    Bs, Ts, _ = x_sample.shape
    dt = x_prompt.dtype
    half = jnp.arange(LANES) // RWKV_HD
    e = (half[:, None] == half[None, :]).astype(BF16)
    row = lambda a: a.reshape(1, -1)
    zpad = jnp.zeros((LANES - 64, RWKV_W), dt)

    w_in_b, w_branch_b, w_out_b, w_up_b, w_down_b, w_mem_kv_b = (
        w.astype(BF16) for w in (w_in, w_branch, w_out, w_up, w_down, w_mem_kv))
    mem_k_rows = cache_mem_k.reshape(DEPTH, Bs, MEM_TOKENS * MEM_HEADS, MEM_HD)
    mem_v_rows = cache_mem_v.reshape(DEPTH, Bs, MEM_TOKENS * MEM_HEADS, MEM_HD)
    yp, ys = x_prompt, x_sample
    outs_p = [[] for _ in range(7)]
    outs_s = [[] for _ in range(5)]
    prompt_tiles = {"dense": (1, 512), "rwkv": (Bp, 256), "att": (1, 512)}
    sample_tiles = {"dense": (8, Ts), "rwkv": (8, Ts), "att": (8, Ts)}
    for l in range(DEPTH):
        lw = {
            "layer": l, "norm1_g": row(norm1_g[l]), "w_in": w_in_b,
            "qn_g": row(jnp.tile(swa_qn_g[l], 2)) * (SWA_HD ** -0.5), "kn_g": row(jnp.tile(swa_kn_g[l], 2)),
            "mqn_g": row(mem_qn_g[l]) * (MEM_HD ** -0.5),
            "mu": row(rwkv_mu[l]), "w0": row(rwkv_w0[l]),
            "w2p": jnp.concatenate([rwkv_w2[l], zpad], axis=0),
            "a0": row(rwkv_a0[l]),
            "a2p": jnp.concatenate([zpad, rwkv_a2[l]], axis=0),
            "g2": rwkv_g2[l], "kk": row(rwkv_kk[l]), "ka": row(rwkv_ka[l]), "rk": row(rwkv_rk[l]),
            "ln_g": row(rwkv_ln_g[l]), "ln_b": row(rwkv_ln_b[l]),
            "sink": swa_sink[l],
            "w_branch": w_branch_b, "w_out": w_out_b,
            "norm2_g": row(norm2_g[l]), "w_up": w_up_b, "conv_w": conv_w[l],
            "conv_b": row(conv_b[l]), "w_down": w_down_b,
        }
        mk, mv = _mem_kv(mem_prompt, row(mem_norm_g[l]), w_mem_kv_b, l, row(mem_kn_g[l]))
        yp, (k, v, s_new, sh_new, cv_new) = _layer(
            yp, lw, e, mk, mv, None,
            jnp.zeros((Bp, 1, RWKV_COLS), dt),
            jnp.zeros((Bp, RWKV_HEADS, RWKV_HD, RWKV_HD), dt),
            jnp.zeros((Bp, CONV_W - 1, D_FF), dt), None, None, prompt_tiles)
        kv_shape = (Bp, WINDOW, 2, SWA_HD)
        for lst, val in zip(outs_p, (k[:, -WINDOW:].reshape(kv_shape), v[:, -WINDOW:].reshape(kv_shape),
                                     mk.reshape(Bp, MEM_TOKENS, MEM_HEADS, MEM_HD),
                                     mv.reshape(Bp, MEM_TOKENS, MEM_HEADS, MEM_HD), s_new, sh_new, cv_new)):
            lst.append(val)
        ck = cache_swa_k[l].reshape(Bs, WINDOW, LANES)
        cv = cache_swa_v[l].reshape(Bs, WINDOW, LANES)
        ys, (k, v, s_new, sh_new, cv_new) = _layer(
            ys, lw, e, mem_k_rows, mem_v_rows, l,
            state_shift[l], state_rwkv[l], state_conv[l], ck, cv, sample_tiles)
        kv_shape = (Bs, WINDOW, 2, SWA_HD)
        kf = jnp.concatenate([ck, k], axis=1)[:, -WINDOW:].reshape(kv_shape)
        vf = jnp.concatenate([cv, v], axis=1)[:, -WINDOW:].reshape(kv_shape)
        for lst, val in zip(outs_s, (kf, vf, s_new, sh_new, cv_new)):
            lst.append(val)
    return (yp, ys) + tuple(jnp.stack(o) for o in outs_p) + tuple(jnp.stack(o) for o in outs_s)
```

```python
import functools
import math

import jax
import jax.numpy as jnp
from jax import lax
from jax.experimental import pallas as pl
from jax.experimental.pallas import tpu as pltpu

F32 = jnp.float32
BF16 = jnp.bfloat16
HIGHEST = lax.Precision.HIGHEST

D_MODEL = 1024
DEPTH = 2
CHUNK = 64
RWKV_HEADS = 8
RWKV_HD = 64
RWKV_W = 512
RWKV_COLS = 1792
GN_EPS = 64e-5
SWA_HEADS = 8
SWA_HD = 64
WINDOW = 128
MEM_TOKENS = 256
MEM_HEADS = 4
MEM_HD = 128
MEM_W = 512
N_BRANCH = 3
D_FF = 2816
CONV_W = 3
RMS_EPS = 1e-6

LANES = 128
PAIRS = RWKV_W // LANES
VMEM_LIMIT = 56 * 1024 * 1024
MXU_DIM = 256
FF_STEP = 6 * MXU_DIM
FF_BLOCKS = tuple((c, min(c + FF_STEP, D_FF)) for c in range(0, D_FF, FF_STEP))
UNROLL = 4
RWKV_GROUP = 4

C_Q = RWKV_COLS
C_K = C_Q + SWA_HEADS * SWA_HD
C_V = C_K + LANES
C_QM = C_V + LANES
C_GT = C_QM + MEM_W
IN_COLS = C_GT + N_BRANCH * D_MODEL


def _dot(a, b, dims=((1,), (0,)), exact=False):
    if exact:
        return lax.dot_general(a, b, (dims, ((), ())), precision=HIGHEST, preferred_element_type=F32)
    return lax.dot_general(a.astype(BF16), b.astype(BF16), (dims, ((), ())), preferred_element_type=F32)


_NT = ((1,), (1,))
_TN = ((0,), (0,))


def _seg_sum(x, e):
    return jnp.dot(x.astype(BF16), e, preferred_element_type=F32)


def _sigmoid(x):
    return 1.0 / (1.0 + jnp.exp(-x))


def _const_spec(shape):
    n = len(shape)
    return pl.BlockSpec(shape, lambda *_: (0,) * n, pipeline_mode=pl.Buffered(1))


def _layer_spec(shape, l):
    n = len(shape)
    return pl.BlockSpec((None,) + tuple(shape), lambda *_: (l,) + (0,) * n, pipeline_mode=pl.Buffered(1))


def _params():
    return pltpu.CompilerParams(dimension_semantics=("arbitrary", "arbitrary"), vmem_limit_bytes=VMEM_LIMIT)


def _in_kernel(x_ref, g1_ref, w_ref, qng_ref, kng_ref, mqg_ref, e_ref,
               p_ref, q_ref, k_ref, v_ref, qm_ref):
    G, L, D = x_ref.shape
    R = G * L
    x = x_ref[...].reshape(R, D)
    h = x * lax.rsqrt(jnp.mean(x * x, axis=-1, keepdims=True) + RMS_EPS) * g1_ref[...]
    hb = h.astype(BF16)
    e = e_ref[...]

    def proj(c0, c1):
        return jnp.dot(hb, w_ref[:, c0:c1], preferred_element_type=F32)

    def head_rms(z, gain):
        return z * lax.rsqrt(_seg_sum(z * z, e) * (1.0 / SWA_HD) + RMS_EPS) * gain

    p_ref[...] = proj(0, RWKV_COLS).reshape(G, L, RWKV_COLS)
    zq = proj(C_Q, C_K)
    zkv = proj(C_K, C_QM)
    zqm = proj(C_QM, C_GT)
    for s in range(SWA_HEADS * SWA_HD // LANES):
        sl = slice(s * LANES, (s + 1) * LANES)
        q_ref[:, :, sl] = head_rms(zq[:, sl], qng_ref[...]).astype(BF16).reshape(G, L, LANES)
    k_ref[...] = head_rms(zkv[:, :LANES], kng_ref[...]).reshape(G, L, LANES)
    v_ref[...] = zkv[:, LANES:].reshape(G, L, LANES)
    for s in range(MEM_HEADS):
        sl = slice(s * MEM_HD, (s + 1) * MEM_HD)
        z = zqm[:, sl]
        z = z * lax.rsqrt(jnp.mean(z * z, axis=-1, keepdims=True) + RMS_EPS) * mqg_ref[...]
        qm_ref[:, :, sl] = z.astype(BF16).reshape(G, L, MEM_HD)


def _in_proj(x, g1, w_in_b, l, qng, kng, mqg, e, G, L):
    B, T, D = x.shape
    tile = lambda c: pl.BlockSpec((G, L, c), lambda i, j: (i, j, 0))
    outs = ((RWKV_COLS, F32), (SWA_HEADS * SWA_HD, BF16), (LANES, F32), (LANES, F32), (MEM_W, BF16))
    return pl.pallas_call(
        _in_kernel,
        grid=(B // G, T // L),
        in_specs=[tile(D), _const_spec((1, D)), _layer_spec((D, C_GT), l), _const_spec((1, LANES)),
                  _const_spec((1, LANES)), _const_spec((1, MEM_HD)), _const_spec((LANES, LANES))],
        out_specs=[tile(c) for c, _ in outs],
        out_shape=[jax.ShapeDtypeStruct((B, T, c), dt) for c, dt in outs],
        compiler_params=_params(),
        name="in_proj",
    )(x, g1, w_in_b, qng, kng, mqg, e)


def _rwkv_kernel(p_ref, sh_ref, s0_ref, mu_ref, w0_ref, w2_ref, a0_ref, a2_ref, g2_ref, kk_ref, ka_ref,
                 rk_ref, lng_ref, lnb_ref, e_ref,
                 o_ref, sout_ref,
                 prev_scr, st_scr, r_scr, k_scr, v_scr, am_scr, b_scr, lw_scr, g_scr, o_scr,
                 wr_scr, uo_scr, bkt_scr, vb_scr, dcol_scr):
    G, L, _ = p_ref.shape
    C = CHUNK
    nc = L // C
    j = pl.program_id(1)
    e = e_ref[...]

    @pl.when(j == 0)
    def _():
        zero = jnp.zeros((RWKV_HD, RWKV_HD), F32)
        for g in range(G):
            for s in range(PAIRS):
                top = jnp.concatenate([s0_ref[g, 2 * s], zero], axis=1)
                bot = jnp.concatenate([zero, s0_ref[g, 2 * s + 1]], axis=1)
                st_scr[g, s] = jnp.concatenate([top, bot], axis=0).T
        prev_scr[...] = sh_ref[...]

    first_row = lax.broadcasted_iota(jnp.int32, (L, 1), 0) == 0

    def per_token(g):
        p = p_ref[g]
        shifted = jnp.where(first_row, prev_scr[g], pltpu.roll(p, 1, 0))
        pm = p + (shifted - p) * mu_ref[...]
        prev_scr[g] = p[L - 1:L, :]
        rows = slice(g * L, (g + 1) * L)
        r = pm[:, 0:RWKV_W]
        k = pm[:, RWKV_W:2 * RWKV_W]
        v = pm[:, 2 * RWKV_W:3 * RWKV_W]
        xwa = pm[:, 3 * RWKV_W:3 * RWKV_W + LANES]
        xg = pm[:, 3 * RWKV_W + LANES:RWKV_COLS]
        z = w0_ref[...] + _dot(jnp.tanh(xwa), w2_ref[...])
        a = _sigmoid(a0_ref[...] + _dot(xwa, a2_ref[...]))
        kkv = k * kk_ref[...]
        for s in range(PAIRS):
            sl = slice(s * LANES, (s + 1) * LANES)
            kks = kkv[:, sl]
            kkn = kks * lax.rsqrt(jnp.maximum(_seg_sum(kks * kks, e), 1e-24))
            am_scr[rows, sl] = -kkn
            b_scr[rows, sl] = kkn * a[:, sl]
        r_scr[rows, :] = r
        k_scr[rows, :] = k * (1.0 + (a - 1.0) * ka_ref[...])
        v_scr[rows, :] = v
        lw_scr[rows, :] = (-math.exp(-0.5)) * _sigmoid(z)
        g_scr[rows, :] = _dot(_sigmoid(xg), g2_ref[...])

    ri = lax.broadcasted_iota(jnp.int32, (C, C), 0)
    ci = lax.broadcasted_iota(jnp.int32, (C, C), 1)
    cumsum_mat = (ri >= ci).astype(BF16)
    rq = lax.broadcasted_iota(jnp.int32, (C, 2 * C), 0)
    cq = lax.broadcasted_iota(jnp.int32, (C, 2 * C), 1)
    strict_l = jnp.logical_and(cq < C, rq > cq).astype(F32)
    strict_r = jnp.logical_and(cq >= C, rq > cq - C).astype(F32)
    incl = (rq >= cq % C).astype(F32)
    incl_l = jnp.logical_and(cq < C, rq >= cq).astype(F32)
    incl_r = jnp.logical_and(cq >= C, rq >= cq - C).astype(F32)
    r2 = lax.broadcasted_iota(jnp.int32, (2 * C, 2 * C), 0)
    c2 = lax.broadcasted_iota(jnp.int32, (2 * C, 2 * C), 1)
    bd_mask = ((r2 // C) == (c2 // C)).astype(F32)
    lo = lax.broadcasted_iota(jnp.int32, (1, LANES), 1) < RWKV_HD

    def stack(z):
        return jnp.concatenate([jnp.where(lo, z, 0.0), jnp.where(lo, 0.0, z)], axis=0)

    def stack_other(z):
        return jnp.concatenate([jnp.where(lo, 0.0, z), jnp.where(lo, z, 0.0)], axis=0)

    def prepare(chunks):
        units = []
        for i in chunks:
            rows = slice(i * C, (i + 1) * C)
            lw = lw_scr[rows, :]
            lw_hi = lw.astype(BF16)
            lw_lo = (lw - lw_hi.astype(F32)).astype(BF16)
            cum = (jnp.dot(cumsum_mat, lw_hi, preferred_element_type=F32)
                   + jnp.dot(cumsum_mat, lw_lo, preferred_element_type=F32))
            cum_end = cum[C - 1:C, :]
            inv = jnp.exp(-cum)
            dec_rest = jnp.exp(cum_end - cum)
            dec_end = jnp.exp(cum_end)
            kc = k_scr[rows, :]
            bc = b_scr[rows, :]
            r_t = r_scr[rows, :] * jnp.exp(cum)
            a_t = am_scr[rows, :] * jnp.exp(cum - lw)
            b_t = bc * inv
            k_t = kc * inv
            b_e = bc * dec_rest
            k_e = kc * dec_rest
            vc = v_scr[rows, :]
            for s in range(PAIRS):
                sl = slice(s * LANES, (s + 1) * LANES)
                bkt_scr[i, s] = jnp.concatenate([b_e[:, sl], k_e[:, sl]], axis=0).T.astype(BF16)
                vb_scr[i, s] = vc[:, sl].astype(BF16)
                dcol_scr[i, s] = jnp.broadcast_to(dec_end[:, sl], (LANES, LANES)).T
                units.append((i, s, a_t[:, sl], r_t[:, sl], b_t[:, sl], k_t[:, sl], vc[:, sl]))
        N = range(len(units))
        lhs = [jnp.concatenate([u[2], u[3]], axis=0).astype(BF16) for u in units]
        nbk = [_dot(lhs[n], jnp.concatenate([stack(units[n][4]), stack(units[n][5])], axis=0), _NT) for n in N]
        nb = [z[:, :LANES] for z in nbk]
        nk = [z[:, LANES:] for z in nbk]
        m = [jnp.concatenate([nb[n][0:C] * strict_l, nb[n][0:C] * strict_r], axis=0) for n in N]
        mk = [jnp.concatenate([nk[n][0:C] * strict_l, nk[n][0:C] * strict_r], axis=0) for n in N]
        vs = [stack(units[n][6]).astype(BF16) for n in N]
        vsw = [stack_other(pltpu.roll(units[n][6], RWKV_HD, 1)).astype(BF16) for n in N]
        y = [stack(units[n][2]) + _dot(mk[n], vsw[n]) for n in N]
        for it in range(6):
            if it < 5:
                my = [_dot(m[n], jnp.concatenate([m[n], y[n]], axis=1)) for n in N]
                y = [y[n] + my[n][:, LANES:] for n in N]
                m = [my[n][:, :LANES] for n in N]
            else:
                my = [_dot(m[n], y[n]) for n in N]
                y = [y[n] + my[n] for n in N]
        arb = [jnp.concatenate([nb[n][C:2 * C] * incl_l, nb[n][C:2 * C] * incl_r], axis=0) for n in N]
        ay = [_dot(arb[n], y[n]) for n in N]
        akv = [_dot(nk[n][C:2 * C] * incl, vs[n]) for n in N]
        for n in N:
            i, s = units[n][0], units[n][1]
            wf = jnp.where(lo, y[n][0:C], y[n][C:2 * C])
            rw = units[n][3] + jnp.where(lo, ay[n][0:C], ay[n][C:2 * C])
            swapped = jnp.concatenate([jnp.where(lo, y[n][C:2 * C], y[n][0:C]),
                                       jnp.where(lo, ay[n][C:2 * C], ay[n][0:C])], axis=0)
            uo = pltpu.roll(swapped, RWKV_HD, 1)
            wr_scr[i, s] = jnp.concatenate([wf, rw], axis=0).astype(BF16)
            uo_scr[i, s] = jnp.concatenate([uo[0:C], uo[C:2 * C] + akv[n]], axis=0)

    assert (G * nc) % RWKV_GROUP == 0
    seen = set()
    for c0 in range(0, G * nc, RWKV_GROUP):
        chunks = range(c0, c0 + RWKV_GROUP)
        for g in sorted({i // nc for i in chunks} - seen):
            per_token(g)
            seen.add(g)
        prepare(chunks)

    def advance(c):
        units = [(g, s, g * nc + c) for g in range(G) for s in range(PAIRS)]
        hs = [st_scr[g, s] for g, s, _ in units]
        ys = [jnp.dot(wr_scr[i, s], h.astype(BF16), preferred_element_type=F32) + uo_scr[i, s]
              for (g, s, i), h in zip(units, hs)]
        uvb = [jnp.concatenate([y[0:C].astype(BF16), vb_scr[i, s]], axis=0) for (g, s, i), y in zip(units, ys)]
        upd = [jnp.dot(bkt_scr[i, s], z, preferred_element_type=F32) for (g, s, i), z in zip(units, uvb)]
        for (g, s, i), h, y, up in zip(units, hs, ys, upd):
            st_scr[g, s] = h * dcol_scr[i, s] + up * bd_mask
            o_scr[i * C:(i + 1) * C, s * LANES:(s + 1) * LANES] = y[C:2 * C]

    for c in range(nc):
        advance(c)

    for s in range(PAIRS):
        sl = slice(s * LANES, (s + 1) * LANES)
        o = o_scr[:, sl]
        d = o - _seg_sum(o, e) * (1.0 / RWKV_HD)
        var = _seg_sum(d * d, e) * (1.0 / RWKV_HD)
        y = d * lax.rsqrt(var + GN_EPS) * lng_ref[:, sl] + lnb_ref[:, sl]
        bonus = _seg_sum(r_scr[:, sl] * k_scr[:, sl] * rk_ref[:, sl], e) * v_scr[:, sl]
        o_ref[:, :, sl] = ((y + bonus) * g_scr[:, sl]).astype(BF16).reshape(G, L, LANES)

    @pl.when(j == pl.num_programs(1) - 1)
    def _():
        for g in range(G):
            for s in range(PAIRS):
                sp = st_scr[g, s].T
                sout_ref[g, 2 * s] = sp[0:RWKV_HD, 0:RWKV_HD]
                sout_ref[g, 2 * s + 1] = sp[RWKV_HD:, RWKV_HD:]


def _rwkv(p, shift_in, s0, lw, e, G, L):
    B, T, _ = p.shape
    R = G * L
    vec = lambda c: _const_spec((1, c))
    buf = lambda: pltpu.VMEM((R, RWKV_W), F32)
    per_chunk = lambda rows, dt: pltpu.VMEM((R // CHUNK, PAIRS, rows, LANES), dt)
    state = pl.BlockSpec((G, RWKV_HEADS, RWKV_HD, RWKV_HD), lambda i, j: (i, 0, 0, 0))
    return pl.pallas_call(
        _rwkv_kernel,
        grid=(B // G, T // L),
        in_specs=[pl.BlockSpec((G, L, RWKV_COLS), lambda i, j: (i, j, 0)),
                  pl.BlockSpec((G, 1, RWKV_COLS), lambda i, j: (i, 0, 0)),
                  state,
                  vec(RWKV_COLS), vec(RWKV_W), _const_spec((LANES, RWKV_W)), vec(RWKV_W),
                  _const_spec((LANES, RWKV_W)), _const_spec((LANES, RWKV_W)), vec(RWKV_W), vec(RWKV_W),
                  vec(RWKV_W), vec(RWKV_W), vec(RWKV_W), _const_spec((LANES, LANES))],
        out_specs=[pl.BlockSpec((G, L, RWKV_W), lambda i, j: (i, j, 0)),
                   state],
        out_shape=[jax.ShapeDtypeStruct((B, T, RWKV_W), BF16),
                   jax.ShapeDtypeStruct((B, RWKV_HEADS, RWKV_HD, RWKV_HD), F32)],
        scratch_shapes=[pltpu.VMEM((G, 1, RWKV_COLS), F32), pltpu.VMEM((G, PAIRS, LANES, LANES), F32)]
                       + [buf() for _ in range(8)]
                       + [per_chunk(LANES, BF16), per_chunk(LANES, F32), per_chunk(LANES, BF16),
                          per_chunk(CHUNK, BF16), per_chunk(LANES, F32)],
        compiler_params=_params(),
        name="rwkv7",
    )(p, shift_in, s0, lw["mu"], lw["w0"], lw["w2p"], lw["a0"], lw["a2p"], lw["g2"], lw["kk"],
      lw["ka"], lw["rk"], lw["ln_g"], lw["ln_b"], e)


def _swa_kernel(sink_ref, q_ref, k_ref, v_ref, hk_ref, hv_ref, o_ref, kd, vd, *, has_cache):
    G, L, _ = q_ref.shape
    C = CHUNK
    nc = L // C
    KB = WINDOW + C
    GROUP = 4
    j = pl.program_id(1)
    lo = lax.broadcasted_iota(jnp.int32, (1, LANES), 1) < SWA_HD
    qi = lax.broadcasted_iota(jnp.int32, (C, KB), 0)
    kj = lax.broadcasted_iota(jnp.int32, (C, KB), 1)
    dist = jnp.abs(WINDOW + qi - kj).astype(F32)
    kj4 = lax.broadcasted_iota(jnp.int32, (GROUP * C, KB), 1)
    ones = jnp.ones((WINDOW + L, LANES), F32)
    neg_pad = jnp.full((GROUP * C, 2 * LANES - KB), -jnp.inf, F32)
    bias = [jnp.concatenate([(2.0 ** -(GROUP * kv + h + 1)) * dist for h in range(GROUP)], axis=0)
            for kv in range(2)]
    sink = [jnp.concatenate([jnp.full((C, LANES), sink_ref[GROUP * kv + h], F32) for h in range(GROUP)], axis=0)
            for kv in range(2)]
    for g in range(G):
        kc = jnp.concatenate([hk_ref[g], k_ref[g]], axis=0)
        vc = jnp.concatenate([hv_ref[g], v_ref[g]], axis=0)
        ks = pltpu.roll(kc, SWA_HD, 1)
        vs = pltpu.roll(vc, SWA_HD, 1)
        kd[g, 0] = jnp.where(lo, kc, ks).astype(BF16)
        kd[g, 1] = jnp.where(lo, ks, kc).astype(BF16)
        vd[g, 0] = jnp.concatenate([jnp.where(lo, vc, vs), ones], axis=1).astype(BF16)
        vd[g, 1] = jnp.concatenate([jnp.where(lo, vs, vc), ones], axis=1).astype(BF16)

    def chunks(t, carry, mask_halo):
        units = []
        for n in range(UNROLL):
            i = t * UNROLL + n
            for kv in range(2):
                units.append((i // nc, pl.multiple_of((i % nc) * C, C), kv))
        scs = []
        for g, off, kv in units:
            q0 = q_ref[g, pl.ds(off, C), (2 * kv) * LANES:(2 * kv + 1) * LANES]
            q1 = q_ref[g, pl.ds(off, C), (2 * kv + 1) * LANES:(2 * kv + 2) * LANES]
            zero = jnp.zeros_like(q0)
            lhs = jnp.concatenate([jnp.where(lo, q0, zero), jnp.where(lo, zero, q0),
                                   jnp.where(lo, q1, zero), jnp.where(lo, zero, q1)], axis=0)
            sc = lax.dot_general(lhs, kd[g, kv, pl.ds(off, KB), :], ((_NT), ((), ())),
                                 preferred_element_type=F32) - bias[kv]
            if mask_halo:
                sc = jnp.where(kj4 + off < WINDOW, -jnp.inf, sc)
            scs.append(sc)
        exs, sink_terms = [], []
        for (g, off, kv), sc in zip(units, scs):
            folded = jnp.maximum(sc[:, :LANES], jnp.concatenate([sc[:, LANES:], neg_pad], axis=1))
            mx = jnp.maximum(jnp.broadcast_to(jnp.max(folded, axis=-1, keepdims=True), (GROUP * C, LANES)),
                             sink[kv])
            ex = jnp.concatenate([jnp.exp(sc[:, :LANES] - mx), jnp.exp(sc[:, LANES:] - mx[:, :KB - LANES])],
                                 axis=1)
            exs.append(ex.astype(BF16))
            sink_terms.append(jnp.exp(sink[kv] - mx))
        for (g, off, kv), ex, st in zip(units, exs, sink_terms):
            pvd = jnp.dot(ex, vd[g, kv, pl.ds(off, KB), :], preferred_element_type=F32)
            pv = pvd[:, :LANES] / (pvd[:, LANES:] + st)
            pv = pv.astype(BF16)
            o_ref[g, pl.ds(off, C), (2 * kv) * LANES:(2 * kv + 1) * LANES] = jnp.where(lo, pv[0:C], pv[C:2 * C])
            o_ref[g, pl.ds(off, C), (2 * kv + 1) * LANES:(2 * kv + 2) * LANES] = jnp.where(
                lo, pv[2 * C:3 * C], pv[3 * C:4 * C])
        return carry

    steps = G * nc // UNROLL
    if has_cache:
        lax.fori_loop(0, steps, functools.partial(chunks, mask_halo=False), 0)
    else:
        @pl.when(j == 0)
        def _():
            lax.fori_loop(0, steps, functools.partial(chunks, mask_halo=True), 0)

        @pl.when(j != 0)
        def _():
            lax.fori_loop(0, steps, functools.partial(chunks, mask_halo=False), 0)


def _swa(q, k, v, halo_k, halo_v, sink, G, L, has_cache):
    B, T, _ = q.shape
    if has_cache:
        halo = pl.BlockSpec((G, WINDOW, LANES), lambda i, j: (i, 0, 0))
    else:
        per = L // WINDOW
        halo = pl.BlockSpec((G, WINDOW, LANES), lambda i, j: (i, jnp.maximum(j * per - 1, 0), 0))
    tile = lambda c: pl.BlockSpec((G, L, c), lambda i, j: (i, j, 0))
    cat = lambda cols: pltpu.VMEM((G, 2, WINDOW + L, cols), BF16)
    return pl.pallas_call(
        functools.partial(_swa_kernel, has_cache=has_cache),
        grid=(B // G, T // L),
        in_specs=[pl.BlockSpec(memory_space=pltpu.SMEM), tile(SWA_HEADS * SWA_HD), tile(LANES), tile(LANES),
                  halo, halo],
        out_specs=tile(SWA_HEADS * SWA_HD),
        out_shape=jax.ShapeDtypeStruct((B, T, SWA_HEADS * SWA_HD), BF16),
        scratch_shapes=[cat(LANES), cat(2 * LANES)],
        compiler_params=_params(),
        name="swa",
    )(sink, q, k, v, halo_k, halo_v)


def _mem_kv_kernel(m_ref, g_ref, w_ref, kng_ref, mk_ref, mv_ref):
    G, M, D = m_ref.shape
    x = m_ref[...].reshape(G * M, D)
    h = (x * lax.rsqrt(jnp.mean(x * x, axis=-1, keepdims=True) + RMS_EPS) * g_ref[...]).astype(BF16)
    for s in range(MEM_HEADS):
        z = jnp.dot(h, w_ref[:, s * MEM_HD:(s + 1) * MEM_HD], preferred_element_type=F32)
        z = z * lax.rsqrt(jnp.mean(z * z, axis=-1, keepdims=True) + RMS_EPS) * kng_ref[...]
        zv = jnp.dot(h, w_ref[:, MEM_W + s * MEM_HD:MEM_W + (s + 1) * MEM_HD], preferred_element_type=F32)
        for g in range(G):
            mk_ref[g, pl.ds(s, M, stride=MEM_HEADS), :] = z[g * M:(g + 1) * M]
            mv_ref[g, pl.ds(s, M, stride=MEM_HEADS), :] = zv[g * M:(g + 1) * M]


def _mem_kv(mem, g, w_b, l, kng):
    B, M, D = mem.shape
    heads = pl.BlockSpec((1, M * MEM_HEADS, MEM_HD), lambda i, j: (i, 0, 0))
    return pl.pallas_call(
        _mem_kv_kernel,
        grid=(B, 1),
        in_specs=[pl.BlockSpec((1, M, D), lambda i, j: (i, 0, 0)), _const_spec((1, D)),
                  _layer_spec((D, 2 * MEM_W), l), _const_spec((1, MEM_HD))],
        out_specs=[heads, heads],
        out_shape=[jax.ShapeDtypeStruct((B, M * MEM_HEADS, MEM_HD), F32)] * 2,
        compiler_params=_params(),
        name="mem_kv",
    )(mem, g, w_b, kng)


def _mem_att_kernel(q_ref, mk_ref, mv_ref, o_ref, kb, vb):
    G, L, _ = q_ref.shape
    units = [(g, h) for g in range(G) for h in range(MEM_HEADS)]
    sl = lambda h: slice(h * MEM_HD, (h + 1) * MEM_HD)

    @pl.when(pl.program_id(1) == 0)
    def _():
        for g, h in units:
            kb[g, h] = mk_ref[g, pl.ds(h, MEM_TOKENS, stride=MEM_HEADS), :].astype(BF16)
            vb[g, h] = mv_ref[g, pl.ds(h, MEM_TOKENS, stride=MEM_HEADS), :].astype(BF16)

    ones = jnp.ones((MEM_TOKENS, MEM_HD), BF16)
    scs = [lax.dot_general(q_ref[g, :, sl(h)], kb[g, h], (_NT, ((), ())), preferred_element_type=F32)
           for g, h in units]
    exs = [jnp.exp(sc - jnp.max(sc, axis=-1, keepdims=True)).astype(BF16) for sc in scs]
    for (g, h), ex in zip(units, exs):
        den = jnp.dot(ex, ones, preferred_element_type=F32)
        pv = jnp.dot(ex, vb[g, h], preferred_element_type=F32)
        o_ref[g, :, sl(h)] = (pv / den).astype(BF16)


def _mem_att(qm, mk, mv, l, G, L):
    B, T, _ = qm.shape
    tile = pl.BlockSpec((G, L, MEM_W), lambda i, j: (i, j, 0))
    if l is None:
        mem = pl.BlockSpec((G, MEM_TOKENS * MEM_HEADS, MEM_HD), lambda i, j: (i, 0, 0))
    else:
        mem = pl.BlockSpec((None, G, MEM_TOKENS * MEM_HEADS, MEM_HD), lambda i, j: (l, i, 0, 0))
    gathered = pltpu.VMEM((G, MEM_HEADS, MEM_TOKENS, MEM_HD), BF16)
    return pl.pallas_call(
        _mem_att_kernel,
        grid=(B // G, T // L),
        in_specs=[tile, mem, mem],
        out_specs=tile,
        out_shape=jax.ShapeDtypeStruct((B, T, MEM_W), BF16),
        scratch_shapes=[gathered, gathered],
        compiler_params=_params(),
        name="mem_att",
    )(qm, mk, mv)


def _merge_kernel(x_ref, oa_ref, ob_ref, om_ref, g1_ref, wg_ref, wb_ref, wo_ref, y_ref):
    G, L, D = x_ref.shape
    R = G * L
    x = x_ref[...].reshape(R, D)
    hb = (x * lax.rsqrt(jnp.mean(x * x, axis=-1, keepdims=True) + RMS_EPS) * g1_ref[...]).astype(BF16)
    mix = None
    for n, o_ref in enumerate((oa_ref, ob_ref, om_ref)):
        br = jnp.dot(o_ref[...].reshape(R, RWKV_W), wb_ref[n], preferred_element_type=F32)
        gate = _sigmoid(jnp.dot(hb, wg_ref[:, n * D:(n + 1) * D], preferred_element_type=F32))
        mix = gate * br if mix is None else mix + gate * br
    y_ref[...] = (x + jnp.dot(mix.astype(BF16), wo_ref[...], preferred_element_type=F32)).reshape(G, L, D)


def _merge(x, oa, ob, om, g1, w_in_b, wb_b, wo_b, l, G, L):
    B, T, D = x.shape
    tile = lambda c: pl.BlockSpec((G, L, c), lambda i, j: (i, j, 0))
    return pl.pallas_call(
        _merge_kernel,
        grid=(B // G, T // L),
        in_specs=[tile(D), tile(RWKV_W), tile(RWKV_W), tile(MEM_W), _const_spec((1, D)),
                  pl.BlockSpec((None, D, N_BRANCH * D), lambda *_: (l, 0, C_GT // (N_BRANCH * D)),
                               pipeline_mode=pl.Buffered(1)),
                  _layer_spec((N_BRANCH, RWKV_W, D), l), _layer_spec((D, D), l)],
        out_specs=tile(D),
        out_shape=jax.ShapeDtypeStruct((B, T, D), F32),
        compiler_params=_params(),
        name="merge",
    )(x, oa, ob, om, g1, w_in_b, wb_b, wo_b)


def _mix_kernel(sink_ref, x_ref, oa_ref, q_ref, k_ref, v_ref, hk_ref, hv_ref, qm_ref, mk_ref, mv_ref,
                g1_ref, wg_ref, wb_ref, wo_ref, y_ref, kd, vd, kb, vb, ob_scr, om_scr, *, has_cache):
    G, L, D = x_ref.shape
    R = G * L
    C = CHUNK
    nc = L // C
    KB = WINDOW + C
    GROUP = 4
    j = pl.program_id(1)
    lo = lax.broadcasted_iota(jnp.int32, (1, LANES), 1) < SWA_HD
    qi = lax.broadcasted_iota(jnp.int32, (C, KB), 0)
    kj = lax.broadcasted_iota(jnp.int32, (C, KB), 1)
    dist = jnp.abs(WINDOW + qi - kj).astype(F32)
    kj4 = lax.broadcasted_iota(jnp.int32, (GROUP * C, KB), 1)
    ones = jnp.ones((WINDOW + L, LANES), F32)
    neg_pad = jnp.full((GROUP * C, 2 * LANES - KB), -jnp.inf, F32)
    bias = [jnp.concatenate([(2.0 ** -(GROUP * kv + h + 1)) * dist for h in range(GROUP)], axis=0)
            for kv in range(2)]
    sink = [jnp.concatenate([jnp.full((C, LANES), sink_ref[GROUP * kv + h], F32) for h in range(GROUP)], axis=0)
            for kv in range(2)]
    for g in range(G):
        kc = jnp.concatenate([hk_ref[g], k_ref[g]], axis=0)
        vc = jnp.concatenate([hv_ref[g], v_ref[g]], axis=0)
        ks = pltpu.roll(kc, SWA_HD, 1)
        vs = pltpu.roll(vc, SWA_HD, 1)
        kd[g, 0] = jnp.where(lo, kc, ks).astype(BF16)
        kd[g, 1] = jnp.where(lo, ks, kc).astype(BF16)
        vd[g, 0] = jnp.concatenate([jnp.where(lo, vc, vs), ones], axis=1).astype(BF16)
        vd[g, 1] = jnp.concatenate([jnp.where(lo, vs, vc), ones], axis=1).astype(BF16)

    mem_units = [(g, h) for g in range(G) for h in range(MEM_HEADS)]
    msl = lambda h: slice(h * MEM_HD, (h + 1) * MEM_HD)

    @pl.when(j == 0)
    def _():
        for g, h in mem_units:
            kb[g, h] = mk_ref[g, pl.ds(h, MEM_TOKENS, stride=MEM_HEADS), :].astype(BF16)
            vb[g, h] = mv_ref[g, pl.ds(h, MEM_TOKENS, stride=MEM_HEADS), :].astype(BF16)

    x = x_ref[...].reshape(R, D)
    hb = (x * lax.rsqrt(jnp.mean(x * x, axis=-1, keepdims=True) + RMS_EPS) * g1_ref[...]).astype(BF16)
    todo = list(range(N_BRANCH))
    gates = []

    def next_gate():
        if todo:
            n = todo.pop(0)
            gates.append(jnp.dot(hb, wg_ref[:, n * D:(n + 1) * D], preferred_element_type=F32))

    def swa_scores(chunk_ids):
        units = [(i // nc, (i % nc) * C, kv) for i in chunk_ids for kv in range(2)]
        scs = []
        for g, off, kv in units:
            q0 = q_ref[g, off:off + C, (2 * kv) * LANES:(2 * kv + 1) * LANES]
            q1 = q_ref[g, off:off + C, (2 * kv + 1) * LANES:(2 * kv + 2) * LANES]
            zero = jnp.zeros_like(q0)
            lhs = jnp.concatenate([jnp.where(lo, q0, zero), jnp.where(lo, zero, q0),
                                   jnp.where(lo, q1, zero), jnp.where(lo, zero, q1)], axis=0)
            sc = lax.dot_general(lhs, kd[g, kv, off:off + KB, :], ((_NT), ((), ())),
                                 preferred_element_type=F32) - bias[kv]
            if not has_cache and off < WINDOW:
                sc = jnp.where(jnp.logical_and(j == 0, kj4 + off < WINDOW), -jnp.inf, sc)
            scs.append(sc)
        return units, scs

    def swa_finish(units, scs):
        exs, sink_terms = [], []
        for (g, off, kv), sc in zip(units, scs):
            folded = jnp.maximum(sc[:, :LANES], jnp.concatenate([sc[:, LANES:], neg_pad], axis=1))
            mx = jnp.maximum(jnp.broadcast_to(jnp.max(folded, axis=-1, keepdims=True), (GROUP * C, LANES)),
                             sink[kv])
            ex = jnp.concatenate([jnp.exp(sc[:, :LANES] - mx), jnp.exp(sc[:, LANES:] - mx[:, :KB - LANES])],
                                 axis=1)
            exs.append(ex.astype(BF16))
            sink_terms.append(jnp.exp(sink[kv] - mx))
        for (g, off, kv), ex, st in zip(units, exs, sink_terms):
            pvd = jnp.dot(ex, vd[g, kv, off:off + KB, :], preferred_element_type=F32)
            pv = (pvd[:, :LANES] / (pvd[:, LANES:] + st)).astype(BF16)
            ob_scr[g, off:off + C, (2 * kv) * LANES:(2 * kv + 1) * LANES] = jnp.where(lo, pv[0:C], pv[C:2 * C])
            ob_scr[g, off:off + C, (2 * kv + 1) * LANES:(2 * kv + 2) * LANES] = jnp.where(
                lo, pv[2 * C:3 * C], pv[3 * C:4 * C])

    for c0 in range(0, G * nc, UNROLL):
        units, scs = swa_scores(range(c0, c0 + UNROLL))
        next_gate()
        swa_finish(units, scs)

    mones = jnp.ones((MEM_TOKENS, MEM_HD), BF16)
    mscs = [lax.dot_general(qm_ref[g, :, msl(h)], kb[g, h], (_NT, ((), ())), preferred_element_type=F32)
            for g, h in mem_units]
    while todo:
        next_gate()
    mexs = [jnp.exp(sc - jnp.max(sc, axis=-1, keepdims=True)).astype(BF16) for sc in mscs]
    for (g, h), ex in zip(mem_units, mexs):
        den = jnp.dot(ex, mones, preferred_element_type=F32)
        pv = jnp.dot(ex, vb[g, h], preferred_element_type=F32)
        om_scr[g, :, msl(h)] = (pv / den).astype(BF16)

    mix = None
    for n, o_ref in enumerate((oa_ref, ob_scr, om_scr)):
        br = jnp.dot(o_ref[...].reshape(R, RWKV_W), wb_ref[n], preferred_element_type=F32)
        t = _sigmoid(gates[n]) * br
        mix = t if mix is None else mix + t
    y_ref[...] = (x + jnp.dot(mix.astype(BF16), wo_ref[...], preferred_element_type=F32)).reshape(G, L, D)


def _mix(x, oa, q, k, v, halo_k, halo_v, sink, qm, mk, mv, mem_layer, g1, w_in_b, wb_b, wo_b, l, G, L, has_cache):
    B, T, D = x.shape
    if has_cache:
        halo = pl.BlockSpec((G, WINDOW, LANES), lambda i, j: (i, 0, 0))
    else:
        per = L // WINDOW
        halo = pl.BlockSpec((G, WINDOW, LANES), lambda i, j: (i, jnp.maximum(j * per - 1, 0), 0))
    tile = lambda c: pl.BlockSpec((G, L, c), lambda i, j: (i, j, 0))
    if mem_layer is None:
        mem = pl.BlockSpec((G, MEM_TOKENS * MEM_HEADS, MEM_HD), lambda i, j: (i, 0, 0))
    else:
        mem = pl.BlockSpec((None, G, MEM_TOKENS * MEM_HEADS, MEM_HD), lambda i, j: (mem_layer, i, 0, 0))
    cat = lambda cols: pltpu.VMEM((G, 2, WINDOW + L, cols), BF16)
    gathered = pltpu.VMEM((G, MEM_HEADS, MEM_TOKENS, MEM_HD), BF16)
    branch = pltpu.VMEM((G, L, RWKV_W), BF16)
    return pl.pallas_call(
        functools.partial(_mix_kernel, has_cache=has_cache),
        grid=(B // G, T // L),
        in_specs=[pl.BlockSpec(memory_space=pltpu.SMEM), tile(D), tile(RWKV_W), tile(SWA_HEADS * SWA_HD),
                  tile(LANES), tile(LANES), halo, halo, tile(MEM_W), mem, mem, _const_spec((1, D)),
                  pl.BlockSpec((None, D, N_BRANCH * D), lambda *_: (l, 0, C_GT // (N_BRANCH * D)),
                               pipeline_mode=pl.Buffered(1)),
                  _layer_spec((N_BRANCH, RWKV_W, D), l), _layer_spec((D, D), l)],
        out_specs=tile(D),
        out_shape=jax.ShapeDtypeStruct((B, T, D), F32),
        scratch_shapes=[cat(LANES), cat(2 * LANES), gathered, gathered, branch, branch],
        compiler_params=_params(),
        name="mix",
    )(sink, x, oa, q, k, v, halo_k, halo_v, qm, mk, mv, g1, w_in_b, wb_b, wo_b)


def _ffn_kernel(x_ref, cin_ref, g2_ref, wu_ref, cw_ref, cb_ref, wd_ref, y_ref, cout_ref, carry):
    G, L, D = x_ref.shape
    R = G * L
    j = pl.program_id(1)

    @pl.when(j == 0)
    def _():
        carry[...] = cin_ref[...]

    x = x_ref[...].reshape(R, D)
    hb = (x * lax.rsqrt(jnp.mean(x * x, axis=-1, keepdims=True) + RMS_EPS) * g2_ref[...]).astype(BF16)
    row = lax.broadcasted_iota(jnp.int32, (L, 1), 0)
    acc = x
    for c0, c1 in FF_BLOCKS:
        cs = slice(c0, c1)
        a_in = jnp.dot(hb, wu_ref[:, cs], preferred_element_type=F32)
        u = jnp.dot(hb, wu_ref[:, D_FF + c0:D_FF + c1], preferred_element_type=F32)
        convs = []
        for g in range(G):
            a = a_in[g * L:(g + 1) * L]
            prev = carry[g, :, cs]
            a1 = jnp.where(row == 0, prev[1:2], pltpu.roll(a, 1, 0))
            a2 = jnp.where(row == 0, prev[0:1], jnp.where(row == 1, prev[1:2], pltpu.roll(a, 2, 0)))
            carry[g, :, cs] = a[L - 2:L]
            convs.append(cb_ref[:, cs] + a2 * cw_ref[0:1, cs] + a1 * cw_ref[1:2, cs] + a * cw_ref[2:3, cs])
        c = convs[0] if G == 1 else jnp.concatenate(convs, axis=0)
        gelu = 0.5 * c * (1.0 + jnp.tanh(0.7978845608028654 * (c + 0.044715 * (c * c * c))))
        acc = acc + jnp.dot((gelu * u).astype(BF16), wd_ref[cs, :], preferred_element_type=F32)
    y_ref[...] = acc.reshape(G, L, D)

    @pl.when(j == pl.num_programs(1) - 1)
    def _():
        cout_ref[...] = carry[...]


def _ffn(x, conv_in, g2, wu_b, cw, cb, wd_b, l, G, L):
    B, T, D = x.shape
    tile = pl.BlockSpec((G, L, D), lambda i, j: (i, j, 0))
    st = pl.BlockSpec((G, CONV_W - 1, D_FF), lambda i, j: (i, 0, 0))
    return pl.pallas_call(
        _ffn_kernel,
        grid=(B // G, T // L),
        in_specs=[tile, st, _const_spec((1, D)), _layer_spec((D, 2 * D_FF), l), _const_spec((CONV_W, D_FF)),
                  _const_spec((1, D_FF)), _layer_spec((D_FF, D), l)],
        out_specs=[tile, st],
        out_shape=[jax.ShapeDtypeStruct((B, T, D), F32), jax.ShapeDtypeStruct((B, CONV_W - 1, D_FF), F32)],
        scratch_shapes=[pltpu.VMEM((G, CONV_W - 1, D_FF), F32)],
        compiler_params=_params(),
        name="conv_ffn",
    )(x, conv_in, g2, wu_b, cw, cb, wd_b)


def _layer(x, lw, e, mk, mv, mem_layer, shift_in, s0, conv_in, halo_k, halo_v, tiles):
    has_cache = halo_k is not None
    p, q, k, v, qm = _in_proj(x, lw["norm1_g"], lw["w_in"], lw["layer"], lw["qn_g"], lw["kn_g"], lw["mqn_g"], e,
                              *tiles["dense"])
    oa, s_new = _rwkv(p, shift_in, s0, lw, e, *tiles["rwkv"])
    hk, hv = (halo_k, halo_v) if has_cache else (k, v)
    x = _mix(x, oa, q, k, v, hk, hv, lw["sink"], qm, mk, mv, mem_layer, lw["norm1_g"], lw["w_in"],
             lw["w_branch"], lw["w_out"], lw["layer"], *tiles["dense"], has_cache)
    x, conv_new = _ffn(x, conv_in, lw["norm2_g"], lw["w_up"], lw["conv_w"], lw["conv_b"], lw["w_down"], lw["layer"],
                       *tiles["dense"])
    return x, (k, v, s_new, p[:, -1:, :], conv_new)


def kernel(x_prompt, x_sample, cache_swa_k, cache_swa_v, cache_mem_k, cache_mem_v, state_rwkv, state_shift, state_conv, mem_prompt, norm1_g, w_in, rwkv_mu, rwkv_w0, rwkv_w2, rwkv_a0, rwkv_a2, rwkv_g2, rwkv_kk, rwkv_ka, rwkv_rk, rwkv_ln_g, rwkv_ln_b, swa_qn_g, swa_kn_g, swa_sink, mem_norm_g, w_mem_kv, mem_qn_g, mem_kn_g, w_branch, w_out, norm2_g, w_up, conv_w, conv_b, w_down):
    Bp, Tp, _ = x_prompt.shape
    Bs, Ts, _ = x_sample.shape
    dt = x_prompt.dtype
    half = jnp.arange(LANES) // RWKV_HD
    e = (half[:, None] == half[None, :]).astype(BF16)
    row = lambda a: a.reshape(1, -1)
    zpad = jnp.zeros((LANES - 64, RWKV_W), dt)

    w_in_b, w_branch_b, w_out_b, w_up_b, w_down_b, w_mem_kv_b = (
        w.astype(BF16) for w in (w_in, w_branch, w_out, w_up, w_down, w_mem_kv))
    mem_k_rows = cache_mem_k.reshape(DEPTH, Bs, MEM_TOKENS * MEM_HEADS, MEM_HD)
    mem_v_rows = cache_mem_v.reshape(DEPTH, Bs, MEM_TOKENS * MEM_HEADS, MEM_HD)
    yp, ys = x_prompt, x_sample
    outs_p = [[] for _ in range(7)]
    outs_s = [[] for _ in range(5)]
    prompt_tiles = {"dense": (1, 512), "rwkv": (Bp, 256), "att": (1, 512)}
    sample_tiles = {"dense": (8, Ts), "rwkv": (8, Ts), "att": (8, Ts)}
    for l in range(DEPTH):
        lw = {
            "layer": l, "norm1_g": row(norm1_g[l]), "w_in": w_in_b,
            "qn_g": row(jnp.tile(swa_qn_g[l], 2)) * (SWA_HD ** -0.5), "kn_g": row(jnp.tile(swa_kn_g[l], 2)),
            "mqn_g": row(mem_qn_g[l]) * (MEM_HD ** -0.5),
            "mu": row(rwkv_mu[l]), "w0": row(rwkv_w0[l]),
            "w2p": jnp.concatenate([rwkv_w2[l], zpad], axis=0),
            "a0": row(rwkv_a0[l]),
            "a2p": jnp.concatenate([zpad, rwkv_a2[l]], axis=0),
            "g2": rwkv_g2[l], "kk": row(rwkv_kk[l]), "ka": row(rwkv_ka[l]), "rk": row(rwkv_rk[l]),
            "ln_g": row(rwkv_ln_g[l]), "ln_b": row(rwkv_ln_b[l]),
            "sink": swa_sink[l],
            "w_branch": w_branch_b, "w_out": w_out_b,
            "norm2_g": row(norm2_g[l]), "w_up": w_up_b, "conv_w": conv_w[l],
            "conv_b": row(conv_b[l]), "w_down": w_down_b,
        }
        mk, mv = _mem_kv(mem_prompt, row(mem_norm_g[l]), w_mem_kv_b, l, row(mem_kn_g[l]))
        yp, (k, v, s_new, sh_new, cv_new) = _layer(
            yp, lw, e, mk, mv, None,
            jnp.zeros((Bp, 1, RWKV_COLS), dt),
            jnp.zeros((Bp, RWKV_HEADS, RWKV_HD, RWKV_HD), dt),
            jnp.zeros((Bp, CONV_W - 1, D_FF), dt), None, None, prompt_tiles)
        kv_shape = (Bp, WINDOW, 2, SWA_HD)
        for lst, val in zip(outs_p, (k[:, -WINDOW:].reshape(kv_shape), v[:, -WINDOW:].reshape(kv_shape),
                                     mk.reshape(Bp, MEM_TOKENS, MEM_HEADS, MEM_HD),
                                     mv.reshape(Bp, MEM_TOKENS, MEM_HEADS, MEM_HD), s_new, sh_new, cv_new)):
            lst.append(val)
        ck = cache_swa_k[l].reshape(Bs, WINDOW, LANES)
        cv = cache_swa_v[l].reshape(Bs, WINDOW, LANES)
        ys, (k, v, s_new, sh_new, cv_new) = _layer(
            ys, lw, e, mem_k_rows, mem_v_rows, l,
            state_shift[l], state_rwkv[l], state_conv[l], ck, cv, sample_tiles)
        kv_shape = (Bs, WINDOW, 2, SWA_HD)
        kf = jnp.concatenate([ck, k], axis=1)[:, -WINDOW:].reshape(kv_shape)
        vf = jnp.concatenate([cv, v], axis=1)[:, -WINDOW:].reshape(kv_shape)
        for lst, val in zip(outs_s, (kf, vf, s_new, sh_new, cv_new)):
            lst.append(val)
    return (yp, ys) + tuple(jnp.stack(o) for o in outs_p) + tuple(jnp.stack(o) for o in outs_s)
```

```python
import functools
import math

import jax
import jax.numpy as jnp
from jax import lax
from jax.experimental import pallas as pl
from jax.experimental.pallas import tpu as pltpu

F32 = jnp.float32
BF16 = jnp.bfloat16

D_MODEL = 1024
DEPTH = 2
CHUNK = 64
RWKV_HEADS = 8
RWKV_HD = 64
RWKV_W = 512
RWKV_COLS = 1792
GN_EPS = 64e-5
SWA_HEADS = 8
SWA_HD = 64
WINDOW = 128
MEM_TOKENS = 256
MEM_HEADS = 4
MEM_HD = 128
MEM_W = 512
N_BRANCH = 3
D_FF = 2816
CONV_W = 3
RMS_EPS = 1e-6

LANES = 128
PAIRS = RWKV_W // LANES
VMEM_LIMIT = 56 * 1024 * 1024
MXU_DIM = 256
FF_STEP = 6 * MXU_DIM
FF_BLOCKS = tuple((c, min(c + FF_STEP, D_FF)) for c in range(0, D_FF, FF_STEP))
UNROLL = 4
RWKV_GROUP = 4

C_Q = RWKV_COLS
C_K = C_Q + SWA_HEADS * SWA_HD
C_V = C_K + LANES
C_QM = C_V + LANES
C_GT = C_QM + MEM_W
IN_COLS = C_GT + N_BRANCH * D_MODEL


def _dot(a, b, dims=((1,), (0,))):
    return lax.dot_general(a.astype(BF16), b.astype(BF16), (dims, ((), ())), preferred_element_type=F32)


_NT = ((1,), (1,))


def _seg_sum(x, e):
    return jnp.dot(x.astype(BF16), e, preferred_element_type=F32)


def _sigmoid(x):
    return 1.0 / (1.0 + jnp.exp(-x))


def _const_spec(shape):
    n = len(shape)
    return pl.BlockSpec(shape, lambda *_: (0,) * n, pipeline_mode=pl.Buffered(1))


def _layer_spec(shape, l):
    n = len(shape)
    return pl.BlockSpec((None,) + tuple(shape), lambda *_: (l,) + (0,) * n, pipeline_mode=pl.Buffered(1))


def _params():
    return pltpu.CompilerParams(dimension_semantics=("arbitrary", "arbitrary"), vmem_limit_bytes=VMEM_LIMIT)


def _in_kernel(x_ref, g1_ref, w_ref, qng_ref, kng_ref, mqg_ref, e_ref,
               p_ref, q_ref, k_ref, v_ref, qm_ref):
    G, L, D = x_ref.shape
    R = G * L
    x = x_ref[...].reshape(R, D)
    h = x * lax.rsqrt(jnp.mean(x * x, axis=-1, keepdims=True) + RMS_EPS) * g1_ref[...]
    hb = h.astype(BF16)
    e = e_ref[...]

    def proj(c0, c1):
        return jnp.dot(hb, w_ref[:, c0:c1], preferred_element_type=F32)

    def head_rms(z, gain):
        return z * lax.rsqrt(_seg_sum(z * z, e) * (1.0 / SWA_HD) + RMS_EPS) * gain

    p_ref[...] = proj(0, RWKV_COLS).reshape(G, L, RWKV_COLS)
    zq = proj(C_Q, C_K)
    zkv = proj(C_K, C_QM)
    zqm = proj(C_QM, C_GT)
    for s in range(SWA_HEADS * SWA_HD // LANES):
        sl = slice(s * LANES, (s + 1) * LANES)
        q_ref[:, :, sl] = head_rms(zq[:, sl], qng_ref[...]).astype(BF16).reshape(G, L, LANES)
    k_ref[...] = head_rms(zkv[:, :LANES], kng_ref[...]).reshape(G, L, LANES)
    v_ref[...] = zkv[:, LANES:].reshape(G, L, LANES)
    for s in range(MEM_HEADS):
        sl = slice(s * MEM_HD, (s + 1) * MEM_HD)
        z = zqm[:, sl]
        z = z * lax.rsqrt(jnp.mean(z * z, axis=-1, keepdims=True) + RMS_EPS) * mqg_ref[...]
        qm_ref[:, :, sl] = z.astype(BF16).reshape(G, L, MEM_HD)


def _in_proj(x, g1, w_in_b, l, qng, kng, mqg, e, G, L):
    B, T, D = x.shape
    tile = lambda c: pl.BlockSpec((G, L, c), lambda i, j: (i, j, 0))
    outs = ((RWKV_COLS, F32), (SWA_HEADS * SWA_HD, BF16), (LANES, F32), (LANES, F32), (MEM_W, BF16))
    return pl.pallas_call(
        _in_kernel,
        grid=(B // G, T // L),
        in_specs=[tile(D), _const_spec((1, D)), _layer_spec((D, C_GT), l), _const_spec((1, LANES)),
                  _const_spec((1, LANES)), _const_spec((1, MEM_HD)), _const_spec((LANES, LANES))],
        out_specs=[tile(c) for c, _ in outs],
        out_shape=[jax.ShapeDtypeStruct((B, T, c), dt) for c, dt in outs],
        compiler_params=_params(),
        name="in_proj",
    )(x, g1, w_in_b, qng, kng, mqg, e)


def _rwkv_kernel(p_ref, sh_ref, s0_ref, mu_ref, w0_ref, w2_ref, a0_ref, a2_ref, g2_ref, kk_ref, ka_ref,
                 rk_ref, lng_ref, lnb_ref, e_ref,
                 o_ref, sout_ref,
                 prev_scr, st_scr, r_scr, k_scr, v_scr, am_scr, b_scr, lw_scr, g_scr, o_scr,
                 wr_scr, uo_scr, bkt_scr, vb_scr, dcol_scr):
    G, L, _ = p_ref.shape
    C = CHUNK
    nc = L // C
    j = pl.program_id(1)
    e = e_ref[...]

    @pl.when(j == 0)
    def _():
        zero = jnp.zeros((RWKV_HD, RWKV_HD), F32)
        for g in range(G):
            for s in range(PAIRS):
                top = jnp.concatenate([s0_ref[g, 2 * s], zero], axis=1)
                bot = jnp.concatenate([zero, s0_ref[g, 2 * s + 1]], axis=1)
                st_scr[g, s] = jnp.concatenate([top, bot], axis=0).T
        prev_scr[...] = sh_ref[...]

    first_row = lax.broadcasted_iota(jnp.int32, (L, 1), 0) == 0

    def per_token(g):
        p = p_ref[g]
        shifted = jnp.where(first_row, prev_scr[g], pltpu.roll(p, 1, 0))
        pm = p + (shifted - p) * mu_ref[...]
        prev_scr[g] = p[L - 1:L, :]
        rows = slice(g * L, (g + 1) * L)
        r = pm[:, 0:RWKV_W]
        k = pm[:, RWKV_W:2 * RWKV_W]
        v = pm[:, 2 * RWKV_W:3 * RWKV_W]
        xwa = pm[:, 3 * RWKV_W:3 * RWKV_W + LANES]
        xg = pm[:, 3 * RWKV_W + LANES:RWKV_COLS]
        z = w0_ref[...] + _dot(jnp.tanh(xwa), w2_ref[...])
        a = _sigmoid(a0_ref[...] + _dot(xwa, a2_ref[...]))
        kkv = k * kk_ref[...]
        for s in range(PAIRS):
            sl = slice(s * LANES, (s + 1) * LANES)
            kks = kkv[:, sl]
            kkn = kks * lax.rsqrt(jnp.maximum(_seg_sum(kks * kks, e), 1e-24))
            am_scr[rows, sl] = -kkn
            b_scr[rows, sl] = kkn * a[:, sl]
        r_scr[rows, :] = r
        k_scr[rows, :] = k * (1.0 + (a - 1.0) * ka_ref[...])
        v_scr[rows, :] = v
        lw_scr[rows, :] = (-math.exp(-0.5)) * _sigmoid(z)
        g_scr[rows, :] = _dot(_sigmoid(xg), g2_ref[...])

    ri = lax.broadcasted_iota(jnp.int32, (C, C), 0)
    ci = lax.broadcasted_iota(jnp.int32, (C, C), 1)
    cumsum_mat = (ri >= ci).astype(BF16)
    rq = lax.broadcasted_iota(jnp.int32, (C, 2 * C), 0)
    cq = lax.broadcasted_iota(jnp.int32, (C, 2 * C), 1)
    strict_l = jnp.logical_and(cq < C, rq > cq).astype(F32)
    strict_r = jnp.logical_and(cq >= C, rq > cq - C).astype(F32)
    incl = (rq >= cq % C).astype(F32)
    incl_l = jnp.logical_and(cq < C, rq >= cq).astype(F32)
    incl_r = jnp.logical_and(cq >= C, rq >= cq - C).astype(F32)
    r2 = lax.broadcasted_iota(jnp.int32, (2 * C, 2 * C), 0)
    c2 = lax.broadcasted_iota(jnp.int32, (2 * C, 2 * C), 1)
    bd_mask = ((r2 // C) == (c2 // C)).astype(F32)
    lo = lax.broadcasted_iota(jnp.int32, (1, LANES), 1) < RWKV_HD

    def stack(z):
        return jnp.concatenate([jnp.where(lo, z, 0.0), jnp.where(lo, 0.0, z)], axis=0)

    def stack_other(z):
        return jnp.concatenate([jnp.where(lo, 0.0, z), jnp.where(lo, z, 0.0)], axis=0)

    def prepare(chunks):
        units = []
        for i in chunks:
            rows = slice(i * C, (i + 1) * C)
            lw = lw_scr[rows, :]
            lw_hi = lw.astype(BF16)
            lw_lo = (lw - lw_hi.astype(F32)).astype(BF16)
            cum = (jnp.dot(cumsum_mat, lw_hi, preferred_element_type=F32)
                   + jnp.dot(cumsum_mat, lw_lo, preferred_element_type=F32))
            cum_end = cum[C - 1:C, :]
            inv = jnp.exp(-cum)
            dec_rest = jnp.exp(cum_end - cum)
            dec_end = jnp.exp(cum_end)
            kc = k_scr[rows, :]
            bc = b_scr[rows, :]
            r_t = r_scr[rows, :] * jnp.exp(cum)
            a_t = am_scr[rows, :] * jnp.exp(cum - lw)
            b_t = bc * inv
            k_t = kc * inv
            b_e = bc * dec_rest
            k_e = kc * dec_rest
            vc = v_scr[rows, :]
            for s in range(PAIRS):
                sl = slice(s * LANES, (s + 1) * LANES)
                bkt_scr[i, s] = jnp.concatenate([b_e[:, sl], k_e[:, sl]], axis=0).T.astype(BF16)
                vb_scr[i, s] = vc[:, sl].astype(BF16)
                dcol_scr[i, s] = jnp.broadcast_to(dec_end[:, sl], (LANES, LANES)).T
                units.append((i, s, a_t[:, sl], r_t[:, sl], b_t[:, sl], k_t[:, sl], vc[:, sl]))
        N = range(len(units))
        lhs = [jnp.concatenate([u[2], u[3]], axis=0).astype(BF16) for u in units]
        nbk = [_dot(lhs[n], jnp.concatenate([stack(units[n][4]), stack(units[n][5])], axis=0), _NT) for n in N]
        nb = [z[:, :LANES] for z in nbk]
        nk = [z[:, LANES:] for z in nbk]
        m = [jnp.concatenate([nb[n][0:C] * strict_l, nb[n][0:C] * strict_r], axis=0) for n in N]
        mk = [jnp.concatenate([nk[n][0:C] * strict_l, nk[n][0:C] * strict_r], axis=0) for n in N]
        vs = [stack(units[n][6]).astype(BF16) for n in N]
        vsw = [stack_other(pltpu.roll(units[n][6], RWKV_HD, 1)).astype(BF16) for n in N]
        y = [stack(units[n][2]) + _dot(mk[n], vsw[n]) for n in N]
        for it in range(6):
            if it < 5:
                my = [_dot(m[n], jnp.concatenate([m[n], y[n]], axis=1)) for n in N]
                y = [y[n] + my[n][:, LANES:] for n in N]
                m = [my[n][:, :LANES] for n in N]
            else:
                my = [_dot(m[n], y[n]) for n in N]
                y = [y[n] + my[n] for n in N]
        arb = [jnp.concatenate([nb[n][C:2 * C] * incl_l, nb[n][C:2 * C] * incl_r], axis=0) for n in N]
        ay = [_dot(arb[n], y[n]) for n in N]
        akv = [_dot(nk[n][C:2 * C] * incl, vs[n]) for n in N]
        for n in N:
            i, s = units[n][0], units[n][1]
            wf = jnp.where(lo, y[n][0:C], y[n][C:2 * C])
            rw = units[n][3] + jnp.where(lo, ay[n][0:C], ay[n][C:2 * C])
            swapped = jnp.concatenate([jnp.where(lo, y[n][C:2 * C], y[n][0:C]),
                                       jnp.where(lo, ay[n][C:2 * C], ay[n][0:C])], axis=0)
            uo = pltpu.roll(swapped, RWKV_HD, 1)
            wr_scr[i, s] = jnp.concatenate([wf, rw], axis=0).astype(BF16)
            uo_scr[i, s] = jnp.concatenate([uo[0:C], uo[C:2 * C] + akv[n]], axis=0)

    assert (G * nc) % RWKV_GROUP == 0
    seen = set()
    for c0 in range(0, G * nc, RWKV_GROUP):
        chunks = range(c0, c0 + RWKV_GROUP)
        for g in sorted({i // nc for i in chunks} - seen):
            per_token(g)
            seen.add(g)
        prepare(chunks)

    def advance(c):
        units = [(g, s, g * nc + c) for g in range(G) for s in range(PAIRS)]
        hs = [st_scr[g, s] for g, s, _ in units]
        ys = [jnp.dot(wr_scr[i, s], h.astype(BF16), preferred_element_type=F32) + uo_scr[i, s]
              for (g, s, i), h in zip(units, hs)]
        uvb = [jnp.concatenate([y[0:C].astype(BF16), vb_scr[i, s]], axis=0) for (g, s, i), y in zip(units, ys)]
        upd = [jnp.dot(bkt_scr[i, s], z, preferred_element_type=F32) for (g, s, i), z in zip(units, uvb)]
        for (g, s, i), h, y, up in zip(units, hs, ys, upd):
            st_scr[g, s] = h * dcol_scr[i, s] + up * bd_mask
            o_scr[i * C:(i + 1) * C, s * LANES:(s + 1) * LANES] = y[C:2 * C]

    for c in range(nc):
        advance(c)

    for s in range(PAIRS):
        sl = slice(s * LANES, (s + 1) * LANES)
        o = o_scr[:, sl]
        d = o - _seg_sum(o, e) * (1.0 / RWKV_HD)
        var = _seg_sum(d * d, e) * (1.0 / RWKV_HD)
        y = d * lax.rsqrt(var + GN_EPS) * lng_ref[:, sl] + lnb_ref[:, sl]
        bonus = _seg_sum(r_scr[:, sl] * k_scr[:, sl] * rk_ref[:, sl], e) * v_scr[:, sl]
        o_ref[:, :, sl] = ((y + bonus) * g_scr[:, sl]).astype(BF16).reshape(G, L, LANES)

    @pl.when(j == pl.num_programs(1) - 1)
    def _():
        for g in range(G):
            for s in range(PAIRS):
                sp = st_scr[g, s].T
                sout_ref[g, 2 * s] = sp[0:RWKV_HD, 0:RWKV_HD]
                sout_ref[g, 2 * s + 1] = sp[RWKV_HD:, RWKV_HD:]


def _rwkv(p, shift_in, s0, lw, e, G, L):
    B, T, _ = p.shape
    R = G * L
    vec = lambda c: _const_spec((1, c))
    buf = lambda: pltpu.VMEM((R, RWKV_W), F32)
    per_chunk = lambda rows, dt: pltpu.VMEM((R // CHUNK, PAIRS, rows, LANES), dt)
    state = pl.BlockSpec((G, RWKV_HEADS, RWKV_HD, RWKV_HD), lambda i, j: (i, 0, 0, 0))
    return pl.pallas_call(
        _rwkv_kernel,
        grid=(B // G, T // L),
        in_specs=[pl.BlockSpec((G, L, RWKV_COLS), lambda i, j: (i, j, 0)),
                  pl.BlockSpec((G, 1, RWKV_COLS), lambda i, j: (i, 0, 0)),
                  state,
                  vec(RWKV_COLS), vec(RWKV_W), _const_spec((LANES, RWKV_W)), vec(RWKV_W),
                  _const_spec((LANES, RWKV_W)), _const_spec((LANES, RWKV_W)), vec(RWKV_W), vec(RWKV_W),
                  vec(RWKV_W), vec(RWKV_W), vec(RWKV_W), _const_spec((LANES, LANES))],
        out_specs=[pl.BlockSpec((G, L, RWKV_W), lambda i, j: (i, j, 0)),
                   state],
        out_shape=[jax.ShapeDtypeStruct((B, T, RWKV_W), BF16),
                   jax.ShapeDtypeStruct((B, RWKV_HEADS, RWKV_HD, RWKV_HD), F32)],
        scratch_shapes=[pltpu.VMEM((G, 1, RWKV_COLS), F32), pltpu.VMEM((G, PAIRS, LANES, LANES), F32)]
                       + [buf() for _ in range(8)]
                       + [per_chunk(LANES, BF16), per_chunk(LANES, F32), per_chunk(LANES, BF16),
                          per_chunk(CHUNK, BF16), per_chunk(LANES, F32)],
        compiler_params=_params(),
        name="rwkv7",
    )(p, shift_in, s0, lw["mu"], lw["w0"], lw["w2p"], lw["a0"], lw["a2p"], lw["g2"], lw["kk"],
      lw["ka"], lw["rk"], lw["ln_g"], lw["ln_b"], e)


def _mem_kv_kernel(m_ref, g_ref, w_ref, kng_ref, mk_ref, mv_ref):
    G, M, D = m_ref.shape
    x = m_ref[...].reshape(G * M, D)
    h = (x * lax.rsqrt(jnp.mean(x * x, axis=-1, keepdims=True) + RMS_EPS) * g_ref[...]).astype(BF16)
    for s in range(MEM_HEADS):
        z = jnp.dot(h, w_ref[:, s * MEM_HD:(s + 1) * MEM_HD], preferred_element_type=F32)
        z = z * lax.rsqrt(jnp.mean(z * z, axis=-1, keepdims=True) + RMS_EPS) * kng_ref[...]
        zv = jnp.dot(h, w_ref[:, MEM_W + s * MEM_HD:MEM_W + (s + 1) * MEM_HD], preferred_element_type=F32)
        for g in range(G):
            mk_ref[g, pl.ds(s, M, stride=MEM_HEADS), :] = z[g * M:(g + 1) * M]
            mv_ref[g, pl.ds(s, M, stride=MEM_HEADS), :] = zv[g * M:(g + 1) * M]


def _mem_kv(mem, g, w_b, l, kng):
    B, M, D = mem.shape
    heads = pl.BlockSpec((1, M * MEM_HEADS, MEM_HD), lambda i, j: (i, 0, 0))
    return pl.pallas_call(
        _mem_kv_kernel,
        grid=(B, 1),
        in_specs=[pl.BlockSpec((1, M, D), lambda i, j: (i, 0, 0)), _const_spec((1, D)),
                  _layer_spec((D, 2 * MEM_W), l), _const_spec((1, MEM_HD))],
        out_specs=[heads, heads],
        out_shape=[jax.ShapeDtypeStruct((B, M * MEM_HEADS, MEM_HD), F32)] * 2,
        compiler_params=_params(),
        name="mem_kv",
    )(mem, g, w_b, kng)


def _mix_kernel(sink_ref, x_ref, oa_ref, q_ref, k_ref, v_ref, hk_ref, hv_ref, qm_ref, mk_ref, mv_ref,
                g1_ref, wg_ref, wb_ref, wo_ref, y_ref, kd, vd, kb, vb, ob_scr, om_scr, *, has_cache):
    G, L, D = x_ref.shape
    R = G * L
    C = CHUNK
    nc = L // C
    KB = WINDOW + C
    GROUP = 4
    j = pl.program_id(1)
    lo = lax.broadcasted_iota(jnp.int32, (1, LANES), 1) < SWA_HD
    qi = lax.broadcasted_iota(jnp.int32, (C, KB), 0)
    kj = lax.broadcasted_iota(jnp.int32, (C, KB), 1)
    dist = jnp.abs(WINDOW + qi - kj).astype(F32)
    kj4 = lax.broadcasted_iota(jnp.int32, (GROUP * C, KB), 1)
    ones = jnp.ones((WINDOW + L, LANES), F32)
    neg_pad = jnp.full((GROUP * C, 2 * LANES - KB), -jnp.inf, F32)
    bias = [jnp.concatenate([(2.0 ** -(GROUP * kv + h + 1)) * dist for h in range(GROUP)], axis=0)
            for kv in range(2)]
    sink = [jnp.concatenate([jnp.full((C, LANES), sink_ref[GROUP * kv + h], F32) for h in range(GROUP)], axis=0)
            for kv in range(2)]
    for g in range(G):
        kc = jnp.concatenate([hk_ref[g], k_ref[g]], axis=0)
        vc = jnp.concatenate([hv_ref[g], v_ref[g]], axis=0)
        ks = pltpu.roll(kc, SWA_HD, 1)
        vs = pltpu.roll(vc, SWA_HD, 1)
        kd[g, 0] = jnp.where(lo, kc, ks).astype(BF16)
        kd[g, 1] = jnp.where(lo, ks, kc).astype(BF16)
        vd[g, 0] = jnp.concatenate([jnp.where(lo, vc, vs), ones], axis=1).astype(BF16)
        vd[g, 1] = jnp.concatenate([jnp.where(lo, vs, vc), ones], axis=1).astype(BF16)

    mem_units = [(g, h) for g in range(G) for h in range(MEM_HEADS)]
    msl = lambda h: slice(h * MEM_HD, (h + 1) * MEM_HD)

    @pl.when(j == 0)
    def _():
        for g, h in mem_units:
            kb[g, h] = mk_ref[g, pl.ds(h, MEM_TOKENS, stride=MEM_HEADS), :].astype(BF16)
            vb[g, h] = mv_ref[g, pl.ds(h, MEM_TOKENS, stride=MEM_HEADS), :].astype(BF16)

    x = x_ref[...].reshape(R, D)
    hb = (x * lax.rsqrt(jnp.mean(x * x, axis=-1, keepdims=True) + RMS_EPS) * g1_ref[...]).astype(BF16)
    todo = list(range(N_BRANCH))
    gates = []

    def next_gate():
        if todo:
            n = todo.pop(0)
            gates.append(jnp.dot(hb, wg_ref[:, n * D:(n + 1) * D], preferred_element_type=F32))

    def swa_scores(chunk_ids):
        units = [(i // nc, (i % nc) * C, kv) for i in chunk_ids for kv in range(2)]
        scs = []
        for g, off, kv in units:
            q0 = q_ref[g, off:off + C, (2 * kv) * LANES:(2 * kv + 1) * LANES]
            q1 = q_ref[g, off:off + C, (2 * kv + 1) * LANES:(2 * kv + 2) * LANES]
            zero = jnp.zeros_like(q0)
            lhs = jnp.concatenate([jnp.where(lo, q0, zero), jnp.where(lo, zero, q0),
                                   jnp.where(lo, q1, zero), jnp.where(lo, zero, q1)], axis=0)
            sc = lax.dot_general(lhs, kd[g, kv, off:off + KB, :], ((_NT), ((), ())),
                                 preferred_element_type=F32) - bias[kv]
            if not has_cache and off < WINDOW:
                sc = jnp.where(jnp.logical_and(j == 0, kj4 + off < WINDOW), -jnp.inf, sc)
            scs.append(sc)
        return units, scs

    def swa_finish(units, scs):
        exs, sink_terms = [], []
        for (g, off, kv), sc in zip(units, scs):
            folded = jnp.maximum(sc[:, :LANES], jnp.concatenate([sc[:, LANES:], neg_pad], axis=1))
            mx = jnp.maximum(jnp.broadcast_to(jnp.max(folded, axis=-1, keepdims=True), (GROUP * C, LANES)),
                             sink[kv])
            ex = jnp.concatenate([jnp.exp(sc[:, :LANES] - mx), jnp.exp(sc[:, LANES:] - mx[:, :KB - LANES])],
                                 axis=1)
            exs.append(ex.astype(BF16))
            sink_terms.append(jnp.exp(sink[kv] - mx))
        for (g, off, kv), ex, st in zip(units, exs, sink_terms):
            pvd = jnp.dot(ex, vd[g, kv, off:off + KB, :], preferred_element_type=F32)
            pv = (pvd[:, :LANES] / (pvd[:, LANES:] + st)).astype(BF16)
            ob_scr[g, off:off + C, (2 * kv) * LANES:(2 * kv + 1) * LANES] = jnp.where(lo, pv[0:C], pv[C:2 * C])
            ob_scr[g, off:off + C, (2 * kv + 1) * LANES:(2 * kv + 2) * LANES] = jnp.where(
                lo, pv[2 * C:3 * C], pv[3 * C:4 * C])

    for c0 in range(0, G * nc, UNROLL):
        units, scs = swa_scores(range(c0, c0 + UNROLL))
        next_gate()
        swa_finish(units, scs)

    mones = jnp.ones((MEM_TOKENS, MEM_HD), BF16)
    mscs = [lax.dot_general(qm_ref[g, :, msl(h)], kb[g, h], (_NT, ((), ())), preferred_element_type=F32)
            for g, h in mem_units]
    while todo:
        next_gate()
    mexs = [jnp.exp(sc - jnp.max(sc, axis=-1, keepdims=True)).astype(BF16) for sc in mscs]
    for (g, h), ex in zip(mem_units, mexs):
        den = jnp.dot(ex, mones, preferred_element_type=F32)
        pv = jnp.dot(ex, vb[g, h], preferred_element_type=F32)
        om_scr[g, :, msl(h)] = (pv / den).astype(BF16)

    mix = None
    for n, o_ref in enumerate((oa_ref, ob_scr, om_scr)):
        br = jnp.dot(o_ref[...].reshape(R, RWKV_W), wb_ref[n], preferred_element_type=F32)
        t = _sigmoid(gates[n]) * br
        mix = t if mix is None else mix + t
    y_ref[...] = (x + jnp.dot(mix.astype(BF16), wo_ref[...], preferred_element_type=F32)).reshape(G, L, D)


def _mix(x, oa, q, k, v, halo_k, halo_v, sink, qm, mk, mv, mem_layer, g1, w_in_b, wb_b, wo_b, l, G, L, has_cache):
    B, T, D = x.shape
    if has_cache:
        halo = pl.BlockSpec((G, WINDOW, LANES), lambda i, j: (i, 0, 0))
    else:
        per = L // WINDOW
        halo = pl.BlockSpec((G, WINDOW, LANES), lambda i, j: (i, jnp.maximum(j * per - 1, 0), 0))
    tile = lambda c: pl.BlockSpec((G, L, c), lambda i, j: (i, j, 0))
    if mem_layer is None:
        mem = pl.BlockSpec((G, MEM_TOKENS * MEM_HEADS, MEM_HD), lambda i, j: (i, 0, 0))
    else:
        mem = pl.BlockSpec((None, G, MEM_TOKENS * MEM_HEADS, MEM_HD), lambda i, j: (mem_layer, i, 0, 0))
    cat = lambda cols: pltpu.VMEM((G, 2, WINDOW + L, cols), BF16)
    gathered = pltpu.VMEM((G, MEM_HEADS, MEM_TOKENS, MEM_HD), BF16)
    branch = pltpu.VMEM((G, L, RWKV_W), BF16)
    return pl.pallas_call(
        functools.partial(_mix_kernel, has_cache=has_cache),
        grid=(B // G, T // L),
        in_specs=[pl.BlockSpec(memory_space=pltpu.SMEM), tile(D), tile(RWKV_W), tile(SWA_HEADS * SWA_HD),
                  tile(LANES), tile(LANES), halo, halo, tile(MEM_W), mem, mem, _const_spec((1, D)),
                  pl.BlockSpec((None, D, N_BRANCH * D), lambda *_: (l, 0, C_GT // (N_BRANCH * D)),
                               pipeline_mode=pl.Buffered(1)),
                  _layer_spec((N_BRANCH, RWKV_W, D), l), _layer_spec((D, D), l)],
        out_specs=tile(D),
        out_shape=jax.ShapeDtypeStruct((B, T, D), F32),
        scratch_shapes=[cat(LANES), cat(2 * LANES), gathered, gathered, branch, branch],
        compiler_params=_params(),
        name="mix",
    )(sink, x, oa, q, k, v, halo_k, halo_v, qm, mk, mv, g1, w_in_b, wb_b, wo_b)


def _ffn_kernel(x_ref, cin_ref, g2_ref, wu_ref, cw_ref, cb_ref, wd_ref, y_ref, cout_ref, carry):
    G, L, D = x_ref.shape
    R = G * L
    j = pl.program_id(1)

    @pl.when(j == 0)
    def _():
        carry[...] = cin_ref[...]

    x = x_ref[...].reshape(R, D)
    hb = (x * lax.rsqrt(jnp.mean(x * x, axis=-1, keepdims=True) + RMS_EPS) * g2_ref[...]).astype(BF16)
    row = lax.broadcasted_iota(jnp.int32, (L, 1), 0)
    acc = x
    for c0, c1 in FF_BLOCKS:
        cs = slice(c0, c1)
        a_in = jnp.dot(hb, wu_ref[:, cs], preferred_element_type=F32)
        u = jnp.dot(hb, wu_ref[:, D_FF + c0:D_FF + c1], preferred_element_type=F32)
        convs = []
        for g in range(G):
            a = a_in[g * L:(g + 1) * L]
            prev = carry[g, :, cs]
            a1 = jnp.where(row == 0, prev[1:2], pltpu.roll(a, 1, 0))
            a2 = jnp.where(row == 0, prev[0:1], jnp.where(row == 1, prev[1:2], pltpu.roll(a, 2, 0)))
            carry[g, :, cs] = a[L - 2:L]
            convs.append(cb_ref[:, cs] + a2 * cw_ref[0:1, cs] + a1 * cw_ref[1:2, cs] + a * cw_ref[2:3, cs])
        c = convs[0] if G == 1 else jnp.concatenate(convs, axis=0)
        gelu = 0.5 * c * (1.0 + jnp.tanh(0.7978845608028654 * (c + 0.044715 * (c * c * c))))
        acc = acc + jnp.dot((gelu * u).astype(BF16), wd_ref[cs, :], preferred_element_type=F32)
    y_ref[...] = acc.reshape(G, L, D)

    @pl.when(j == pl.num_programs(1) - 1)
    def _():
        cout_ref[...] = carry[...]


def _ffn(x, conv_in, g2, wu_b, cw, cb, wd_b, l, G, L):
    B, T, D = x.shape
    tile = pl.BlockSpec((G, L, D), lambda i, j: (i, j, 0))
    st = pl.BlockSpec((G, CONV_W - 1, D_FF), lambda i, j: (i, 0, 0))
    return pl.pallas_call(
        _ffn_kernel,
        grid=(B // G, T // L),
        in_specs=[tile, st, _const_spec((1, D)), _layer_spec((D, 2 * D_FF), l), _const_spec((CONV_W, D_FF)),
                  _const_spec((1, D_FF)), _layer_spec((D_FF, D), l)],
        out_specs=[tile, st],
        out_shape=[jax.ShapeDtypeStruct((B, T, D), F32), jax.ShapeDtypeStruct((B, CONV_W - 1, D_FF), F32)],
        scratch_shapes=[pltpu.VMEM((G, CONV_W - 1, D_FF), F32)],
        compiler_params=_params(),
        name="conv_ffn",
    )(x, conv_in, g2, wu_b, cw, cb, wd_b)


def _layer(x, lw, e, mk, mv, mem_layer, shift_in, s0, conv_in, halo_k, halo_v, tiles):
    has_cache = halo_k is not None
    p, q, k, v, qm = _in_proj(x, lw["norm1_g"], lw["w_in"], lw["layer"], lw["qn_g"], lw["kn_g"], lw["mqn_g"], e,
                              *tiles["dense"])
    oa, s_new = _rwkv(p, shift_in, s0, lw, e, *tiles["rwkv"])
    hk, hv = (halo_k, halo_v) if has_cache else (k, v)
    x = _mix(x, oa, q, k, v, hk, hv, lw["sink"], qm, mk, mv, mem_layer, lw["norm1_g"], lw["w_in"],
             lw["w_branch"], lw["w_out"], lw["layer"], *tiles["dense"], has_cache)
    x, conv_new = _ffn(x, conv_in, lw["norm2_g"], lw["w_up"], lw["conv_w"], lw["conv_b"], lw["w_down"], lw["layer"],
                       *tiles["dense"])
    return x, (k, v, s_new, p[:, -1:, :], conv_new)


def kernel(x_prompt, x_sample, cache_swa_k, cache_swa_v, cache_mem_k, cache_mem_v, state_rwkv, state_shift, state_conv, mem_prompt, norm1_g, w_in, rwkv_mu, rwkv_w0, rwkv_w2, rwkv_a0, rwkv_a2, rwkv_g2, rwkv_kk, rwkv_ka, rwkv_rk, rwkv_ln_g, rwkv_ln_b, swa_qn_g, swa_kn_g, swa_sink, mem_norm_g, w_mem_kv, mem_qn_g, mem_kn_g, w_branch, w_out, norm2_g, w_up, conv_w, conv_b, w_down):
    Bp, Tp, _ = x_prompt.shape
    Bs, Ts, _ = x_sample.shape
    dt = x_prompt.dtype
    half = jnp.arange(LANES) // RWKV_HD
    e = (half[:, None] == half[None, :]).astype(BF16)
    row = lambda a: a.reshape(1, -1)
    zpad = jnp.zeros((LANES - 64, RWKV_W), dt)

    w_in_b, w_branch_b, w_out_b, w_up_b, w_down_b, w_mem_kv_b = (
        w.astype(BF16) for w in (w_in, w_branch, w_out, w_up, w_down, w_mem_kv))
    mem_k_rows = cache_mem_k.reshape(DEPTH, Bs, MEM_TOKENS * MEM_HEADS, MEM_HD)
    mem_v_rows = cache_mem_v.reshape(DEPTH, Bs, MEM_TOKENS * MEM_HEADS, MEM_HD)
    yp, ys = x_prompt, x_sample
    outs_p = [[] for _ in range(7)]
    outs_s = [[] for _ in range(5)]
    prompt_tiles = {"dense": (1, 512), "rwkv": (Bp, 256)}
    sample_tiles = {"dense": (8, Ts), "rwkv": (8, Ts)}
    for l in range(DEPTH):
        lw = {
            "layer": l, "norm1_g": row(norm1_g[l]), "w_in": w_in_b,
            "qn_g": row(jnp.tile(swa_qn_g[l], 2)) * (SWA_HD ** -0.5), "kn_g": row(jnp.tile(swa_kn_g[l], 2)),
            "mqn_g": row(mem_qn_g[l]) * (MEM_HD ** -0.5),
            "mu": row(rwkv_mu[l]), "w0": row(rwkv_w0[l]),
            "w2p": jnp.concatenate([rwkv_w2[l], zpad], axis=0),
            "a0": row(rwkv_a0[l]),
            "a2p": jnp.concatenate([zpad, rwkv_a2[l]], axis=0),
            "g2": rwkv_g2[l], "kk": row(rwkv_kk[l]), "ka": row(rwkv_ka[l]), "rk": row(rwkv_rk[l]),
            "ln_g": row(rwkv_ln_g[l]), "ln_b": row(rwkv_ln_b[l]),
            "sink": swa_sink[l],
            "w_branch": w_branch_b, "w_out": w_out_b,
            "norm2_g": row(norm2_g[l]), "w_up": w_up_b, "conv_w": conv_w[l],
            "conv_b": row(conv_b[l]), "w_down": w_down_b,
        }
        mk, mv = _mem_kv(mem_prompt, row(mem_norm_g[l]), w_mem_kv_b, l, row(mem_kn_g[l]))
        yp, (k, v, s_new, sh_new, cv_new) = _layer(
            yp, lw, e, mk, mv, None,
            jnp.zeros((Bp, 1, RWKV_COLS), dt),
            jnp.zeros((Bp, RWKV_HEADS, RWKV_HD, RWKV_HD), dt),
            jnp.zeros((Bp, CONV_W - 1, D_FF), dt), None, None, prompt_tiles)
        kv_shape = (Bp, WINDOW, 2, SWA_HD)
        for lst, val in zip(outs_p, (k[:, -WINDOW:].reshape(kv_shape), v[:, -WINDOW:].reshape(kv_shape),
                                     mk.reshape(Bp, MEM_TOKENS, MEM_HEADS, MEM_HD),
                                     mv.reshape(Bp, MEM_TOKENS, MEM_HEADS, MEM_HD), s_new, sh_new, cv_new)):
            lst.append(val)
        ck = cache_swa_k[l].reshape(Bs, WINDOW, LANES)
        cv = cache_swa_v[l].reshape(Bs, WINDOW, LANES)
        ys, (k, v, s_new, sh_new, cv_new) = _layer(
            ys, lw, e, mem_k_rows, mem_v_rows, l,
            state_shift[l], state_rwkv[l], state_conv[l], ck, cv, sample_tiles)
        kv_shape = (Bs, WINDOW, 2, SWA_HD)
        kf = jnp.concatenate([ck, k], axis=1)[:, -WINDOW:].reshape(kv_shape)
        vf = jnp.concatenate([cv, v], axis=1)[:, -WINDOW:].reshape(kv_shape)
        for lst, val in zip(outs_s, (kf, vf, s_new, sh_new, cv_new)):
            lst.append(val)
    return (yp, ys) + tuple(jnp.stack(o) for o in outs_p) + tuple(jnp.stack(o) for o in outs_s)
```

```python
import functools
import math

import jax
import jax.numpy as jnp
from jax import lax
from jax.experimental import pallas as pl
from jax.experimental.pallas import tpu as pltpu

F32 = jnp.float32
BF16 = jnp.bfloat16

D_MODEL = 1024
DEPTH = 2
CHUNK = 64
RWKV_HEADS = 8
RWKV_HD = 64
RWKV_W = 512
RWKV_COLS = 1792
GN_EPS = 64e-5
SWA_HEADS = 8
SWA_HD = 64
WINDOW = 128
MEM_TOKENS = 256
MEM_HEADS = 4
MEM_HD = 128
MEM_W = 512
N_BRANCH = 3
D_FF = 2816
CONV_W = 3
RMS_EPS = 1e-6

LANES = 128
PAIRS = RWKV_W // LANES
VMEM_LIMIT = 56 * 1024 * 1024
MXU_DIM = 256
FF_STEP = 6 * MXU_DIM
FF_BLOCKS = tuple((c, min(c + FF_STEP, D_FF)) for c in range(0, D_FF, FF_STEP))
UNROLL = 4
RWKV_GROUP = 4

C_Q = RWKV_COLS
C_K = C_Q + SWA_HEADS * SWA_HD
C_V = C_K + LANES
C_QM = C_V + LANES
C_GT = C_QM + MEM_W
IN_COLS = C_GT + N_BRANCH * D_MODEL


def _dot(a, b, dims=((1,), (0,))):
    return lax.dot_general(a.astype(BF16), b.astype(BF16), (dims, ((), ())), preferred_element_type=F32)


_NT = ((1,), (1,))


def _seg_sum(x, e):
    return jnp.dot(x.astype(BF16), e, preferred_element_type=F32)


def _sigmoid(x):
    return 1.0 / (1.0 + jnp.exp(-x))


def _const_spec(shape):
    n = len(shape)
    return pl.BlockSpec(shape, lambda *_: (0,) * n, pipeline_mode=pl.Buffered(1))


def _layer_spec(shape, l):
    n = len(shape)
    return pl.BlockSpec((None,) + tuple(shape), lambda *_: (l,) + (0,) * n, pipeline_mode=pl.Buffered(1))


def _params():
    return pltpu.CompilerParams(dimension_semantics=("arbitrary", "arbitrary"), vmem_limit_bytes=VMEM_LIMIT)


def _in_kernel(x_ref, g1_ref, w_ref, qng_ref, kng_ref, mqg_ref, e_ref,
               p_ref, q_ref, k_ref, v_ref, qm_ref):
    G, L, D = x_ref.shape
    R = G * L
    x = x_ref[...].reshape(R, D)
    h = x * lax.rsqrt(jnp.mean(x * x, axis=-1, keepdims=True) + RMS_EPS) * g1_ref[...]
    hb = h.astype(BF16)
    e = e_ref[...]

    def proj(c0, c1):
        return jnp.dot(hb, w_ref[:, c0:c1], preferred_element_type=F32)

    def head_rms(z, gain):
        return z * lax.rsqrt(_seg_sum(z * z, e) * (1.0 / SWA_HD) + RMS_EPS) * gain

    p_ref[...] = proj(0, RWKV_COLS).reshape(G, L, RWKV_COLS)
    zq = proj(C_Q, C_K)
    zkv = proj(C_K, C_QM)
    zqm = proj(C_QM, C_GT)
    for s in range(SWA_HEADS * SWA_HD // LANES):
        sl = slice(s * LANES, (s + 1) * LANES)
        q_ref[:, :, sl] = head_rms(zq[:, sl], qng_ref[...]).astype(BF16).reshape(G, L, LANES)
    k_ref[...] = head_rms(zkv[:, :LANES], kng_ref[...]).reshape(G, L, LANES)
    v_ref[...] = zkv[:, LANES:].reshape(G, L, LANES)
    for s in range(MEM_HEADS):
        sl = slice(s * MEM_HD, (s + 1) * MEM_HD)
        z = zqm[:, sl]
        z = z * lax.rsqrt(jnp.mean(z * z, axis=-1, keepdims=True) + RMS_EPS) * mqg_ref[...]
        qm_ref[:, :, sl] = z.astype(BF16).reshape(G, L, MEM_HD)


def _in_proj(x, g1, w_in_b, l, qng, kng, mqg, e, G, L):
    B, T, D = x.shape
    tile = lambda c: pl.BlockSpec((G, L, c), lambda i, j: (i, j, 0))
    outs = ((RWKV_COLS, F32), (SWA_HEADS * SWA_HD, BF16), (LANES, F32), (LANES, F32), (MEM_W, BF16))
    return pl.pallas_call(
        _in_kernel,
        grid=(B // G, T // L),
        in_specs=[tile(D), _const_spec((1, D)), _layer_spec((D, C_GT), l), _const_spec((1, LANES)),
                  _const_spec((1, LANES)), _const_spec((1, MEM_HD)), _const_spec((LANES, LANES))],
        out_specs=[tile(c) for c, _ in outs],
        out_shape=[jax.ShapeDtypeStruct((B, T, c), dt) for c, dt in outs],
        compiler_params=_params(),
        name="in_proj",
    )(x, g1, w_in_b, qng, kng, mqg, e)


def _rwkv_kernel(p_ref, sh_ref, s0_ref, mu_ref, w0_ref, w2_ref, a0_ref, a2_ref, g2_ref, kk_ref, ka_ref,
                 rk_ref, lng_ref, lnb_ref, e_ref,
                 o_ref, sout_ref,
                 prev_scr, st_scr, r_scr, k_scr, v_scr, am_scr, b_scr, lw_scr, g_scr, o_scr,
                 wr_scr, uo_scr, bkt_scr, vb_scr, dcol_scr):
    G, L, _ = p_ref.shape
    C = CHUNK
    nc = L // C
    j = pl.program_id(1)
    e = e_ref[...]

    @pl.when(j == 0)
    def _():
        zero = jnp.zeros((RWKV_HD, RWKV_HD), F32)
        for g in range(G):
            for s in range(PAIRS):
                top = jnp.concatenate([s0_ref[g, 2 * s], zero], axis=1)
                bot = jnp.concatenate([zero, s0_ref[g, 2 * s + 1]], axis=1)
                st_scr[g, s] = jnp.concatenate([top, bot], axis=0).T
        prev_scr[...] = sh_ref[...]

    first_row = lax.broadcasted_iota(jnp.int32, (L, 1), 0) == 0

    def per_token(g):
        p = p_ref[g]
        shifted = jnp.where(first_row, prev_scr[g], pltpu.roll(p, 1, 0))
        pm = p + (shifted - p) * mu_ref[...]
        prev_scr[g] = p[L - 1:L, :]
        rows = slice(g * L, (g + 1) * L)
        r = pm[:, 0:RWKV_W]
        k = pm[:, RWKV_W:2 * RWKV_W]
        v = pm[:, 2 * RWKV_W:3 * RWKV_W]
        xwa = pm[:, 3 * RWKV_W:3 * RWKV_W + LANES]
        xg = pm[:, 3 * RWKV_W + LANES:RWKV_COLS]
        z = w0_ref[...] + _dot(jnp.tanh(xwa), w2_ref[...])
        a = _sigmoid(a0_ref[...] + _dot(xwa, a2_ref[...]))
        kkv = k * kk_ref[...]
        for s in range(PAIRS):
            sl = slice(s * LANES, (s + 1) * LANES)
            kks = kkv[:, sl]
            kkn = kks * lax.rsqrt(jnp.maximum(_seg_sum(kks * kks, e), 1e-24))
            am_scr[rows, sl] = -kkn
            b_scr[rows, sl] = kkn * a[:, sl]
        r_scr[rows, :] = r
        k_scr[rows, :] = k * (1.0 + (a - 1.0) * ka_ref[...])
        v_scr[rows, :] = v
        lw_scr[rows, :] = (-math.exp(-0.5)) * _sigmoid(z)
        g_scr[rows, :] = _dot(_sigmoid(xg), g2_ref[...])

    ri = lax.broadcasted_iota(jnp.int32, (C, C), 0)
    ci = lax.broadcasted_iota(jnp.int32, (C, C), 1)
    cumsum_mat = (ri >= ci).astype(BF16)
    rq = lax.broadcasted_iota(jnp.int32, (C, 2 * C), 0)
    cq = lax.broadcasted_iota(jnp.int32, (C, 2 * C), 1)
    strict_l = jnp.logical_and(cq < C, rq > cq).astype(F32)
    strict_r = jnp.logical_and(cq >= C, rq > cq - C).astype(F32)
    incl = (rq >= cq % C).astype(F32)
    incl_l = jnp.logical_and(cq < C, rq >= cq).astype(F32)
    incl_r = jnp.logical_and(cq >= C, rq >= cq - C).astype(F32)
    r2 = lax.broadcasted_iota(jnp.int32, (2 * C, 2 * C), 0)
    c2 = lax.broadcasted_iota(jnp.int32, (2 * C, 2 * C), 1)
    bd_mask = ((r2 // C) == (c2 // C)).astype(F32)
    lo = lax.broadcasted_iota(jnp.int32, (1, LANES), 1) < RWKV_HD

    def stack(z):
        return jnp.concatenate([jnp.where(lo, z, 0.0), jnp.where(lo, 0.0, z)], axis=0)

    def stack_other(z):
        return jnp.concatenate([jnp.where(lo, 0.0, z), jnp.where(lo, z, 0.0)], axis=0)

    def prepare(chunks):
        units = []
        for i in chunks:
            rows = slice(i * C, (i + 1) * C)
            lw = lw_scr[rows, :]
            lw_hi = lw.astype(BF16)
            lw_lo = (lw - lw_hi.astype(F32)).astype(BF16)
            cum = (jnp.dot(cumsum_mat, lw_hi, preferred_element_type=F32)
                   + jnp.dot(cumsum_mat, lw_lo, preferred_element_type=F32))
            cum_end = cum[C - 1:C, :]
            inv = jnp.exp(-cum)
            dec_rest = jnp.exp(cum_end - cum)
            dec_end = jnp.exp(cum_end)
            kc = k_scr[rows, :]
            bc = b_scr[rows, :]
            r_t = r_scr[rows, :] * jnp.exp(cum)
            a_t = am_scr[rows, :] * jnp.exp(cum - lw)
            b_t = bc * inv
            k_t = kc * inv
            b_e = bc * dec_rest
            k_e = kc * dec_rest
            vc = v_scr[rows, :]
            for s in range(PAIRS):
                sl = slice(s * LANES, (s + 1) * LANES)
                bkt_scr[i, s] = jnp.concatenate([b_e[:, sl], k_e[:, sl]], axis=0).astype(BF16).T
                vb_scr[i, s] = vc[:, sl].astype(BF16)
                dcol_scr[i, s] = jnp.broadcast_to(dec_end[:, sl], (LANES, LANES)).T
                units.append((i, s, a_t[:, sl], r_t[:, sl], b_t[:, sl], k_t[:, sl], vc[:, sl]))
        N = range(len(units))
        lhs = [jnp.concatenate([u[2], u[3]], axis=0).astype(BF16) for u in units]
        nbk = [_dot(lhs[n], jnp.concatenate([stack(units[n][4]), stack(units[n][5])], axis=0), _NT) for n in N]
        nb = [z[:, :LANES] for z in nbk]
        nk = [z[:, LANES:] for z in nbk]
        m = [jnp.concatenate([nb[n][0:C] * strict_l, nb[n][0:C] * strict_r], axis=0) for n in N]
        mk = [jnp.concatenate([nk[n][0:C] * strict_l, nk[n][0:C] * strict_r], axis=0) for n in N]
        vs = [stack(units[n][6]).astype(BF16) for n in N]
        vsw = [stack_other(pltpu.roll(units[n][6], RWKV_HD, 1)).astype(BF16) for n in N]
        y = [stack(units[n][2]) + _dot(mk[n], vsw[n]) for n in N]
        for it in range(6):
            if it < 5:
                my = [_dot(m[n], jnp.concatenate([m[n], y[n]], axis=1)) for n in N]
                y = [y[n] + my[n][:, LANES:] for n in N]
                m = [my[n][:, :LANES] for n in N]
            else:
                my = [_dot(m[n], y[n]) for n in N]
                y = [y[n] + my[n] for n in N]
        arb = [jnp.concatenate([nb[n][C:2 * C] * incl_l, nb[n][C:2 * C] * incl_r], axis=0) for n in N]
        ay = [_dot(arb[n], y[n]) for n in N]
        akv = [_dot(nk[n][C:2 * C] * incl, vs[n]) for n in N]
        for n in N:
            i, s = units[n][0], units[n][1]
            wf = jnp.where(lo, y[n][0:C], y[n][C:2 * C])
            rw = units[n][3] + jnp.where(lo, ay[n][0:C], ay[n][C:2 * C])
            swapped = jnp.concatenate([jnp.where(lo, y[n][C:2 * C], y[n][0:C]),
                                       jnp.where(lo, ay[n][C:2 * C], ay[n][0:C])], axis=0)
            uo = pltpu.roll(swapped, RWKV_HD, 1)
            wr_scr[i, s] = jnp.concatenate([wf, rw], axis=0).astype(BF16)
            uo_scr[i, s] = jnp.concatenate([uo[0:C], uo[C:2 * C] + akv[n]], axis=0)

    assert (G * nc) % RWKV_GROUP == 0
    seen = set()
    for c0 in range(0, G * nc, RWKV_GROUP):
        chunks = range(c0, c0 + RWKV_GROUP)
        for g in sorted({i // nc for i in chunks} - seen):
            per_token(g)
            seen.add(g)
        prepare(chunks)

    def advance(c):
        units = [(g, s, g * nc + c) for g in range(G) for s in range(PAIRS)]
        hs = [st_scr[g, s] for g, s, _ in units]
        ys = [jnp.dot(wr_scr[i, s], h.astype(BF16), preferred_element_type=F32) + uo_scr[i, s]
              for (g, s, i), h in zip(units, hs)]
        uvb = [jnp.concatenate([y[0:C].astype(BF16), vb_scr[i, s]], axis=0) for (g, s, i), y in zip(units, ys)]
        upd = [jnp.dot(bkt_scr[i, s], z, preferred_element_type=F32) for (g, s, i), z in zip(units, uvb)]
        for (g, s, i), h, y, up in zip(units, hs, ys, upd):
            st_scr[g, s] = h * dcol_scr[i, s] + up * bd_mask
            o_scr[i * C:(i + 1) * C, s * LANES:(s + 1) * LANES] = y[C:2 * C]

    for c in range(nc):
        advance(c)

    for s in range(PAIRS):
        sl = slice(s * LANES, (s + 1) * LANES)
        o = o_scr[:, sl]
        d = o - _seg_sum(o, e) * (1.0 / RWKV_HD)
        var = _seg_sum(d * d, e) * (1.0 / RWKV_HD)
        y = d * lax.rsqrt(var + GN_EPS) * lng_ref[:, sl] + lnb_ref[:, sl]
        bonus = _seg_sum(r_scr[:, sl] * k_scr[:, sl] * rk_ref[:, sl], e) * v_scr[:, sl]
        o_ref[:, :, sl] = ((y + bonus) * g_scr[:, sl]).astype(BF16).reshape(G, L, LANES)

    @pl.when(j == pl.num_programs(1) - 1)
    def _():
        for g in range(G):
            for s in range(PAIRS):
                sp = st_scr[g, s].T
                sout_ref[g, 2 * s] = sp[0:RWKV_HD, 0:RWKV_HD]
                sout_ref[g, 2 * s + 1] = sp[RWKV_HD:, RWKV_HD:]


def _rwkv(p, shift_in, s0, lw, e, G, L):
    B, T, _ = p.shape
    R = G * L
    vec = lambda c: _const_spec((1, c))
    buf = lambda: pltpu.VMEM((R, RWKV_W), F32)
    per_chunk = lambda rows, dt: pltpu.VMEM((R // CHUNK, PAIRS, rows, LANES), dt)
    state = pl.BlockSpec((G, RWKV_HEADS, RWKV_HD, RWKV_HD), lambda i, j: (i, 0, 0, 0))
    return pl.pallas_call(
        _rwkv_kernel,
        grid=(B // G, T // L),
        in_specs=[pl.BlockSpec((G, L, RWKV_COLS), lambda i, j: (i, j, 0)),
                  pl.BlockSpec((G, 1, RWKV_COLS), lambda i, j: (i, 0, 0)),
                  state,
                  vec(RWKV_COLS), vec(RWKV_W), _const_spec((LANES, RWKV_W)), vec(RWKV_W),
                  _const_spec((LANES, RWKV_W)), _const_spec((LANES, RWKV_W)), vec(RWKV_W), vec(RWKV_W),
                  vec(RWKV_W), vec(RWKV_W), vec(RWKV_W), _const_spec((LANES, LANES))],
        out_specs=[pl.BlockSpec((G, L, RWKV_W), lambda i, j: (i, j, 0)),
                   state],
        out_shape=[jax.ShapeDtypeStruct((B, T, RWKV_W), BF16),
                   jax.ShapeDtypeStruct((B, RWKV_HEADS, RWKV_HD, RWKV_HD), F32)],
        scratch_shapes=[pltpu.VMEM((G, 1, RWKV_COLS), F32), pltpu.VMEM((G, PAIRS, LANES, LANES), F32)]
                       + [buf() for _ in range(8)]
                       + [per_chunk(LANES, BF16), per_chunk(LANES, F32), per_chunk(LANES, BF16),
                          per_chunk(CHUNK, BF16), per_chunk(LANES, F32)],
        compiler_params=_params(),
        name="rwkv7",
    )(p, shift_in, s0, lw["mu"], lw["w0"], lw["w2p"], lw["a0"], lw["a2p"], lw["g2"], lw["kk"],
      lw["ka"], lw["rk"], lw["ln_g"], lw["ln_b"], e)


def _mem_kv_kernel(m_ref, g_ref, w_ref, kng_ref, mk_ref, mv_ref):
    G, M, D = m_ref.shape
    x = m_ref[...].reshape(G * M, D)
    h = (x * lax.rsqrt(jnp.mean(x * x, axis=-1, keepdims=True) + RMS_EPS) * g_ref[...]).astype(BF16)
    for s in range(MEM_HEADS):
        z = jnp.dot(h, w_ref[:, s * MEM_HD:(s + 1) * MEM_HD], preferred_element_type=F32)
        z = z * lax.rsqrt(jnp.mean(z * z, axis=-1, keepdims=True) + RMS_EPS) * kng_ref[...]
        zv = jnp.dot(h, w_ref[:, MEM_W + s * MEM_HD:MEM_W + (s + 1) * MEM_HD], preferred_element_type=F32)
        for g in range(G):
            mk_ref[g, pl.ds(s, M, stride=MEM_HEADS), :] = z[g * M:(g + 1) * M]
            mv_ref[g, pl.ds(s, M, stride=MEM_HEADS), :] = zv[g * M:(g + 1) * M]


def _mem_kv(mem, g, w_b, l, kng):
    B, M, D = mem.shape
    heads = pl.BlockSpec((1, M * MEM_HEADS, MEM_HD), lambda i, j: (i, 0, 0))
    return pl.pallas_call(
        _mem_kv_kernel,
        grid=(B, 1),
        in_specs=[pl.BlockSpec((1, M, D), lambda i, j: (i, 0, 0)), _const_spec((1, D)),
                  _layer_spec((D, 2 * MEM_W), l), _const_spec((1, MEM_HD))],
        out_specs=[heads, heads],
        out_shape=[jax.ShapeDtypeStruct((B, M * MEM_HEADS, MEM_HD), F32)] * 2,
        compiler_params=_params(),
        name="mem_kv",
    )(mem, g, w_b, kng)


def _mix_kernel(sink_ref, x_ref, oa_ref, q_ref, k_ref, v_ref, hk_ref, hv_ref, qm_ref, mk_ref, mv_ref,
                g1_ref, wg_ref, wb_ref, wo_ref, y_ref, kd, vd, kb, vb, ob_scr, om_scr, *, has_cache):
    G, L, D = x_ref.shape
    R = G * L
    C = CHUNK
    nc = L // C
    KB = WINDOW + C
    GROUP = 4
    j = pl.program_id(1)
    lo = lax.broadcasted_iota(jnp.int32, (1, LANES), 1) < SWA_HD
    qi = lax.broadcasted_iota(jnp.int32, (C, KB), 0)
    kj = lax.broadcasted_iota(jnp.int32, (C, KB), 1)
    dist = jnp.abs(WINDOW + qi - kj).astype(F32)
    kj4 = lax.broadcasted_iota(jnp.int32, (GROUP * C, KB), 1)
    ones = jnp.ones((WINDOW + L, LANES), F32)
    neg_pad = jnp.full((GROUP * C, 2 * LANES - KB), -jnp.inf, F32)
    bias = [jnp.concatenate([(2.0 ** -(GROUP * kv + h + 1)) * dist for h in range(GROUP)], axis=0)
            for kv in range(2)]
    sink = [jnp.concatenate([jnp.full((C, LANES), sink_ref[GROUP * kv + h], F32) for h in range(GROUP)], axis=0)
            for kv in range(2)]
    for g in range(G):
        kc = jnp.concatenate([hk_ref[g], k_ref[g]], axis=0)
        vc = jnp.concatenate([hv_ref[g], v_ref[g]], axis=0)
        ks = pltpu.roll(kc, SWA_HD, 1)
        vs = pltpu.roll(vc, SWA_HD, 1)
        kd[g, 0] = jnp.where(lo, kc, ks).astype(BF16)
        kd[g, 1] = jnp.where(lo, ks, kc).astype(BF16)
        vd[g, 0] = jnp.concatenate([jnp.where(lo, vc, vs), ones], axis=1).astype(BF16)
        vd[g, 1] = jnp.concatenate([jnp.where(lo, vs, vc), ones], axis=1).astype(BF16)

    mem_units = [(g, h) for g in range(G) for h in range(MEM_HEADS)]
    msl = lambda h: slice(h * MEM_HD, (h + 1) * MEM_HD)

    @pl.when(j == 0)
    def _():
        for g, h in mem_units:
            kb[g, h] = mk_ref[g, pl.ds(h, MEM_TOKENS, stride=MEM_HEADS), :].astype(BF16)
            vb[g, h] = mv_ref[g, pl.ds(h, MEM_TOKENS, stride=MEM_HEADS), :].astype(BF16)

    x = x_ref[...].reshape(R, D)
    hb = (x * lax.rsqrt(jnp.mean(x * x, axis=-1, keepdims=True) + RMS_EPS) * g1_ref[...]).astype(BF16)
    todo = list(range(N_BRANCH))
    gates = []

    def next_gate():
        if todo:
            n = todo.pop(0)
            gates.append(jnp.dot(hb, wg_ref[:, n * D:(n + 1) * D], preferred_element_type=F32))

    def swa_scores(chunk_ids):
        units = [(i // nc, (i % nc) * C, kv) for i in chunk_ids for kv in range(2)]
        scs = []
        for g, off, kv in units:
            q0 = q_ref[g, off:off + C, (2 * kv) * LANES:(2 * kv + 1) * LANES]
            q1 = q_ref[g, off:off + C, (2 * kv + 1) * LANES:(2 * kv + 2) * LANES]
            zero = jnp.zeros_like(q0)
            lhs = jnp.concatenate([jnp.where(lo, q0, zero), jnp.where(lo, zero, q0),
                                   jnp.where(lo, q1, zero), jnp.where(lo, zero, q1)], axis=0)
            sc = lax.dot_general(lhs, kd[g, kv, off:off + KB, :], ((_NT), ((), ())),
                                 preferred_element_type=F32) - bias[kv]
            if not has_cache and off < WINDOW:
                sc = jnp.where(jnp.logical_and(j == 0, kj4 + off < WINDOW), -jnp.inf, sc)
            scs.append(sc)
        return units, scs

    def swa_finish(units, scs):
        exs, sink_terms = [], []
        for (g, off, kv), sc in zip(units, scs):
            folded = jnp.maximum(sc[:, :LANES], jnp.concatenate([sc[:, LANES:], neg_pad], axis=1))
            mx = jnp.maximum(jnp.broadcast_to(jnp.max(folded, axis=-1, keepdims=True), (GROUP * C, LANES)),
                             sink[kv])
            ex = jnp.concatenate([jnp.exp(sc[:, :LANES] - mx), jnp.exp(sc[:, LANES:] - mx[:, :KB - LANES])],
                                 axis=1)
            exs.append(ex.astype(BF16))
            sink_terms.append(jnp.exp(sink[kv] - mx))
        for (g, off, kv), ex, st in zip(units, exs, sink_terms):
            pvd = jnp.dot(ex, vd[g, kv, off:off + KB, :], preferred_element_type=F32)
            pv = (pvd[:, :LANES] / (pvd[:, LANES:] + st)).astype(BF16)
            ob_scr[g, off:off + C, (2 * kv) * LANES:(2 * kv + 1) * LANES] = jnp.where(lo, pv[0:C], pv[C:2 * C])
            ob_scr[g, off:off + C, (2 * kv + 1) * LANES:(2 * kv + 2) * LANES] = jnp.where(
                lo, pv[2 * C:3 * C], pv[3 * C:4 * C])

    for c0 in range(0, G * nc, UNROLL):
        units, scs = swa_scores(range(c0, c0 + UNROLL))
        next_gate()
        swa_finish(units, scs)

    mones = jnp.ones((MEM_TOKENS, MEM_HD), BF16)
    mscs = [lax.dot_general(qm_ref[g, :, msl(h)], kb[g, h], (_NT, ((), ())), preferred_element_type=F32)
            for g, h in mem_units]
    while todo:
        next_gate()
    mexs = [jnp.exp(sc - jnp.max(sc, axis=-1, keepdims=True)).astype(BF16) for sc in mscs]
    for (g, h), ex in zip(mem_units, mexs):
        den = jnp.dot(ex, mones, preferred_element_type=F32)
        pv = jnp.dot(ex, vb[g, h], preferred_element_type=F32)
        om_scr[g, :, msl(h)] = (pv / den).astype(BF16)

    mix = None
    for n, o_ref in enumerate((oa_ref, ob_scr, om_scr)):
        br = jnp.dot(o_ref[...].reshape(R, RWKV_W), wb_ref[n], preferred_element_type=F32)
        t = _sigmoid(gates[n]) * br
        mix = t if mix is None else mix + t
    y_ref[...] = (x + jnp.dot(mix.astype(BF16), wo_ref[...], preferred_element_type=F32)).reshape(G, L, D)


def _mix(x, oa, q, k, v, halo_k, halo_v, sink, qm, mk, mv, mem_layer, g1, w_in_b, wb_b, wo_b, l, G, L, has_cache):
    B, T, D = x.shape
    if has_cache:
        halo = pl.BlockSpec((G, WINDOW, LANES), lambda i, j: (i, 0, 0))
    else:
        per = L // WINDOW
        halo = pl.BlockSpec((G, WINDOW, LANES), lambda i, j: (i, jnp.maximum(j * per - 1, 0), 0))
    tile = lambda c: pl.BlockSpec((G, L, c), lambda i, j: (i, j, 0))
    if mem_layer is None:
        mem = pl.BlockSpec((G, MEM_TOKENS * MEM_HEADS, MEM_HD), lambda i, j: (i, 0, 0))
    else:
        mem = pl.BlockSpec((None, G, MEM_TOKENS * MEM_HEADS, MEM_HD), lambda i, j: (mem_layer, i, 0, 0))
    cat = lambda cols: pltpu.VMEM((G, 2, WINDOW + L, cols), BF16)
    gathered = pltpu.VMEM((G, MEM_HEADS, MEM_TOKENS, MEM_HD), BF16)
    branch = pltpu.VMEM((G, L, RWKV_W), BF16)
    return pl.pallas_call(
        functools.partial(_mix_kernel, has_cache=has_cache),
        grid=(B // G, T // L),
        in_specs=[pl.BlockSpec(memory_space=pltpu.SMEM), tile(D), tile(RWKV_W), tile(SWA_HEADS * SWA_HD),
                  tile(LANES), tile(LANES), halo, halo, tile(MEM_W), mem, mem, _const_spec((1, D)),
                  pl.BlockSpec((None, D, N_BRANCH * D), lambda *_: (l, 0, C_GT // (N_BRANCH * D)),
                               pipeline_mode=pl.Buffered(1)),
                  _layer_spec((N_BRANCH, RWKV_W, D), l), _layer_spec((D, D), l)],
        out_specs=tile(D),
        out_shape=jax.ShapeDtypeStruct((B, T, D), F32),
        scratch_shapes=[cat(LANES), cat(2 * LANES), gathered, gathered, branch, branch],
        compiler_params=_params(),
        name="mix",
    )(sink, x, oa, q, k, v, halo_k, halo_v, qm, mk, mv, g1, w_in_b, wb_b, wo_b)


def _ffn_kernel(x_ref, cin_ref, g2_ref, wu_ref, cw_ref, cb_ref, wd_ref, y_ref, cout_ref, carry):
    G, L, D = x_ref.shape
    R = G * L
    j = pl.program_id(1)

    @pl.when(j == 0)
    def _():
        carry[...] = cin_ref[...]

    x = x_ref[...].reshape(R, D)
    hb = (x * lax.rsqrt(jnp.mean(x * x, axis=-1, keepdims=True) + RMS_EPS) * g2_ref[...]).astype(BF16)
    row = lax.broadcasted_iota(jnp.int32, (L, 1), 0)
    acc = x
    for c0, c1 in FF_BLOCKS:
        cs = slice(c0, c1)
        a_in = jnp.dot(hb, wu_ref[:, cs], preferred_element_type=F32)
        u = jnp.dot(hb, wu_ref[:, D_FF + c0:D_FF + c1], preferred_element_type=F32)
        convs = []
        for g in range(G):
            a = a_in[g * L:(g + 1) * L]
            prev = carry[g, :, cs]
            a1 = jnp.where(row == 0, prev[1:2], pltpu.roll(a, 1, 0))
            a2 = jnp.where(row == 0, prev[0:1], jnp.where(row == 1, prev[1:2], pltpu.roll(a, 2, 0)))
            carry[g, :, cs] = a[L - 2:L]
            convs.append(cb_ref[:, cs] + a2 * cw_ref[0:1, cs] + a1 * cw_ref[1:2, cs] + a * cw_ref[2:3, cs])
        c = convs[0] if G == 1 else jnp.concatenate(convs, axis=0)
        gelu = 0.5 * c * (1.0 + jnp.tanh(0.7978845608028654 * (c + 0.044715 * (c * c * c))))
        acc = acc + jnp.dot((gelu * u).astype(BF16), wd_ref[cs, :], preferred_element_type=F32)
    y_ref[...] = acc.reshape(G, L, D)

    @pl.when(j == pl.num_programs(1) - 1)
    def _():
        cout_ref[...] = carry[...]


def _ffn(x, conv_in, g2, wu_b, cw, cb, wd_b, l, G, L):
    B, T, D = x.shape
    tile = pl.BlockSpec((G, L, D), lambda i, j: (i, j, 0))
    st = pl.BlockSpec((G, CONV_W - 1, D_FF), lambda i, j: (i, 0, 0))
    return pl.pallas_call(
        _ffn_kernel,
        grid=(B // G, T // L),
        in_specs=[tile, st, _const_spec((1, D)), _layer_spec((D, 2 * D_FF), l), _const_spec((CONV_W, D_FF)),
                  _const_spec((1, D_FF)), _layer_spec((D_FF, D), l)],
        out_specs=[tile, st],
        out_shape=[jax.ShapeDtypeStruct((B, T, D), F32), jax.ShapeDtypeStruct((B, CONV_W - 1, D_FF), F32)],
        scratch_shapes=[pltpu.VMEM((G, CONV_W - 1, D_FF), F32)],
        compiler_params=_params(),
        name="conv_ffn",
    )(x, conv_in, g2, wu_b, cw, cb, wd_b)


def _layer(x, lw, e, mk, mv, mem_layer, shift_in, s0, conv_in, halo_k, halo_v, tiles):
    has_cache = halo_k is not None
    p, q, k, v, qm = _in_proj(x, lw["norm1_g"], lw["w_in"], lw["layer"], lw["qn_g"], lw["kn_g"], lw["mqn_g"], e,
                              *tiles["in"])
    oa, s_new = _rwkv(p, shift_in, s0, lw, e, *tiles["rwkv"])
    hk, hv = (halo_k, halo_v) if has_cache else (k, v)
    x = _mix(x, oa, q, k, v, hk, hv, lw["sink"], qm, mk, mv, mem_layer, lw["norm1_g"], lw["w_in"],
             lw["w_branch"], lw["w_out"], lw["layer"], *tiles["dense"], has_cache)
    x, conv_new = _ffn(x, conv_in, lw["norm2_g"], lw["w_up"], lw["conv_w"], lw["conv_b"], lw["w_down"], lw["layer"],
                       *tiles["dense"])
    return x, (k, v, s_new, p[:, -1:, :], conv_new)


def kernel(x_prompt, x_sample, cache_swa_k, cache_swa_v, cache_mem_k, cache_mem_v, state_rwkv, state_shift, state_conv, mem_prompt, norm1_g, w_in, rwkv_mu, rwkv_w0, rwkv_w2, rwkv_a0, rwkv_a2, rwkv_g2, rwkv_kk, rwkv_ka, rwkv_rk, rwkv_ln_g, rwkv_ln_b, swa_qn_g, swa_kn_g, swa_sink, mem_norm_g, w_mem_kv, mem_qn_g, mem_kn_g, w_branch, w_out, norm2_g, w_up, conv_w, conv_b, w_down):
    Bp, Tp, _ = x_prompt.shape
    Bs, Ts, _ = x_sample.shape
    dt = x_prompt.dtype
    half = jnp.arange(LANES) // RWKV_HD
    e = (half[:, None] == half[None, :]).astype(BF16)
    row = lambda a: a.reshape(1, -1)
    zpad = jnp.zeros((LANES - 64, RWKV_W), dt)

    w_in_b, w_branch_b, w_out_b, w_up_b, w_down_b, w_mem_kv_b = (
        w.astype(BF16) for w in (w_in, w_branch, w_out, w_up, w_down, w_mem_kv))
    mem_k_rows = cache_mem_k.reshape(DEPTH, Bs, MEM_TOKENS * MEM_HEADS, MEM_HD)
    mem_v_rows = cache_mem_v.reshape(DEPTH, Bs, MEM_TOKENS * MEM_HEADS, MEM_HD)
    yp, ys = x_prompt, x_sample
    outs_p = [[] for _ in range(7)]
    outs_s = [[] for _ in range(5)]
    prompt_tiles = {"in": (1, 1024), "dense": (1, 512), "rwkv": (Bp, 256)}
    sample_tiles = {"in": (8, Ts), "dense": (8, Ts), "rwkv": (8, Ts)}
    for l in range(DEPTH):
        lw = {
            "layer": l, "norm1_g": row(norm1_g[l]), "w_in": w_in_b,
            "qn_g": row(jnp.tile(swa_qn_g[l], 2)) * (SWA_HD ** -0.5), "kn_g": row(jnp.tile(swa_kn_g[l], 2)),
            "mqn_g": row(mem_qn_g[l]) * (MEM_HD ** -0.5),
            "mu": row(rwkv_mu[l]), "w0": row(rwkv_w0[l]),
            "w2p": jnp.concatenate([rwkv_w2[l], zpad], axis=0),
            "a0": row(rwkv_a0[l]),
            "a2p": jnp.concatenate([zpad, rwkv_a2[l]], axis=0),
            "g2": rwkv_g2[l], "kk": row(rwkv_kk[l]), "ka": row(rwkv_ka[l]), "rk": row(rwkv_rk[l]),
            "ln_g": row(rwkv_ln_g[l]), "ln_b": row(rwkv_ln_b[l]),
            "sink": swa_sink[l],
            "w_branch": w_branch_b, "w_out": w_out_b,
            "norm2_g": row(norm2_g[l]), "w_up": w_up_b, "conv_w": conv_w[l],
            "conv_b": row(conv_b[l]), "w_down": w_down_b,
        }
        mk, mv = _mem_kv(mem_prompt, row(mem_norm_g[l]), w_mem_kv_b, l, row(mem_kn_g[l]))
        yp, (k, v, s_new, sh_new, cv_new) = _layer(
            yp, lw, e, mk, mv, None,
            jnp.zeros((Bp, 1, RWKV_COLS), dt),
            jnp.zeros((Bp, RWKV_HEADS, RWKV_HD, RWKV_HD), dt),
            jnp.zeros((Bp, CONV_W - 1, D_FF), dt), None, None, prompt_tiles)
        kv_shape = (Bp, WINDOW, 2, SWA_HD)
        for lst, val in zip(outs_p, (k[:, -WINDOW:].reshape(kv_shape), v[:, -WINDOW:].reshape(kv_shape),
                                     mk.reshape(Bp, MEM_TOKENS, MEM_HEADS, MEM_HD),
                                     mv.reshape(Bp, MEM_TOKENS, MEM_HEADS, MEM_HD), s_new, sh_new, cv_new)):
            lst.append(val)
        ck = cache_swa_k[l].reshape(Bs, WINDOW, LANES)
        cv = cache_swa_v[l].reshape(Bs, WINDOW, LANES)
        ys, (k, v, s_new, sh_new, cv_new) = _layer(
            ys, lw, e, mem_k_rows, mem_v_rows, l,
            state_shift[l], state_rwkv[l], state_conv[l], ck, cv, sample_tiles)
        kv_shape = (Bs, WINDOW, 2, SWA_HD)
        kf = jnp.concatenate([ck, k], axis=1)[:, -WINDOW:].reshape(kv_shape)
        vf = jnp.concatenate([cv, v], axis=1)[:, -WINDOW:].reshape(kv_shape)
        for lst, val in zip(outs_s, (kf, vf, s_new, sh_new, cv_new)):
            lst.append(val)
    return (yp, ys) + tuple(jnp.stack(o) for o in outs_p) + tuple(jnp.stack(o) for o in outs_s)
```

```python
import functools
import math

import jax
import jax.numpy as jnp
from jax import lax
from jax.experimental import pallas as pl
from jax.experimental.pallas import tpu as pltpu

F32 = jnp.float32
BF16 = jnp.bfloat16

D_MODEL = 1024
DEPTH = 2
CHUNK = 64
RWKV_HEADS = 8
RWKV_HD = 64
RWKV_W = 512
RWKV_COLS = 1792
GN_EPS = 64e-5
SWA_HEADS = 8
SWA_HD = 64
WINDOW = 128
MEM_TOKENS = 256
MEM_HEADS = 4
MEM_HD = 128
MEM_W = 512
N_BRANCH = 3
D_FF = 2816
CONV_W = 3
RMS_EPS = 1e-6

LANES = 128
PAIRS = RWKV_W // LANES
VMEM_LIMIT = 56 * 1024 * 1024
MXU_DIM = 256
FF_STEP = 6 * MXU_DIM
FF_BLOCKS = tuple((c, min(c + FF_STEP, D_FF)) for c in range(0, D_FF, FF_STEP))
UNROLL = 4
RWKV_GROUP = 4

C_Q = RWKV_COLS
C_K = C_Q + SWA_HEADS * SWA_HD
C_V = C_K + LANES
C_QM = C_V + LANES
C_GT = C_QM + MEM_W
IN_COLS = C_GT + N_BRANCH * D_MODEL


def _dot(a, b, dims=((1,), (0,))):
    return lax.dot_general(a.astype(BF16), b.astype(BF16), (dims, ((), ())), preferred_element_type=F32)


_NT = ((1,), (1,))


def _seg_sum(x, e):
    return jnp.dot(x.astype(BF16), e, preferred_element_type=F32)


def _sigmoid(x):
    return 1.0 / (1.0 + jnp.exp(-x))


def _const_spec(shape):
    n = len(shape)
    return pl.BlockSpec(shape, lambda *_: (0,) * n, pipeline_mode=pl.Buffered(1))


def _layer_spec(shape, l):
    n = len(shape)
    return pl.BlockSpec((None,) + tuple(shape), lambda *_: (l,) + (0,) * n, pipeline_mode=pl.Buffered(1))


def _params():
    return pltpu.CompilerParams(dimension_semantics=("arbitrary", "arbitrary"), vmem_limit_bytes=VMEM_LIMIT)


def _in_kernel(x_ref, g1_ref, w_ref, qng_ref, kng_ref, mqg_ref, e_ref,
               p_ref, q_ref, k_ref, v_ref, qm_ref):
    G, L, D = x_ref.shape
    R = G * L
    x = x_ref[...].reshape(R, D)
    h = x * lax.rsqrt(jnp.mean(x * x, axis=-1, keepdims=True) + RMS_EPS) * g1_ref[...]
    hb = h.astype(BF16)
    e = e_ref[...]

    def proj(c0, c1):
        return jnp.dot(hb, w_ref[:, c0:c1], preferred_element_type=F32)

    def head_rms(z, gain):
        return z * lax.rsqrt(_seg_sum(z * z, e) * (1.0 / SWA_HD) + RMS_EPS) * gain

    p_ref[...] = proj(0, RWKV_COLS).reshape(G, L, RWKV_COLS)
    zq = proj(C_Q, C_K)
    zkv = proj(C_K, C_QM)
    zqm = proj(C_QM, C_GT)
    for s in range(SWA_HEADS * SWA_HD // LANES):
        sl = slice(s * LANES, (s + 1) * LANES)
        q_ref[:, :, sl] = head_rms(zq[:, sl], qng_ref[...]).astype(BF16).reshape(G, L, LANES)
    k_ref[...] = head_rms(zkv[:, :LANES], kng_ref[...]).reshape(G, L, LANES)
    v_ref[...] = zkv[:, LANES:].reshape(G, L, LANES)
    for s in range(MEM_HEADS):
        sl = slice(s * MEM_HD, (s + 1) * MEM_HD)
        z = zqm[:, sl]
        z = z * lax.rsqrt(jnp.mean(z * z, axis=-1, keepdims=True) + RMS_EPS) * mqg_ref[...]
        qm_ref[:, :, sl] = z.astype(BF16).reshape(G, L, MEM_HD)


def _in_proj(x, g1, w_in_b, l, qng, kng, mqg, e, G, L):
    B, T, D = x.shape
    tile = lambda c: pl.BlockSpec((G, L, c), lambda i, j: (i, j, 0))
    outs = ((RWKV_COLS, F32), (SWA_HEADS * SWA_HD, BF16), (LANES, F32), (LANES, F32), (MEM_W, BF16))
    return pl.pallas_call(
        _in_kernel,
        grid=(B // G, T // L),
        in_specs=[tile(D), _const_spec((1, D)), _layer_spec((D, C_GT), l), _const_spec((1, LANES)),
                  _const_spec((1, LANES)), _const_spec((1, MEM_HD)), _const_spec((LANES, LANES))],
        out_specs=[tile(c) for c, _ in outs],
        out_shape=[jax.ShapeDtypeStruct((B, T, c), dt) for c, dt in outs],
        compiler_params=_params(),
        name="in_proj",
    )(x, g1, w_in_b, qng, kng, mqg, e)


def _rwkv_kernel(p_ref, sh_ref, s0_ref, mu_ref, w0_ref, w2_ref, a0_ref, a2_ref, g2_ref, kk_ref, ka_ref,
                 rk_ref, lng_ref, lnb_ref, e_ref,
                 o_ref, sout_ref,
                 prev_scr, st_scr, r_scr, k_scr, v_scr, am_scr, b_scr, lw_scr, g_scr, o_scr,
                 wr_scr, uo_scr, bkt_scr, vb_scr, dcol_scr):
    G, L, _ = p_ref.shape
    C = CHUNK
    nc = L // C
    j = pl.program_id(1)
    e = e_ref[...]

    @pl.when(j == 0)
    def _():
        zero = jnp.zeros((RWKV_HD, RWKV_HD), F32)
        for g in range(G):
            for s in range(PAIRS):
                top = jnp.concatenate([s0_ref[g, 2 * s], zero], axis=1)
                bot = jnp.concatenate([zero, s0_ref[g, 2 * s + 1]], axis=1)
                st_scr[g, s] = jnp.concatenate([top, bot], axis=0).T
        prev_scr[...] = sh_ref[...]

    first_row = lax.broadcasted_iota(jnp.int32, (L, 1), 0) == 0

    def per_token(g):
        p = p_ref[g]
        shifted = jnp.where(first_row, prev_scr[g], pltpu.roll(p, 1, 0))
        pm = p + (shifted - p) * mu_ref[...]
        prev_scr[g] = p[L - 1:L, :]
        rows = slice(g * L, (g + 1) * L)
        r = pm[:, 0:RWKV_W]
        k = pm[:, RWKV_W:2 * RWKV_W]
        v = pm[:, 2 * RWKV_W:3 * RWKV_W]
        xwa = pm[:, 3 * RWKV_W:3 * RWKV_W + LANES]
        xg = pm[:, 3 * RWKV_W + LANES:RWKV_COLS]
        z = w0_ref[...] + _dot(jnp.tanh(xwa), w2_ref[...])
        a = _sigmoid(a0_ref[...] + _dot(xwa, a2_ref[...]))
        kkv = k * kk_ref[...]
        for s in range(PAIRS):
            sl = slice(s * LANES, (s + 1) * LANES)
            kks = kkv[:, sl]
            kkn = kks * lax.rsqrt(jnp.maximum(_seg_sum(kks * kks, e), 1e-24))
            am_scr[rows, sl] = -kkn
            b_scr[rows, sl] = kkn * a[:, sl]
        r_scr[rows, :] = r
        k_scr[rows, :] = k * (1.0 + (a - 1.0) * ka_ref[...])
        v_scr[rows, :] = v
        lw_scr[rows, :] = (-math.exp(-0.5)) * _sigmoid(z)
        g_scr[rows, :] = _dot(_sigmoid(xg), g2_ref[...])

    ri = lax.broadcasted_iota(jnp.int32, (C, C), 0)
    ci = lax.broadcasted_iota(jnp.int32, (C, C), 1)
    cumsum_mat = (ri >= ci).astype(BF16)
    rq = lax.broadcasted_iota(jnp.int32, (C, 2 * C), 0)
    cq = lax.broadcasted_iota(jnp.int32, (C, 2 * C), 1)
    strict_l = jnp.logical_and(cq < C, rq > cq).astype(F32)
    strict_r = jnp.logical_and(cq >= C, rq > cq - C).astype(F32)
    incl = (rq >= cq % C).astype(F32)
    incl_l = jnp.logical_and(cq < C, rq >= cq).astype(F32)
    incl_r = jnp.logical_and(cq >= C, rq >= cq - C).astype(F32)
    r2 = lax.broadcasted_iota(jnp.int32, (2 * C, 2 * C), 0)
    c2 = lax.broadcasted_iota(jnp.int32, (2 * C, 2 * C), 1)
    bd_mask = ((r2 // C) == (c2 // C)).astype(F32)
    lo = lax.broadcasted_iota(jnp.int32, (1, LANES), 1) < RWKV_HD

    def stack(z):
        return jnp.concatenate([jnp.where(lo, z, 0.0), jnp.where(lo, 0.0, z)], axis=0)

    def stack_other(z):
        return jnp.concatenate([jnp.where(lo, 0.0, z), jnp.where(lo, z, 0.0)], axis=0)

    def prepare(chunks):
        units = []
        for i in chunks:
            rows = slice(i * C, (i + 1) * C)
            lw = lw_scr[rows, :]
            lw_hi = lw.astype(BF16)
            lw_lo = (lw - lw_hi.astype(F32)).astype(BF16)
            cum = (jnp.dot(cumsum_mat, lw_hi, preferred_element_type=F32)
                   + jnp.dot(cumsum_mat, lw_lo, preferred_element_type=F32))
            cum_end = cum[C - 1:C, :]
            inv = jnp.exp(-cum)
            dec_rest = jnp.exp(cum_end - cum)
            dec_end = jnp.exp(cum_end)
            kc = k_scr[rows, :]
            bc = b_scr[rows, :]
            r_t = r_scr[rows, :] * jnp.exp(cum)
            a_t = am_scr[rows, :] * jnp.exp(cum - lw)
            b_t = bc * inv
            k_t = kc * inv
            b_e = bc * dec_rest
            k_e = kc * dec_rest
            vc = v_scr[rows, :]
            for s in range(PAIRS):
                sl = slice(s * LANES, (s + 1) * LANES)
                bkt_scr[i, s] = jnp.concatenate([b_e[:, sl], k_e[:, sl]], axis=0).astype(BF16).T
                vb_scr[i, s] = vc[:, sl].astype(BF16)
                dcol_scr[i, s] = jnp.broadcast_to(dec_end[:, sl], (LANES, LANES)).T
                units.append((i, s, a_t[:, sl], r_t[:, sl], b_t[:, sl], k_t[:, sl], vc[:, sl]))
        N = range(len(units))
        lhs = [jnp.concatenate([u[2], u[3]], axis=0).astype(BF16) for u in units]
        nbk = [_dot(lhs[n], jnp.concatenate([stack(units[n][4]), stack(units[n][5])], axis=0), _NT) for n in N]
        nb = [z[:, :LANES] for z in nbk]
        nk = [z[:, LANES:] for z in nbk]
        m = [jnp.concatenate([nb[n][0:C] * strict_l, nb[n][0:C] * strict_r], axis=0) for n in N]
        mk = [jnp.concatenate([nk[n][0:C] * strict_l, nk[n][0:C] * strict_r], axis=0) for n in N]
        vs = [stack(units[n][6]).astype(BF16) for n in N]
        vsw = [stack_other(pltpu.roll(units[n][6], RWKV_HD, 1)).astype(BF16) for n in N]
        y = [stack(units[n][2]) + _dot(mk[n], vsw[n]) for n in N]
        for it in range(6):
            if it < 5:
                my = [_dot(m[n], jnp.concatenate([m[n], y[n]], axis=1)) for n in N]
                y = [y[n] + my[n][:, LANES:] for n in N]
                m = [my[n][:, :LANES] for n in N]
            else:
                my = [_dot(m[n], y[n]) for n in N]
                y = [y[n] + my[n] for n in N]
        arb = [jnp.concatenate([nb[n][C:2 * C] * incl_l, nb[n][C:2 * C] * incl_r], axis=0) for n in N]
        ay = [_dot(arb[n], y[n]) for n in N]
        akv = [_dot(nk[n][C:2 * C] * incl, vs[n]) for n in N]
        for n in N:
            i, s = units[n][0], units[n][1]
            wf = jnp.where(lo, y[n][0:C], y[n][C:2 * C])
            rw = units[n][3] + jnp.where(lo, ay[n][0:C], ay[n][C:2 * C])
            swapped = jnp.concatenate([jnp.where(lo, y[n][C:2 * C], y[n][0:C]),
                                       jnp.where(lo, ay[n][C:2 * C], ay[n][0:C])], axis=0)
            uo = pltpu.roll(swapped, RWKV_HD, 1)
            wr_scr[i, s] = jnp.concatenate([wf, rw], axis=0).astype(BF16)
            uo_scr[i, s] = jnp.concatenate([uo[0:C], uo[C:2 * C] + akv[n]], axis=0)

    assert (G * nc) % RWKV_GROUP == 0
    seen = set()
    for c0 in range(0, G * nc, RWKV_GROUP):
        chunks = range(c0, c0 + RWKV_GROUP)
        for g in sorted({i // nc for i in chunks} - seen):
            per_token(g)
            seen.add(g)
        prepare(chunks)

    def advance(c):
        units = [(g, s, g * nc + c) for g in range(G) for s in range(PAIRS)]
        hs = [st_scr[g, s] for g, s, _ in units]
        ys = [jnp.dot(wr_scr[i, s], h.astype(BF16), preferred_element_type=F32) + uo_scr[i, s]
              for (g, s, i), h in zip(units, hs)]
        uvb = [jnp.concatenate([y[0:C].astype(BF16), vb_scr[i, s]], axis=0) for (g, s, i), y in zip(units, ys)]
        upd = [jnp.dot(bkt_scr[i, s], z, preferred_element_type=F32) for (g, s, i), z in zip(units, uvb)]
        for (g, s, i), h, y, up in zip(units, hs, ys, upd):
            st_scr[g, s] = h * dcol_scr[i, s] + up * bd_mask
            o_scr[i * C:(i + 1) * C, s * LANES:(s + 1) * LANES] = y[C:2 * C]

    for c in range(nc):
        advance(c)

    for s in range(PAIRS):
        sl = slice(s * LANES, (s + 1) * LANES)
        o = o_scr[:, sl]
        d = o - _seg_sum(o, e) * (1.0 / RWKV_HD)
        var = _seg_sum(d * d, e) * (1.0 / RWKV_HD)
        y = d * lax.rsqrt(var + GN_EPS) * lng_ref[:, sl] + lnb_ref[:, sl]
        bonus = _seg_sum(r_scr[:, sl] * k_scr[:, sl] * rk_ref[:, sl], e) * v_scr[:, sl]
        o_ref[:, :, sl] = ((y + bonus) * g_scr[:, sl]).astype(BF16).reshape(G, L, LANES)

    @pl.when(j == pl.num_programs(1) - 1)
    def _():
        for g in range(G):
            for s in range(PAIRS):
                sp = st_scr[g, s].T
                sout_ref[g, 2 * s] = sp[0:RWKV_HD, 0:RWKV_HD]
                sout_ref[g, 2 * s + 1] = sp[RWKV_HD:, RWKV_HD:]


def _rwkv(p, shift_in, s0, lw, e, G, L, stacked=None):
    B, T, _ = p.shape
    l = lw["layer"]
    n_in = 15
    body = _rwkv_kernel if stacked is None else (lambda *refs: _rwkv_kernel(*refs[:n_in], *refs[n_in + 1:]))
    carried = () if stacked is None else (stacked,)
    R = G * L
    vec = lambda c: _const_spec((1, c))
    buf = lambda: pltpu.VMEM((R, RWKV_W), F32)
    per_chunk = lambda rows, dt: pltpu.VMEM((R // CHUNK, PAIRS, rows, LANES), dt)
    state = pl.BlockSpec((G, RWKV_HEADS, RWKV_HD, RWKV_HD), lambda i, j: (i, 0, 0, 0))
    return pl.pallas_call(
        body,
        grid=(B // G, T // L),
        input_output_aliases={} if stacked is None else {n_in: 1},
        in_specs=[pl.BlockSpec((G, L, RWKV_COLS), lambda i, j: (i, j, 0)),
                  pl.BlockSpec((G, 1, RWKV_COLS), lambda i, j: (i, 0, 0)),
                  state,
                  vec(RWKV_COLS), vec(RWKV_W), _const_spec((LANES, RWKV_W)), vec(RWKV_W),
                  _const_spec((LANES, RWKV_W)), _const_spec((LANES, RWKV_W)), vec(RWKV_W), vec(RWKV_W),
                  vec(RWKV_W), vec(RWKV_W), vec(RWKV_W), _const_spec((LANES, LANES))]
                 + [pl.BlockSpec(memory_space=pl.ANY)] * len(carried),
        out_specs=[pl.BlockSpec((G, L, RWKV_W), lambda i, j: (i, j, 0)),
                   pl.BlockSpec((None, G, RWKV_HEADS, RWKV_HD, RWKV_HD), lambda i, j: (l, i, 0, 0, 0))],
        out_shape=[jax.ShapeDtypeStruct((B, T, RWKV_W), BF16),
                   jax.ShapeDtypeStruct((DEPTH, B, RWKV_HEADS, RWKV_HD, RWKV_HD), F32)],
        scratch_shapes=[pltpu.VMEM((G, 1, RWKV_COLS), F32), pltpu.VMEM((G, PAIRS, LANES, LANES), F32)]
                       + [buf() for _ in range(8)]
                       + [per_chunk(LANES, BF16), per_chunk(LANES, F32), per_chunk(LANES, BF16),
                          per_chunk(CHUNK, BF16), per_chunk(LANES, F32)],
        compiler_params=_params(),
        name="rwkv7",
    )(p, shift_in, s0, lw["mu"], lw["w0"], lw["w2p"], lw["a0"], lw["a2p"], lw["g2"], lw["kk"],
      lw["ka"], lw["rk"], lw["ln_g"], lw["ln_b"], e, *carried)


def _mem_kv_kernel(m_ref, g_ref, w_ref, kng_ref, mk_ref, mv_ref):
    G, M, D = m_ref.shape
    x = m_ref[...].reshape(G * M, D)
    h = (x * lax.rsqrt(jnp.mean(x * x, axis=-1, keepdims=True) + RMS_EPS) * g_ref[...]).astype(BF16)
    for s in range(MEM_HEADS):
        z = jnp.dot(h, w_ref[:, s * MEM_HD:(s + 1) * MEM_HD], preferred_element_type=F32)
        z = z * lax.rsqrt(jnp.mean(z * z, axis=-1, keepdims=True) + RMS_EPS) * kng_ref[...]
        zv = jnp.dot(h, w_ref[:, MEM_W + s * MEM_HD:MEM_W + (s + 1) * MEM_HD], preferred_element_type=F32)
        for g in range(G):
            mk_ref[g, pl.ds(s, M, stride=MEM_HEADS), :] = z[g * M:(g + 1) * M]
            mv_ref[g, pl.ds(s, M, stride=MEM_HEADS), :] = zv[g * M:(g + 1) * M]


def _mem_kv(mem, g, w_b, l, kng):
    B, M, D = mem.shape
    heads = pl.BlockSpec((1, M * MEM_HEADS, MEM_HD), lambda i, j: (i, 0, 0))
    return pl.pallas_call(
        _mem_kv_kernel,
        grid=(B, 1),
        in_specs=[pl.BlockSpec((1, M, D), lambda i, j: (i, 0, 0)), _const_spec((1, D)),
                  _layer_spec((D, 2 * MEM_W), l), _const_spec((1, MEM_HD))],
        out_specs=[heads, heads],
        out_shape=[jax.ShapeDtypeStruct((B, M * MEM_HEADS, MEM_HD), F32)] * 2,
        compiler_params=_params(),
        name="mem_kv",
    )(mem, g, w_b, kng)


def _mix_kernel(sink_ref, x_ref, oa_ref, q_ref, k_ref, v_ref, hk_ref, hv_ref, qm_ref, mk_ref, mv_ref,
                g1_ref, wg_ref, wb_ref, wo_ref, y_ref, kd, vd, kb, vb, ob_scr, om_scr, *, has_cache):
    G, L, D = x_ref.shape
    R = G * L
    C = CHUNK
    nc = L // C
    KB = WINDOW + C
    GROUP = 4
    j = pl.program_id(1)
    lo = lax.broadcasted_iota(jnp.int32, (1, LANES), 1) < SWA_HD
    qi = lax.broadcasted_iota(jnp.int32, (C, KB), 0)
    kj = lax.broadcasted_iota(jnp.int32, (C, KB), 1)
    dist = jnp.abs(WINDOW + qi - kj).astype(F32)
    kj4 = lax.broadcasted_iota(jnp.int32, (GROUP * C, KB), 1)
    ones = jnp.ones((WINDOW + L, LANES), F32)
    neg_pad = jnp.full((GROUP * C, 2 * LANES - KB), -jnp.inf, F32)
    bias = [jnp.concatenate([(2.0 ** -(GROUP * kv + h + 1)) * dist for h in range(GROUP)], axis=0)
            for kv in range(2)]
    sink = [jnp.concatenate([jnp.full((C, LANES), sink_ref[GROUP * kv + h], F32) for h in range(GROUP)], axis=0)
            for kv in range(2)]
    for g in range(G):
        kc = jnp.concatenate([hk_ref[g], k_ref[g]], axis=0)
        vc = jnp.concatenate([hv_ref[g], v_ref[g]], axis=0)
        ks = pltpu.roll(kc, SWA_HD, 1)
        vs = pltpu.roll(vc, SWA_HD, 1)
        kd[g, 0] = jnp.where(lo, kc, ks).astype(BF16)
        kd[g, 1] = jnp.where(lo, ks, kc).astype(BF16)
        vd[g, 0] = jnp.concatenate([jnp.where(lo, vc, vs), ones], axis=1).astype(BF16)
        vd[g, 1] = jnp.concatenate([jnp.where(lo, vs, vc), ones], axis=1).astype(BF16)

    mem_units = [(g, h) for g in range(G) for h in range(MEM_HEADS)]
    msl = lambda h: slice(h * MEM_HD, (h + 1) * MEM_HD)

    @pl.when(j == 0)
    def _():
        for g, h in mem_units:
            kb[g, h] = mk_ref[g, pl.ds(h, MEM_TOKENS, stride=MEM_HEADS), :].astype(BF16)
            vb[g, h] = mv_ref[g, pl.ds(h, MEM_TOKENS, stride=MEM_HEADS), :].astype(BF16)

    x = x_ref[...].reshape(R, D)
    hb = (x * lax.rsqrt(jnp.mean(x * x, axis=-1, keepdims=True) + RMS_EPS) * g1_ref[...]).astype(BF16)
    todo = list(range(N_BRANCH))
    gates = []

    def next_gate():
        if todo:
            n = todo.pop(0)
            gates.append(jnp.dot(hb, wg_ref[:, n * D:(n + 1) * D], preferred_element_type=F32))

    def swa_scores(chunk_ids):
        units = [(i // nc, (i % nc) * C, kv) for i in chunk_ids for kv in range(2)]
        scs = []
        for g, off, kv in units:
            q0 = q_ref[g, off:off + C, (2 * kv) * LANES:(2 * kv + 1) * LANES]
            q1 = q_ref[g, off:off + C, (2 * kv + 1) * LANES:(2 * kv + 2) * LANES]
            zero = jnp.zeros_like(q0)
            lhs = jnp.concatenate([jnp.where(lo, q0, zero), jnp.where(lo, zero, q0),
                                   jnp.where(lo, q1, zero), jnp.where(lo, zero, q1)], axis=0)
            sc = lax.dot_general(lhs, kd[g, kv, off:off + KB, :], ((_NT), ((), ())),
                                 preferred_element_type=F32) - bias[kv]
            if not has_cache and off < WINDOW:
                sc = jnp.where(jnp.logical_and(j == 0, kj4 + off < WINDOW), -jnp.inf, sc)
            scs.append(sc)
        return units, scs

    def swa_finish(units, scs):
        exs, sink_terms = [], []
        for (g, off, kv), sc in zip(units, scs):
            folded = jnp.maximum(sc[:, :LANES], jnp.concatenate([sc[:, LANES:], neg_pad], axis=1))
            mx = jnp.maximum(jnp.broadcast_to(jnp.max(folded, axis=-1, keepdims=True), (GROUP * C, LANES)),
                             sink[kv])
            ex = jnp.concatenate([jnp.exp(sc[:, :LANES] - mx), jnp.exp(sc[:, LANES:] - mx[:, :KB - LANES])],
                                 axis=1)
            exs.append(ex.astype(BF16))
            sink_terms.append(jnp.exp(sink[kv] - mx))
        for (g, off, kv), ex, st in zip(units, exs, sink_terms):
            pvd = jnp.dot(ex, vd[g, kv, off:off + KB, :], preferred_element_type=F32)
            pv = (pvd[:, :LANES] / (pvd[:, LANES:] + st)).astype(BF16)
            ob_scr[g, off:off + C, (2 * kv) * LANES:(2 * kv + 1) * LANES] = jnp.where(lo, pv[0:C], pv[C:2 * C])
            ob_scr[g, off:off + C, (2 * kv + 1) * LANES:(2 * kv + 2) * LANES] = jnp.where(
                lo, pv[2 * C:3 * C], pv[3 * C:4 * C])

    for c0 in range(0, G * nc, UNROLL):
        units, scs = swa_scores(range(c0, c0 + UNROLL))
        next_gate()
        swa_finish(units, scs)

    mones = jnp.ones((MEM_TOKENS, MEM_HD), BF16)
    mscs = [lax.dot_general(qm_ref[g, :, msl(h)], kb[g, h], (_NT, ((), ())), preferred_element_type=F32)
            for g, h in mem_units]
    while todo:
        next_gate()
    mexs = [jnp.exp(sc - jnp.max(sc, axis=-1, keepdims=True)).astype(BF16) for sc in mscs]
    for (g, h), ex in zip(mem_units, mexs):
        den = jnp.dot(ex, mones, preferred_element_type=F32)
        pv = jnp.dot(ex, vb[g, h], preferred_element_type=F32)
        om_scr[g, :, msl(h)] = (pv / den).astype(BF16)

    mix = None
    for n, o_ref in enumerate((oa_ref, ob_scr, om_scr)):
        br = jnp.dot(o_ref[...].reshape(R, RWKV_W), wb_ref[n], preferred_element_type=F32)
        t = _sigmoid(gates[n]) * br
        mix = t if mix is None else mix + t
    y_ref[...] = (x + jnp.dot(mix.astype(BF16), wo_ref[...], preferred_element_type=F32)).reshape(G, L, D)


def _mix(x, oa, q, k, v, halo_k, halo_v, sink, qm, mk, mv, mem_layer, g1, w_in_b, wb_b, wo_b, l, G, L, has_cache):
    B, T, D = x.shape
    if has_cache:
        halo = pl.BlockSpec((G, WINDOW, LANES), lambda i, j: (i, 0, 0))
    else:
        per = L // WINDOW
        halo = pl.BlockSpec((G, WINDOW, LANES), lambda i, j: (i, jnp.maximum(j * per - 1, 0), 0))
    tile = lambda c: pl.BlockSpec((G, L, c), lambda i, j: (i, j, 0))
    if mem_layer is None:
        mem = pl.BlockSpec((G, MEM_TOKENS * MEM_HEADS, MEM_HD), lambda i, j: (i, 0, 0))
    else:
        mem = pl.BlockSpec((None, G, MEM_TOKENS * MEM_HEADS, MEM_HD), lambda i, j: (mem_layer, i, 0, 0))
    cat = lambda cols: pltpu.VMEM((G, 2, WINDOW + L, cols), BF16)
    gathered = pltpu.VMEM((G, MEM_HEADS, MEM_TOKENS, MEM_HD), BF16)
    branch = pltpu.VMEM((G, L, RWKV_W), BF16)
    return pl.pallas_call(
        functools.partial(_mix_kernel, has_cache=has_cache),
        grid=(B // G, T // L),
        in_specs=[pl.BlockSpec(memory_space=pltpu.SMEM), tile(D), tile(RWKV_W), tile(SWA_HEADS * SWA_HD),
                  tile(LANES), tile(LANES), halo, halo, tile(MEM_W), mem, mem, _const_spec((1, D)),
                  pl.BlockSpec((None, D, N_BRANCH * D), lambda *_: (l, 0, C_GT // (N_BRANCH * D)),
                               pipeline_mode=pl.Buffered(1)),
                  _layer_spec((N_BRANCH, RWKV_W, D), l), _layer_spec((D, D), l)],
        out_specs=tile(D),
        out_shape=jax.ShapeDtypeStruct((B, T, D), F32),
        scratch_shapes=[cat(LANES), cat(2 * LANES), gathered, gathered, branch, branch],
        compiler_params=_params(),
        name="mix",
    )(sink, x, oa, q, k, v, halo_k, halo_v, qm, mk, mv, g1, w_in_b, wb_b, wo_b)


def _ffn_kernel(x_ref, cin_ref, g2_ref, wu_ref, cw_ref, cb_ref, wd_ref, y_ref, cout_ref, carry):
    G, L, D = x_ref.shape
    R = G * L
    j = pl.program_id(1)

    @pl.when(j == 0)
    def _():
        carry[...] = cin_ref[...]

    x = x_ref[...].reshape(R, D)
    hb = (x * lax.rsqrt(jnp.mean(x * x, axis=-1, keepdims=True) + RMS_EPS) * g2_ref[...]).astype(BF16)
    row = lax.broadcasted_iota(jnp.int32, (L, 1), 0)
    acc = x
    for c0, c1 in FF_BLOCKS:
        cs = slice(c0, c1)
        a_in = jnp.dot(hb, wu_ref[:, cs], preferred_element_type=F32)
        u = jnp.dot(hb, wu_ref[:, D_FF + c0:D_FF + c1], preferred_element_type=F32)
        convs = []
        for g in range(G):
            a = a_in[g * L:(g + 1) * L]
            prev = carry[g, :, cs]
            a1 = jnp.where(row == 0, prev[1:2], pltpu.roll(a, 1, 0))
            a2 = jnp.where(row == 0, prev[0:1], jnp.where(row == 1, prev[1:2], pltpu.roll(a, 2, 0)))
            carry[g, :, cs] = a[L - 2:L]
            convs.append(cb_ref[:, cs] + a2 * cw_ref[0:1, cs] + a1 * cw_ref[1:2, cs] + a * cw_ref[2:3, cs])
        c = convs[0] if G == 1 else jnp.concatenate(convs, axis=0)
        gelu = 0.5 * c * (1.0 + jnp.tanh(0.7978845608028654 * (c + 0.044715 * (c * c * c))))
        acc = acc + jnp.dot((gelu * u).astype(BF16), wd_ref[cs, :], preferred_element_type=F32)
    y_ref[...] = acc.reshape(G, L, D)

    @pl.when(j == pl.num_programs(1) - 1)
    def _():
        cout_ref[...] = carry[...]


def _ffn(x, conv_in, g2, wu_b, cw, cb, wd_b, l, G, L):
    B, T, D = x.shape
    tile = pl.BlockSpec((G, L, D), lambda i, j: (i, j, 0))
    st = pl.BlockSpec((G, CONV_W - 1, D_FF), lambda i, j: (i, 0, 0))
    return pl.pallas_call(
        _ffn_kernel,
        grid=(B // G, T // L),
        in_specs=[tile, st, _const_spec((1, D)), _layer_spec((D, 2 * D_FF), l), _const_spec((CONV_W, D_FF)),
                  _const_spec((1, D_FF)), _layer_spec((D_FF, D), l)],
        out_specs=[tile, st],
        out_shape=[jax.ShapeDtypeStruct((B, T, D), F32), jax.ShapeDtypeStruct((B, CONV_W - 1, D_FF), F32)],
        scratch_shapes=[pltpu.VMEM((G, CONV_W - 1, D_FF), F32)],
        compiler_params=_params(),
        name="conv_ffn",
    )(x, conv_in, g2, wu_b, cw, cb, wd_b)


def _layer(x, lw, e, mk, mv, mem_layer, shift_in, s0, conv_in, halo_k, halo_v, tiles, states):
    has_cache = halo_k is not None
    p, q, k, v, qm = _in_proj(x, lw["norm1_g"], lw["w_in"], lw["layer"], lw["qn_g"], lw["kn_g"], lw["mqn_g"], e,
                              *tiles["in"])
    oa, s_new = _rwkv(p, shift_in, s0, lw, e, *tiles["rwkv"], states)
    hk, hv = (halo_k, halo_v) if has_cache else (k, v)
    x = _mix(x, oa, q, k, v, hk, hv, lw["sink"], qm, mk, mv, mem_layer, lw["norm1_g"], lw["w_in"],
             lw["w_branch"], lw["w_out"], lw["layer"], *tiles["dense"], has_cache)
    x, conv_new = _ffn(x, conv_in, lw["norm2_g"], lw["w_up"], lw["conv_w"], lw["conv_b"], lw["w_down"], lw["layer"],
                       *tiles["dense"])
    return x, (k, v, s_new, p[:, -1:, :], conv_new)


def kernel(x_prompt, x_sample, cache_swa_k, cache_swa_v, cache_mem_k, cache_mem_v, state_rwkv, state_shift, state_conv, mem_prompt, norm1_g, w_in, rwkv_mu, rwkv_w0, rwkv_w2, rwkv_a0, rwkv_a2, rwkv_g2, rwkv_kk, rwkv_ka, rwkv_rk, rwkv_ln_g, rwkv_ln_b, swa_qn_g, swa_kn_g, swa_sink, mem_norm_g, w_mem_kv, mem_qn_g, mem_kn_g, w_branch, w_out, norm2_g, w_up, conv_w, conv_b, w_down):
    Bp, Tp, _ = x_prompt.shape
    Bs, Ts, _ = x_sample.shape
    dt = x_prompt.dtype
    half = jnp.arange(LANES) // RWKV_HD
    e = (half[:, None] == half[None, :]).astype(BF16)
    row = lambda a: a.reshape(1, -1)
    zpad = jnp.zeros((LANES - 64, RWKV_W), dt)

    w_in_b, w_branch_b, w_out_b, w_up_b, w_down_b, w_mem_kv_b = (
        w.astype(BF16) for w in (w_in, w_branch, w_out, w_up, w_down, w_mem_kv))
    mem_k_rows = cache_mem_k.reshape(DEPTH, Bs, MEM_TOKENS * MEM_HEADS, MEM_HD)
    mem_v_rows = cache_mem_v.reshape(DEPTH, Bs, MEM_TOKENS * MEM_HEADS, MEM_HD)
    yp, ys = x_prompt, x_sample
    outs_p = [[] for _ in range(7)]
    outs_s = [[] for _ in range(5)]
    rw_p = rw_s = None
    prompt_tiles = {"in": (1, 1024), "dense": (1, 512), "rwkv": (Bp, 256)}
    sample_tiles = {"in": (8, Ts), "dense": (8, Ts), "rwkv": (8, Ts)}
    for l in range(DEPTH):
        lw = {
            "layer": l, "norm1_g": row(norm1_g[l]), "w_in": w_in_b,
            "qn_g": row(jnp.tile(swa_qn_g[l], 2)) * (SWA_HD ** -0.5), "kn_g": row(jnp.tile(swa_kn_g[l], 2)),
            "mqn_g": row(mem_qn_g[l]) * (MEM_HD ** -0.5),
            "mu": row(rwkv_mu[l]), "w0": row(rwkv_w0[l]),
            "w2p": jnp.concatenate([rwkv_w2[l], zpad], axis=0),
            "a0": row(rwkv_a0[l]),
            "a2p": jnp.concatenate([zpad, rwkv_a2[l]], axis=0),
            "g2": rwkv_g2[l], "kk": row(rwkv_kk[l]), "ka": row(rwkv_ka[l]), "rk": row(rwkv_rk[l]),
            "ln_g": row(rwkv_ln_g[l]), "ln_b": row(rwkv_ln_b[l]),
            "sink": swa_sink[l],
            "w_branch": w_branch_b, "w_out": w_out_b,
            "norm2_g": row(norm2_g[l]), "w_up": w_up_b, "conv_w": conv_w[l],
            "conv_b": row(conv_b[l]), "w_down": w_down_b,
        }
        mk, mv = _mem_kv(mem_prompt, row(mem_norm_g[l]), w_mem_kv_b, l, row(mem_kn_g[l]))
        yp, (k, v, s_new, sh_new, cv_new) = _layer(
            yp, lw, e, mk, mv, None,
            jnp.zeros((Bp, 1, RWKV_COLS), dt),
            jnp.zeros((Bp, RWKV_HEADS, RWKV_HD, RWKV_HD), dt),
            jnp.zeros((Bp, CONV_W - 1, D_FF), dt), None, None, prompt_tiles, rw_p)
        rw_p = s_new
        kv_shape = (Bp, WINDOW, 2, SWA_HD)
        for lst, val in zip(outs_p, (k[:, -WINDOW:].reshape(kv_shape), v[:, -WINDOW:].reshape(kv_shape),
                                     mk.reshape(Bp, MEM_TOKENS, MEM_HEADS, MEM_HD),
                                     mv.reshape(Bp, MEM_TOKENS, MEM_HEADS, MEM_HD), s_new, sh_new, cv_new)):
            lst.append(val)
        ck = cache_swa_k[l].reshape(Bs, WINDOW, LANES)
        cv = cache_swa_v[l].reshape(Bs, WINDOW, LANES)
        ys, (k, v, s_new, sh_new, cv_new) = _layer(
            ys, lw, e, mem_k_rows, mem_v_rows, l,
            state_shift[l], state_rwkv[l], state_conv[l], ck, cv, sample_tiles, rw_s)
        rw_s = s_new
        kv_shape = (Bs, WINDOW, 2, SWA_HD)
        kf = jnp.concatenate([ck, k], axis=1)[:, -WINDOW:].reshape(kv_shape)
        vf = jnp.concatenate([cv, v], axis=1)[:, -WINDOW:].reshape(kv_shape)
        for lst, val in zip(outs_s, (kf, vf, s_new, sh_new, cv_new)):
            lst.append(val)
    res_p = [jnp.stack(o) for o in outs_p]
    res_s = [jnp.stack(o) for o in outs_s]
    res_p[4], res_s[2] = rw_p, rw_s
    return (yp, ys) + tuple(res_p) + tuple(res_s)
```
